```python
import math
import jax
import jax.numpy as jnp
from jax import lax
import numpy as np

D_MODEL = 1024
BATCH = 8
SEQ = 2048
DEPTH = 4
DEC_BATCH = 32
DEC_SEQ = 4
PAST_LEN = 16384
PAGE_SIZE = 128

N_BRANCH = 4
BRANCH_W = 512
SSM_HEADS = 8
SSM_HEAD_DIM = 64
SSM_GROUPS = 2
SSM_STATE = 128
CONV_W = 4
SSM_CONV_DIM = SSM_HEADS * SSM_HEAD_DIM + 2 * SSM_GROUPS * SSM_STATE
RET_HEADS = 4
RET_QK_DIM = 64
RET_V_DIM = 128
DIFF_HEADS = 4
DIFF_KV_HEADS = 2
DIFF_GROUP = DIFF_HEADS // DIFF_KV_HEADS
DIFF_HEAD_DIM = 64
DIFF_V_DIM = 2 * DIFF_HEAD_DIM
MLA_HEADS = 4
MLA_Q_RANK = 256
MLA_KV_RANK = 128
MLA_NOPE_DIM = 64
MLA_ROPE_DIM = 32
MLA_V_DIM = 128
REL_BUCKETS = 32
REL_MAX_EXACT = 16
REL_MAX_DIST = 128
D_FF = 2816
CHUNK = 128
QBLOCK = 128
ROPE_BASE = 10000.0
NORM_EPS = 1e-6

IN_SIZES = (
    BRANCH_W,
    SSM_CONV_DIM,
    SSM_HEADS,
    RET_HEADS * RET_QK_DIM,
    RET_HEADS * RET_QK_DIM,
    RET_HEADS * RET_V_DIM,
    BRANCH_W,
    DIFF_HEADS * 2 * DIFF_HEAD_DIM,
    DIFF_KV_HEADS * 2 * DIFF_HEAD_DIM,
    DIFF_KV_HEADS * DIFF_V_DIM,
    MLA_Q_RANK,
    MLA_KV_RANK + MLA_ROPE_DIM,
    N_BRANCH * D_MODEL,
)
IN_OFFSETS = tuple(int(o) for o in np.cumsum(IN_SIZES)[:-1])
D_IN = int(sum(IN_SIZES))

kernel_name = "hybrid_ssd_retention_diffattn_mla_decode_step"


def rmsnorm(x, g):
    xf = x.astype(jnp.float32)
    y = xf * lax.rsqrt(jnp.mean(xf * xf, axis=-1, keepdims=True) + NORM_EPS)
    return (y * g.astype(jnp.float32)).astype(x.dtype)


def head_groupnorm(y, g):
    yf = y.astype(jnp.float32)
    mu = jnp.mean(yf, axis=-1, keepdims=True)
    var = jnp.mean(jnp.square(yf - mu), axis=-1, keepdims=True)
    return ((yf - mu) * lax.rsqrt(var + NORM_EPS) * g.astype(jnp.float32)).astype(y.dtype)


def swiglu(x, w1, w3, w2):
    return (jax.nn.silu(x @ w1) * (x @ w3)) @ w2


def rotary(x, pos):
    half = x.shape[-1] // 2
    inv_freq = ROPE_BASE ** (-jnp.arange(half, dtype=jnp.float32) / half)
    ang = pos.astype(jnp.float32)[:, None] * inv_freq[None, :]
    cos = jnp.cos(ang)[None, :, None, :]
    sin = jnp.sin(ang)[None, :, None, :]
    xf = x.astype(jnp.float32)
    x1, x2 = xf[..., :half], xf[..., half:]
    return jnp.concatenate([x1 * cos - x2 * sin, x1 * sin + x2 * cos], axis=-1).astype(x.dtype)


def rel_bucket(dist):
    d = jnp.maximum(dist, 0)
    large = REL_MAX_EXACT + (jnp.log(jnp.maximum(d, 1).astype(jnp.float32) / REL_MAX_EXACT)
                             / math.log(REL_MAX_DIST / REL_MAX_EXACT)
                             * (REL_BUCKETS - REL_MAX_EXACT)).astype(jnp.int32)
    large = jnp.minimum(large, REL_BUCKETS - 1)
    return jnp.where(d < REL_MAX_EXACT, d, large)


def rel_bias_logits(table, q_pos, k_pos):
    dist = q_pos[:, None] - k_pos[None, :]
    bias = jnp.take(table.astype(jnp.float32), rel_bucket(dist), axis=0)
    bias = jnp.where((dist >= 0)[..., None], bias, -jnp.inf)
    return jnp.moveaxis(bias, -1, 0)


def chunked_linear_recurrence(q, k, v, log_a, s0):
    Bsz, L, H, K = q.shape
    V = v.shape[-1]
    Q = CHUNK if L % CHUNK == 0 else L
    nc = L // Q

    def to_chunks(t):
        return jnp.moveaxis(t.astype(jnp.float32).reshape(Bsz, nc, Q, *t.shape[2:]), 1, 0)

    causal = jnp.tril(jnp.ones((Q, Q), dtype=bool))[None, :, :, None]

    def step(s, inp):
        qc, kc, vc, ac = inp
        cum = jnp.cumsum(ac, axis=1)
        seg = cum[:, :, None, :] - cum[:, None, :, :]
        decay = jnp.exp(jnp.where(causal, seg, -jnp.inf))
        scores = jnp.einsum("bihk,bjhk->bijh", qc, kc) * decay
        y = (jnp.einsum("bijh,bjhv->bihv", scores, vc)
             + jnp.einsum("bihk,bhkv->bihv", qc, s) * jnp.exp(cum)[..., None])
        tail = jnp.exp(cum[:, -1:, :] - cum)
        s_new = (s * jnp.exp(cum[:, -1, :])[:, :, None, None]
                 + jnp.einsum("bjhk,bjh,bjhv->bhkv", kc, tail, vc))
        return s_new, y

    s_fin, ys = lax.scan(step, s0.astype(jnp.float32),
                         (to_chunks(q), to_chunks(k), to_chunks(v), to_chunks(log_a)))
    y = jnp.moveaxis(ys, 0, 1).reshape(Bsz, L, H, V)
    return y.astype(q.dtype), s_fin.astype(s0.dtype)


def map_query_blocks(fn, qs):
    Bsz, L = qs[0].shape[:2]
    nb = L // QBLOCK
    blocks = tuple(jnp.moveaxis(q.reshape(Bsz, nb, QBLOCK, *q.shape[2:]), 1, 0) for q in qs)
    out = lax.map(lambda args: fn(args[0] * QBLOCK, *args[1]), (jnp.arange(nb, dtype=jnp.int32), blocks))
    return jnp.moveaxis(out, 0, 1).reshape(Bsz, L, *out.shape[3:])


def diff_attention(q, k_parts, v_parts, bias_parts, lam):
    Tq = q.shape[1]
    scale = DIFF_HEAD_DIM ** -0.5
    s = jnp.concatenate(
        [jnp.einsum("bqhgmd,bkhmd->bhgmqk", q, k).astype(jnp.float32) * scale
         + b.reshape(DIFF_KV_HEADS, DIFF_GROUP, 1, Tq, b.shape[-1])
         for k, b in zip(k_parts, bias_parts)], axis=-1)
    p = jax.nn.softmax(s, axis=-1)
    a = (p[:, :, :, 0] - lam * p[:, :, :, 1]).astype(q.dtype)
    out = None
    off = 0
    for v in v_parts:
        n = v.shape[1]
        o = jnp.einsum("bhgqk,bkhv->bqhgv", a[..., off:off + n], v)
        out = o if out is None else out + o
        off += n
    return out


def mla_attention(q_lat, q_rope, c_parts, kr_parts, mask_parts):
    scale = (MLA_NOPE_DIM + MLA_ROPE_DIM) ** -0.5
    s = jnp.concatenate(
        [(jnp.einsum("bqhr,bkr->bhqk", q_lat, c) + jnp.einsum("bqhd,bkd->bhqk", q_rope, kr)).astype(jnp.float32) * scale + m
         for c, kr, m in zip(c_parts, kr_parts, mask_parts)], axis=-1)
    p = jax.nn.softmax(s, axis=-1).astype(q_lat.dtype)
    out = None
    off = 0
    for c in c_parts:
        n = c.shape[1]
        o = jnp.einsum("bhqk,bkr->bqhr", p[..., off:off + n], c)
        out = o if out is None else out + o
        off += n
    return out


def trunk_layer(x, pos, l, W, conv_buf, ssm_state, ret_state, past):
    Bsz, L, _ = x.shape
    f32 = jnp.float32
    x = x + 0.5 * swiglu(rmsnorm(x, W["norm_ffn1"][l]), W["ffn1_w1"][l], W["ffn1_w3"][l], W["ffn1_w2"][l])
    h = rmsnorm(x, W["norm_mix"][l])
    (ssm_z, ssm_xbc, ssm_dt, ret_q, ret_k, ret_v, ret_g, diff_q, diff_k, diff_v,
     mla_q, mla_kv, gate_logits) = jnp.split(h @ W["w_in"][l], IN_OFFSETS, axis=-1)

    xpad = jnp.concatenate([conv_buf.astype(ssm_xbc.dtype), ssm_xbc], axis=1)
    new_conv_buf = xpad[:, xpad.shape[1] - (CONV_W - 1):]
    conv = lax.conv_general_dilated(xpad, W["ssm_conv_w"][l][:, None, :].astype(xpad.dtype), (1,), "VALID",
                                    dimension_numbers=("NWC", "WIO", "NWC"),
                                    feature_group_count=SSM_CONV_DIM)
    u = jax.nn.silu(conv + W["ssm_conv_b"][l])
    xs, b_ssm, c_ssm = jnp.split(u, [BRANCH_W, BRANCH_W + SSM_GROUPS * SSM_STATE], axis=-1)
    xs = xs.reshape(Bsz, L, SSM_HEADS, SSM_HEAD_DIM)
    rep = SSM_HEADS // SSM_GROUPS
    b_ssm = jnp.repeat(b_ssm.reshape(Bsz, L, SSM_GROUPS, SSM_STATE), rep, axis=2)
    c_ssm = jnp.repeat(c_ssm.reshape(Bsz, L, SSM_GROUPS, SSM_STATE), rep, axis=2)
    dt = jax.nn.softplus((ssm_dt + W["ssm_dt_bias"][l]).astype(f32))
    log_a = -jnp.exp(W["ssm_a_log"][l].astype(f32)) * dt
    y_ssm, new_ssm = chunked_linear_recurrence(c_ssm, b_ssm, xs.astype(f32) * dt[..., None], log_a, ssm_state)
    y_ssm = y_ssm + xs * W["ssm_d"][l][:, None]
    y_ssm = (y_ssm.reshape(Bsz, L, BRANCH_W) * jax.nn.silu(ssm_z)).reshape(Bsz, L, SSM_GROUPS, BRANCH_W // SSM_GROUPS)
    y_ssm = rmsnorm(y_ssm, W["ssm_norm"][l].reshape(SSM_GROUPS, BRANCH_W // SSM_GROUPS)).reshape(Bsz, L, BRANCH_W)

    rq = rotary(ret_q.reshape(Bsz, L, RET_HEADS, RET_QK_DIM), pos)
    rk = rotary(ret_k.reshape(Bsz, L, RET_HEADS, RET_QK_DIM), pos) * (RET_QK_DIM ** -0.5)
    rv = ret_v.reshape(Bsz, L, RET_HEADS, RET_V_DIM)
    log_gamma = jnp.log1p(-jnp.exp2(-5.0 - jnp.arange(RET_HEADS, dtype=f32)))
    y_ret, new_ret = chunked_linear_recurrence(rq, rk, rv, jnp.broadcast_to(log_gamma, (Bsz, L, RET_HEADS)), ret_state)
    y_ret = head_groupnorm(y_ret, W["ret_norm"][l].reshape(RET_HEADS, RET_V_DIM)).reshape(Bsz, L, BRANCH_W)
    y_ret = jax.nn.silu(ret_g) * y_ret

    dq = diff_q.reshape(Bsz, L, DIFF_KV_HEADS, DIFF_GROUP, 2, DIFF_HEAD_DIM)
    dk = diff_k.reshape(Bsz, L, DIFF_KV_HEADS, 2, DIFF_HEAD_DIM)
    dv = diff_v.reshape(Bsz, L, DIFF_KV_HEADS, DIFF_V_DIM)
    lam_init = 0.8 - 0.6 * math.exp(-0.3 * l)
    lw = W["diff_lambda"][l].astype(f32)
    lam = jnp.exp(jnp.sum(lw[0] * lw[1])) - jnp.exp(jnp.sum(lw[2] * lw[3])) + lam_init

    cq = rmsnorm(mla_q, W["mla_q_norm"][l])
    qh = jnp.einsum("bqr,rhd->bqhd", cq, W["mla_w_q_up"][l])
    q_nope = qh[..., :MLA_NOPE_DIM]
    q_rope = rotary(qh[..., MLA_NOPE_DIM:], pos)
    c_kv = rmsnorm(mla_kv[..., :MLA_KV_RANK], W["mla_kv_norm"][l])
    k_rope = rotary(mla_kv[..., MLA_KV_RANK:][:, :, None, :], pos)[:, :, 0]
    q_lat = jnp.einsum("bqhd,rhd->bqhr", q_nope, W["mla_w_uk"][l])

    table = W["rel_bias"]
    if past is None:
        k_pos = jnp.arange(L, dtype=jnp.int32)

        def diff_block(start, qb):
            q_pos = start + jnp.arange(QBLOCK, dtype=jnp.int32)
            return diff_attention(qb, (dk,), (dv,), (rel_bias_logits(table, q_pos, k_pos),), lam)

        def mla_block(start, qlb, qrb):
            q_pos = start + jnp.arange(QBLOCK, dtype=jnp.int32)
            mask = jnp.where(q_pos[:, None] >= k_pos[None, :], 0.0, -jnp.inf).astype(f32)
            return mla_attention(qlb, qrb, (c_kv,), (k_rope,), (mask,))

        o_diff = map_query_blocks(diff_block, (dq,))
        o_mla = map_query_blocks(mla_block, (q_lat, q_rope))
    else:
        k_past, v_past, c_past, kr_past = past
        past_pos = jnp.arange(k_past.shape[1], dtype=jnp.int32)
        o_diff = diff_attention(dq, (k_past, dk), (v_past, dv),
                                (rel_bias_logits(table, pos, past_pos), rel_bias_logits(table, pos, pos)), lam)
        mask_past = jnp.zeros((L, c_past.shape[1]), f32)
        mask_new = jnp.where(pos[:, None] >= pos[None, :], 0.0, -jnp.inf).astype(f32)
        o_mla = mla_attention(q_lat, q_rope, (c_past, c_kv), (kr_past, k_rope), (mask_past, mask_new))

    y_diff = (rmsnorm(o_diff, W["diff_subln"][l]) * (1.0 - lam_init)).reshape(Bsz, L, BRANCH_W)
    y_mla = jnp.einsum("bqhr,rhv->bqhv", o_mla, W["mla_w_uv"][l]).reshape(Bsz, L, BRANCH_W)

    branches = jnp.stack([y_ssm, y_ret, y_diff, y_mla], axis=2)
    proj = jnp.einsum("blnc,ncd->blnd", branches, W["w_branch"][l])
    gates = jax.nn.sigmoid(gate_logits.reshape(Bsz, L, N_BRANCH, D_MODEL))
    x = x + jnp.sum(gates * proj, axis=2) @ W["w_out"][l]

    x = x + 0.5 * swiglu(rmsnorm(x, W["norm_ffn2"][l]), W["ffn2_w1"][l], W["ffn2_w3"][l], W["ffn2_w2"][l])
    return x, (dk, dv, c_kv, k_rope, new_conv_buf, new_ssm, new_ret)


def setup_inputs(seed: int = 0) -> dict:
    key = jax.random.key(seed)
    ks = iter(jax.random.split(key, 64))
    f32 = jnp.float32

    def nrm(shape, scale):
        return jax.random.normal(next(ks), shape, f32) * scale

    def gain(shape):
        return 1.0 + nrm(shape, 0.05)

    n_pages = PAST_LEN // PAGE_SIZE
    n_used = DEC_BATCH * n_pages
    n_pool = n_used + (n_used + 3) // 4
    page_table = jax.random.permutation(next(ks), n_pool)[:n_used].reshape(DEC_BATCH, n_pages).astype(jnp.int32)
    dt0 = jnp.exp(jax.random.uniform(next(ks), (DEPTH, SSM_HEADS), f32, math.log(1e-3), math.log(1e-1)))
    ssm_dt_bias = dt0 + jnp.log(-jnp.expm1(-dt0))
    ssm_a_log = jnp.log(jax.random.uniform(next(ks), (DEPTH, SSM_HEADS), f32, 1.0, 16.0))
    return {
        "x_prompt": nrm((BATCH, SEQ, D_MODEL), 1.0),
        "x_sample": nrm((DEC_BATCH, DEC_SEQ, D_MODEL), 1.0),
        "cache_diff_k": nrm((n_pool, DEPTH, PAGE_SIZE, DIFF_KV_HEADS, 2, DIFF_HEAD_DIM), 1.0),
        "cache_diff_v": nrm((n_pool, DEPTH, PAGE_SIZE, DIFF_KV_HEADS, DIFF_V_DIM), 1.0),
        "cache_mla_latent": nrm((n_pool, DEPTH, PAGE_SIZE, MLA_KV_RANK), 1.0),
        "cache_mla_krope": nrm((n_pool, DEPTH, PAGE_SIZE, MLA_ROPE_DIM), 1.0),
        "state_ssm_conv": nrm((DEC_BATCH, DEPTH, CONV_W - 1, SSM_CONV_DIM), 1.0),
        "state_ssm": nrm((DEC_BATCH, DEPTH, SSM_HEADS, SSM_STATE, SSM_HEAD_DIM), 0.1),
        "state_retention": nrm((DEC_BATCH, DEPTH, RET_HEADS, RET_QK_DIM, RET_V_DIM), 0.1),
        "page_table": page_table,
        "norm_ffn1": gain((DEPTH, D_MODEL)),
        "ffn1_w1": nrm((DEPTH, D_MODEL, D_FF), D_MODEL ** -0.5),
        "ffn1_w3": nrm((DEPTH, D_MODEL, D_FF), D_MODEL ** -0.5),
        "ffn1_w2": nrm((DEPTH, D_FF, D_MODEL), D_FF ** -0.5),
        "norm_mix": gain((DEPTH, D_MODEL)),
        "w_in": nrm((DEPTH, D_MODEL, D_IN), D_MODEL ** -0.5),
        "ssm_conv_w": nrm((DEPTH, CONV_W, SSM_CONV_DIM), CONV_W ** -0.5),
        "ssm_conv_b": nrm((DEPTH, SSM_CONV_DIM), 0.02),
        "ssm_dt_bias": ssm_dt_bias,
        "ssm_a_log": ssm_a_log,
        "ssm_d": gain((DEPTH, SSM_HEADS)),
        "ssm_norm": gain((DEPTH, BRANCH_W)),
        "ret_norm": gain((DEPTH, BRANCH_W)),
        "diff_lambda": nrm((DEPTH, 4, DIFF_HEAD_DIM), 0.1),
        "diff_subln": gain((DEPTH, DIFF_V_DIM)),
        "mla_q_norm": gain((DEPTH, MLA_Q_RANK)),
        "mla_w_q_up": nrm((DEPTH, MLA_Q_RANK, MLA_HEADS, MLA_NOPE_DIM + MLA_ROPE_DIM), MLA_Q_RANK ** -0.5),
        "mla_kv_norm": gain((DEPTH, MLA_KV_RANK)),
        "mla_w_uk": nrm((DEPTH, MLA_KV_RANK, MLA_HEADS, MLA_NOPE_DIM), MLA_KV_RANK ** -0.5),
        "mla_w_uv": nrm((DEPTH, MLA_KV_RANK, MLA_HEADS, MLA_V_DIM), MLA_KV_RANK ** -0.5),
        "rel_bias": nrm((REL_BUCKETS, DIFF_HEADS), 0.5),
        "w_branch": nrm((DEPTH, N_BRANCH, BRANCH_W, D_MODEL), BRANCH_W ** -0.5),
        "w_out": nrm((DEPTH, D_MODEL, D_MODEL), D_MODEL ** -0.5),
        "norm_ffn2": gain((DEPTH, D_MODEL)),
        "ffn2_w1": nrm((DEPTH, D_MODEL, D_FF), D_MODEL ** -0.5),
        "ffn2_w3": nrm((DEPTH, D_MODEL, D_FF), D_MODEL ** -0.5),
        "ffn2_w2": nrm((DEPTH, D_FF, D_MODEL), D_FF ** -0.5),
        "final_norm": gain((D_MODEL,)),
    }


def reference(x_prompt, x_sample, cache_diff_k, cache_diff_v, cache_mla_latent, cache_mla_krope,
              state_ssm_conv, state_ssm, state_retention, page_table,
              norm_ffn1, ffn1_w1, ffn1_w3, ffn1_w2, norm_mix, w_in,
              ssm_conv_w, ssm_conv_b, ssm_dt_bias, ssm_a_log, ssm_d, ssm_norm,
              ret_norm, diff_lambda, diff_subln,
              mla_q_norm, mla_w_q_up, mla_kv_norm, mla_w_uk, mla_w_uv,
              rel_bias, w_branch, w_out, norm_ffn2, ffn2_w1, ffn2_w3, ffn2_w2, final_norm):
    W = dict(norm_ffn1=norm_ffn1, ffn1_w1=ffn1_w1, ffn1_w3=ffn1_w3, ffn1_w2=ffn1_w2,
             norm_mix=norm_mix, w_in=w_in,
             ssm_conv_w=ssm_conv_w, ssm_conv_b=ssm_conv_b, ssm_dt_bias=ssm_dt_bias,
             ssm_a_log=ssm_a_log, ssm_d=ssm_d, ssm_norm=ssm_norm,
             ret_norm=ret_norm, diff_lambda=diff_lambda, diff_subln=diff_subln,
             mla_q_norm=mla_q_norm, mla_w_q_up=mla_w_q_up, mla_kv_norm=mla_kv_norm,
             mla_w_uk=mla_w_uk, mla_w_uv=mla_w_uv, rel_bias=rel_bias,
             w_branch=w_branch, w_out=w_out,
             norm_ffn2=norm_ffn2, ffn2_w1=ffn2_w1, ffn2_w3=ffn2_w3, ffn2_w2=ffn2_w2)

    Bp, Lp, _ = x_prompt.shape
    pos_p = jnp.arange(Lp, dtype=jnp.int32)
    conv0 = jnp.zeros((Bp, CONV_W - 1, SSM_CONV_DIM), x_prompt.dtype)
    ssm0 = jnp.zeros((Bp, SSM_HEADS, SSM_STATE, SSM_HEAD_DIM), x_prompt.dtype)
    ret0 = jnp.zeros((Bp, RET_HEADS, RET_QK_DIM, RET_V_DIM), x_prompt.dtype)
    xp = x_prompt
    rows_p = []
    for l in range(DEPTH):
        xp, new = trunk_layer(xp, pos_p, l, W, conv0, ssm0, ret0, None)
        rows_p.append(new)
    y_prompt = rmsnorm(xp, final_norm)
    p_diff_k, p_diff_v, p_latent, p_krope, p_conv, p_ssm, p_ret = [
        jnp.stack([r[i] for r in rows_p], axis=1) for i in range(7)]

    Bs, Ls, _ = x_sample.shape
    past_len = page_table.shape[1] * PAGE_SIZE
    pos_s = past_len + jnp.arange(Ls, dtype=jnp.int32)
    xs = x_sample
    rows_s = []
    for l in range(DEPTH):
        past = (cache_diff_k[page_table, l].reshape(Bs, past_len, DIFF_KV_HEADS, 2, DIFF_HEAD_DIM),
                cache_diff_v[page_table, l].reshape(Bs, past_len, DIFF_KV_HEADS, DIFF_V_DIM),
                cache_mla_latent[page_table, l].reshape(Bs, past_len, MLA_KV_RANK),
                cache_mla_krope[page_table, l].reshape(Bs, past_len, MLA_ROPE_DIM))
        xs, new = trunk_layer(xs, pos_s, l, W, state_ssm_conv[:, l], state_ssm[:, l], state_retention[:, l], past)
        rows_s.append(new)
    y_sample = rmsnorm(xs, final_norm)
    s_diff_k, s_diff_v, s_latent, s_krope, s_conv, s_ssm, s_ret = [
        jnp.stack([r[i] for r in rows_s], axis=1) for i in range(7)]

    return (y_prompt, y_sample,
            p_diff_k, p_diff_v, p_latent, p_krope, p_conv, p_ssm, p_ret,
            s_diff_k, s_diff_v, s_latent, s_krope, s_conv, s_ssm, s_ret)
```

```python
import functools
import math

import numpy as np
import jax
import jax.numpy as jnp
from jax import lax
from jax.experimental import pallas as pl
from jax.experimental.pallas import tpu as pltpu

F32 = jnp.float32
BF16 = jnp.bfloat16
NEG_INF = float("-inf")

D_MODEL = 1024
D_FF = 2816
BRANCH_W = 512
N_BRANCH = 4
SSM_HEADS = 8
SSM_HEAD_DIM = 64
SSM_STATE = 128
CONV_W = 4
SSM_CONV_DIM = 1024
RET_HEADS = 4
RET_QK = 64
RET_V = 128
DIFF_KVH = 2
DIFF_DH = 64
DIFF_DV = 128
MLA_HEADS = 4
MLA_Q_RANK = 256
MLA_KV_RANK = 128
MLA_NOPE = 64
MLA_ROPE = 32
PAGE = 128
REL_BUCKETS = 32
REL_MAX_EXACT = 16
REL_MAX_DIST = 128
ROPE_BASE = 10000.0
NORM_EPS = 1e-6
IN_SIZES = (512, 1024, 8, 256, 256, 512, 512, 512, 256, 256, 256, 160, 4096)

LANES = 128
CHUNK = 128
VMEM_LIMIT = 56 * 1024 * 1024


def _rel_lower_bounds():
    lb = list(range(REL_MAX_EXACT))
    d = np.arange(REL_MAX_EXACT, 4 * REL_MAX_DIST).astype(np.float32)
    large = REL_MAX_EXACT + (np.log(d / np.float32(REL_MAX_EXACT)) / np.float32(math.log(REL_MAX_DIST / REL_MAX_EXACT))
                             * np.float32(REL_BUCKETS - REL_MAX_EXACT)).astype(np.int32)
    large = np.minimum(large, REL_BUCKETS - 1)
    for bk in range(REL_MAX_EXACT, REL_BUCKETS):
        lb.append(int(REL_MAX_EXACT + np.argmax(large >= bk)))
    return tuple(lb)


REL_LB = _rel_lower_bounds()
REL_FAR = REL_LB[-1]
RET_LOG_GAMMA = tuple(math.log1p(-(2.0 ** (-5.0 - h))) for h in range(RET_HEADS))


def _dot(a, b, precision=None):
    return jnp.dot(a, b, preferred_element_type=F32, precision=precision)


def _dot_nt(a, b):
    return lax.dot_general(a, b, (((1,), (1,)), ((), ())), preferred_element_type=F32)


def _dot_tn(a, b):
    return lax.dot_general(a, b, (((0,), (0,)), ((), ())), preferred_element_type=F32)


def _rms(x):
    return x * lax.rsqrt(jnp.mean(x * x, axis=-1, keepdims=True) + NORM_EPS)


def _silu(x):
    return x / (1.0 + jnp.exp(-x))


def _softplus(x):
    return jnp.maximum(x, 0.0) + jnp.log1p(jnp.exp(-jnp.abs(x)))


def _pad_rows(x, rows):
    if x.shape[0] == rows:
        return x
    return jnp.concatenate([x, jnp.zeros((rows - x.shape[0],) + x.shape[1:], x.dtype)], axis=0)


def _rotate(x, cosf, sins, half):
    outs = []
    for s in range(x.shape[-1] // LANES):
        xs = x[:, s * LANES:(s + 1) * LANES]
        lane = lax.broadcasted_iota(jnp.int32, xs.shape, 1)
        first = (lane % (2 * half)) < half
        partner = jnp.where(first, pltpu.roll(xs, LANES - half, axis=1), pltpu.roll(xs, half, axis=1))
        outs.append(partner)
    partner = outs[0] if len(outs) == 1 else jnp.concatenate(outs, axis=-1)
    return x * cosf + partner * sins


def _full(shape):
    return pl.BlockSpec(shape, lambda *_: (0,) * len(shape))


def _resident(shape):
    return pl.BlockSpec(shape, lambda *_: (0,) * len(shape), pipeline_mode=pl.Buffered(1))


_SMEM = pl.BlockSpec(memory_space=pltpu.SMEM)


def _ffn_body(*refs, fc, final):
    if final:
        x_ref, g_ref, w1_ref, w3_ref, w2_ref, fg_ref, o_ref = refs
    else:
        x_ref, g_ref, w1_ref, w3_ref, w2_ref, o_ref = refs
    x = x_ref[...]
    xn = (_rms(x) * g_ref[...]).astype(BF16)
    acc = x
    for c in range(D_FF // fc):
        sl = slice(c * fc, (c + 1) * fc)
        h1 = _dot(xn, w1_ref[:, sl])
        h3 = _dot(xn, w3_ref[:, sl])
        a = (_silu(h1) * h3).astype(BF16)
        acc = acc + 0.5 * _dot(a, w2_ref[sl, :])
    if final:
        acc = _rms(acc) * fg_ref[...]
    o_ref[...] = acc


def _ffn(x, g, w1, w3, w2, final_g=None):
    T = x.shape[0]
    tm = min(512, T)
    final = final_g is not None
    in_specs = [pl.BlockSpec((tm, D_MODEL), lambda i: (i, 0)), _full((1, D_MODEL)),
                _resident((D_MODEL, D_FF)), _resident((D_MODEL, D_FF)), _resident((D_FF, D_MODEL))]
    args = [x, g, w1, w3, w2]
    if final:
        in_specs.append(_full((1, D_MODEL)))
        args.append(final_g)
    return pl.pallas_call(
        functools.partial(_ffn_body, fc=D_FF // 2, final=final),
        grid=(T // tm,),
        in_specs=in_specs,
        out_specs=pl.BlockSpec((tm, D_MODEL), lambda i: (i, 0)),
        out_shape=jax.ShapeDtypeStruct((T, D_MODEL), F32),
        compiler_params=pltpu.CompilerParams(dimension_semantics=("parallel",), vmem_limit_bytes=VMEM_LIMIT),
        name="ffn",
    )(*args)


PROJ_OUT = (("z", 512, F32), ("xbc", 1024, F32), ("dt", 128, F32), ("rq", 256, F32), ("rk", 256, F32),
            ("rv", 512, F32), ("rg", 512, F32), ("dq", 512, F32), ("dk", 256, F32), ("dv", 256, F32),
            ("mq", 256, F32), ("mkv", 256, F32), ("gates", 4096, BF16))
PROJ_W = sum(w for _, w, _ in PROJ_OUT)


def _inproj_body(x_ref, g_ref, w_ref, *o_refs):
    xn = (_rms(x_ref[...]) * g_ref[...]).astype(BF16)
    off = 0
    for (_, w, dt), o_ref in zip(PROJ_OUT, o_refs):
        o_ref[...] = _dot(xn, w_ref[:, off:off + w]).astype(dt)
        off += w


def _inproj(x, g, w):
    T = x.shape[0]
    tm = min(256, T)
    outs = pl.pallas_call(
        _inproj_body,
        grid=(T // tm,),
        in_specs=[pl.BlockSpec((tm, D_MODEL), lambda i: (i, 0)), _full((1, D_MODEL)), _resident((D_MODEL, PROJ_W))],
        out_specs=[pl.BlockSpec((tm, w_), lambda i: (i, 0)) for _, w_, _ in PROJ_OUT],
        out_shape=[jax.ShapeDtypeStruct((T, w_), dt) for _, w_, dt in PROJ_OUT],
        compiler_params=pltpu.CompilerParams(dimension_semantics=("parallel",), vmem_limit_bytes=VMEM_LIMIT),
        name="inproj",
    )(x, g, w)
    return {n: o for (n, _, _), o in zip(PROJ_OUT, outs)}


def _ssd_body(xbc_ref, z_ref, dt_ref, dtT_ref, cb0_ref, s0_ref, cw_ref, cbias_ref, dtb_ref, dtbT_ref,
              alog_ref, alogT_ref, dexp_ref, nw_ref, e_ref,
              y_ref, cnew_ref, snew_ref, xp_ref, s_ref, *, rows, valid):
    Q = CHUNK
    c = pl.program_id(1)

    @pl.when(c == 0)
    def _():
        xp_ref[0:8, :] = jnp.zeros((8, SSM_CONV_DIM), F32)
        xp_ref[5:8, :] = cb0_ref[...]
        s_ref[...] = s0_ref[...]

    xp_ref[8:8 + Q, :] = _pad_rows(xbc_ref[...], Q)
    conv = cbias_ref[...] + xp_ref[5:5 + Q, :] * cw_ref[0:1, :]
    for w in range(1, CONV_W):
        conv = conv + xp_ref[5 + w:5 + w + Q, :] * cw_ref[w:w + 1, :]
    u = _silu(conv)
    last3 = xp_ref[8 + valid - 3:8 + valid, :]
    xp_ref[5:8, :] = last3
    cnew_ref[...] = last3

    row = lax.broadcasted_iota(jnp.int32, (Q, Q), 0)
    col = lax.broadcasted_iota(jnp.int32, (Q, Q), 1)
    causal = row >= col
    hi = lax.Precision.HIGHEST

    dtv = _softplus(dt_ref[...] + dtb_ref[...])
    dtv = jnp.where(lax.broadcasted_iota(jnp.int32, dtv.shape, 0) < valid, dtv, 0.0)
    la = -jnp.exp(alog_ref[...]) * dtv
    cum = _dot(causal.astype(F32), la, hi)
    dtT = _softplus(dtT_ref[...] + dtbT_ref[...])
    dtT = jnp.where(lax.broadcasted_iota(jnp.int32, dtT.shape, 1) < valid, dtT, 0.0)
    laT = -jnp.exp(alogT_ref[...]) * dtT
    cumT = _dot(laT, (row <= col).astype(F32), hi)
    cum_exp = _dot(cum, e_ref[...], hi)
    dt_exp = _dot(dtv, e_ref[...], hi)
    cl_exp = cum_exp[Q - 1:Q, :]
    ecum = jnp.exp(cum_exp)
    tail = jnp.exp(cl_exp - cum_exp)
    sdecay = jnp.exp(cl_exp)

    xs = u[:, :BRANCH_W]
    v = xs * dt_exp
    vb = v.astype(BF16)
    vt = (v * tail).astype(BF16)
    lane = lax.broadcasted_iota(jnp.int32, (Q, LANES), 1)
    ys = []
    for g in range(2):
        bg = u[:, 512 + g * 128:512 + (g + 1) * 128].astype(BF16)
        cg = u[:, 768 + g * 128:768 + (g + 1) * 128].astype(BF16)
        gmat = _dot_nt(cg, bg)
        for pp in range(2):
            p = g * 2 + pp
            sl = slice(p * LANES, (p + 1) * LANES)
            s_pair = s_ref[p]
            inter = _dot(cg, s_pair.astype(BF16))
            yh = []
            for hh in range(2):
                h = 2 * p + hh
                seg = cum[:, h:h + 1] - cumT[h:h + 1, :]
                dec = jnp.exp(jnp.where(causal, seg, NEG_INF))
                yh.append(_dot((gmat * dec).astype(BF16), vb[:, sl]))
            ys.append(jnp.where(lane < SSM_HEAD_DIM, yh[0], yh[1]) + inter * ecum[:, sl])
            s_ref[p] = s_pair * sdecay[:, sl] + _dot_tn(bg, vt[:, sl])
    y = jnp.concatenate(ys, axis=-1) + xs * dexp_ref[...]
    y = y * _silu(_pad_rows(z_ref[...], Q))
    half = BRANCH_W // 2
    y = jnp.concatenate([_rms(y[:, :half]), _rms(y[:, half:])], axis=-1) * nw_ref[...]
    y_ref[...] = y[:rows]
    snew_ref[...] = s_ref[...]


def _ssd(xbc, z, dt, dtT, cb0, s0, w, rows, valid):
    B, L, _ = xbc.shape
    nc = L // rows
    seq = lambda width: pl.BlockSpec((None, rows, width), lambda b, c: (b, c, 0))
    per_b = lambda *shape: pl.BlockSpec((None,) + shape, lambda b, c: (b,) + (0,) * len(shape))
    return pl.pallas_call(
        functools.partial(_ssd_body, rows=rows, valid=valid),
        grid=(B, nc),
        in_specs=[seq(SSM_CONV_DIM), seq(BRANCH_W), pl.BlockSpec((None, CHUNK, LANES), lambda b, c: (b, c, 0)),
                  pl.BlockSpec((None, SSM_HEADS, CHUNK), lambda b, c: (b, 0, c)),
                  per_b(CONV_W - 1, SSM_CONV_DIM), per_b(4, LANES, LANES),
                  _full((CONV_W, SSM_CONV_DIM)), _full((1, SSM_CONV_DIM)), _full((1, LANES)), _full((SSM_HEADS, 1)),
                  _full((1, LANES)), _full((SSM_HEADS, 1)), _full((1, BRANCH_W)), _full((1, BRANCH_W)),
                  _full((LANES, BRANCH_W))],
        out_specs=[seq(BRANCH_W), per_b(CONV_W - 1, SSM_CONV_DIM), per_b(4, LANES, LANES)],
        out_shape=[jax.ShapeDtypeStruct((B, L, BRANCH_W), F32),
                   jax.ShapeDtypeStruct((B, CONV_W - 1, SSM_CONV_DIM), F32),
                   jax.ShapeDtypeStruct((B, 4, LANES, LANES), F32)],
        scratch_shapes=[pltpu.VMEM((8 + CHUNK, SSM_CONV_DIM), F32), pltpu.VMEM((4, LANES, LANES), F32)],
        compiler_params=pltpu.CompilerParams(dimension_semantics=("parallel", "arbitrary")),
        name="ssd",
    )(xbc, z, dt, dtT, cb0, s0, w["conv_w"], w["conv_b"], w["dt_bias"], w["dt_biasT"], w["a_log"], w["a_logT"],
      w["d_exp"], w["ssm_norm"], w["expand"])


def _ret_body(q_ref, k_ref, v_ref, g_ref, cos_ref, sin_ref, s0_ref, nw_ref, y_ref, snew_ref, s_ref, *, rows, valid):
    Q = CHUNK
    c = pl.program_id(1)

    @pl.when(c == 0)
    def _():
        s_ref[...] = s0_ref[...]

    cosf = _pad_rows(cos_ref[...], Q)
    sins = _pad_rows(sin_ref[...], Q)
    q = _rotate(_pad_rows(q_ref[...], Q), cosf, sins, RET_QK // 2)
    k = _rotate(_pad_rows(k_ref[...], Q), cosf, sins, RET_QK // 2) * (RET_QK ** -0.5)
    v = _pad_rows(v_ref[...], Q)
    gate = _pad_rows(g_ref[...], Q)
    ri = lax.broadcasted_iota(jnp.int32, (Q, 1), 0)
    ci = lax.broadcasted_iota(jnp.int32, (1, Q), 1)
    cnt_i = jnp.minimum(ri + 1, valid).astype(F32)
    cnt_j = jnp.minimum(ci + 1, valid).astype(F32)
    causal = ri >= ci
    lane = lax.broadcasted_iota(jnp.int32, (1, LANES), 1)
    ys = []
    for h in range(RET_HEADS):
        lg = RET_LOG_GAMMA[h]
        p, hh = divmod(h, 2)
        mine = (lane // RET_QK) == hh
        qm = jnp.where(mine, q[:, p * LANES:(p + 1) * LANES], 0.0).astype(BF16)
        km = jnp.where(mine, k[:, p * LANES:(p + 1) * LANES], 0.0).astype(BF16)
        dec = jnp.exp(jnp.where(causal, (cnt_i - cnt_j) * lg, NEG_INF))
        sc = (_dot_nt(qm, km) * dec).astype(BF16)
        vh = jnp.where(ri < valid, v[:, h * RET_V:(h + 1) * RET_V], 0.0)
        s_stack = s_ref[p * LANES:(p + 1) * LANES, :]
        y = _dot(sc, vh.astype(BF16)) + _dot(qm, s_stack.astype(BF16)) * jnp.exp(cnt_i * lg)
        tail = jnp.exp((valid - cnt_i) * lg)
        upd = _dot_tn(km, (vh * tail).astype(BF16))
        rs = slice(h * RET_QK, (h + 1) * RET_QK)
        s_ref[rs, :] = s_ref[rs, :] * math.exp(valid * lg) + upd[hh * RET_QK:(hh + 1) * RET_QK, :]
        mu = jnp.mean(y, axis=-1, keepdims=True)
        yc = y - mu
        var = jnp.mean(yc * yc, axis=-1, keepdims=True)
        ys.append(yc * lax.rsqrt(var + NORM_EPS))
    y = jnp.concatenate(ys, axis=-1) * nw_ref[...] * _silu(gate)
    y_ref[...] = y[:rows]
    snew_ref[...] = s_ref[...]


def _ret(q, k, v, g, cos, sin, s0, nw, rows, valid):
    B, L, _ = q.shape
    nc = L // rows
    seq = lambda width: pl.BlockSpec((None, rows, width), lambda b, c: (b, c, 0))
    tab = pl.BlockSpec((rows, 2 * LANES), lambda b, c: (c, 0))
    st = pl.BlockSpec((None, 2 * LANES, LANES), lambda b, c: (b, 0, 0))
    return pl.pallas_call(
        functools.partial(_ret_body, rows=rows, valid=valid),
        grid=(B, nc),
        in_specs=[seq(256), seq(256), seq(512), seq(512), tab, tab, st, _full((1, BRANCH_W))],
        out_specs=[seq(BRANCH_W), st],
        out_shape=[jax.ShapeDtypeStruct((B, L, BRANCH_W), F32), jax.ShapeDtypeStruct((B, 2 * LANES, LANES), F32)],
        scratch_shapes=[pltpu.VMEM((2 * LANES, LANES), F32)],
        compiler_params=pltpu.CompilerParams(dimension_semantics=("parallel", "arbitrary")),
        name="retention",
    )(q, k, v, g, cos, sin, s0, nw)


def _rel_bias_minus_far(dist, tb_ref, head):
    far = tb_ref[REL_BUCKETS - 1, head]
    val = jnp.full(dist.shape, tb_ref[0, head] - far, F32)
    for bk in range(1, REL_BUCKETS):
        val = jnp.where(dist >= REL_LB[bk], tb_ref[bk, head] - far, val)
    return val


def _softmax_step(s, v_bf16, m_ref, l_ref, acc_ref):
    m_prev = m_ref[...]
    m_new = jnp.maximum(m_prev, jnp.max(s, axis=-1, keepdims=True))
    alpha = jnp.exp(m_prev - m_new)
    p = jnp.exp(s - m_new)
    l_ref[...] = alpha * l_ref[...] + jnp.sum(p, axis=-1, keepdims=True)
    acc_ref[...] = alpha * acc_ref[...] + _dot(p.astype(BF16), v_bf16)
    m_ref[...] = m_new


def _softmax_init(m_ref, l_ref, acc_ref):
    m_ref[...] = jnp.full(m_ref.shape, NEG_INF, F32)
    l_ref[...] = jnp.zeros(l_ref.shape, F32)
    acc_ref[...] = jnp.zeros(acc_ref.shape, F32)


def _diff_stack_q(q, qs_ref, rows):
    lane = lax.broadcasted_iota(jnp.int32, (1, LANES), 1)
    for g in range(2):
        qg = q[:, g * LANES:(g + 1) * LANES] * (DIFF_DH ** -0.5)
        for m in range(2):
            r = (g * 2 + m) * rows
            qs_ref[r:r + rows, :] = jnp.where((lane // DIFF_DH) == m, qg, 0.0).astype(BF16)


def _diff_finish(l_ref, acc_ref, sc_ref, subln, rows):
    o = acc_ref[...] / l_ref[...]
    outs = []
    for g in range(2):
        og = o[(2 * g) * rows:(2 * g + 1) * rows] - sc_ref[0] * o[(2 * g + 1) * rows:(2 * g + 2) * rows]
        outs.append(_rms(og) * subln * sc_ref[1])
    return jnp.concatenate(outs, axis=-1)


def _diffp_body(tb_ref, sc_ref, q_ref, k_ref, v_ref, sub_ref, y_ref,
                qs_ref, m_ref, l_ref, acc_ref, t0_ref, t1_ref):
    tq = tk = CHUNK
    h = pl.program_id(1)
    qi = pl.program_id(2)

    @pl.when(qi == 0)
    def _():
        ri = lax.broadcasted_iota(jnp.int32, (tq, tk), 0)
        ci = lax.broadcasted_iota(jnp.int32, (tq, tk), 1)
        for g in range(2):
            b0 = jnp.where(ri >= ci, _rel_bias_minus_far(jnp.maximum(ri - ci, 0), tb_ref, h * 2 + g), NEG_INF)
            b1 = _rel_bias_minus_far(ri - ci + tk, tb_ref, h * 2 + g)
            for m in range(2):
                r = (g * 2 + m) * tq
                t0_ref[r:r + tq, :] = b0
                t1_ref[r:r + tq, :] = b1

    _diff_stack_q(q_ref[...], qs_ref, tq)
    _softmax_init(m_ref, l_ref, acc_ref)

    def step(j, bias_ref):
        start = pl.multiple_of(j * tk, tk)
        s = _dot_nt(qs_ref[...], k_ref[pl.ds(start, tk), :].astype(BF16))
        if bias_ref is not None:
            s = s + bias_ref[...]
        _softmax_step(s, v_ref[pl.ds(start, tk), :].astype(BF16), m_ref, l_ref, acc_ref)

    def far(j, carry):
        step(j, None)
        return carry

    lax.fori_loop(0, jnp.maximum(qi - 1, 0), far, 0)

    @pl.when(qi >= 1)
    def _():
        step(qi - 1, t1_ref)

    step(qi, t0_ref)
    y_ref[...] = _diff_finish(l_ref, acc_ref, sc_ref, sub_ref[...], tq)


def _diff_prompt(tb, sc, dq, dk, dv, subln):
    B, L, _ = dq.shape
    tq = CHUNK
    assert 2 * CHUNK - 1 >= REL_FAR and L % tq == 0
    return pl.pallas_call(
        _diffp_body,
        grid=(B, DIFF_KVH, L // tq),
        in_specs=[_SMEM, _SMEM,
                  pl.BlockSpec((None, tq, 2 * LANES), lambda b, h, i: (b, i, h)),
                  pl.BlockSpec((None, L, LANES), lambda b, h, i: (b, 0, h)),
                  pl.BlockSpec((None, L, LANES), lambda b, h, i: (b, 0, h)),
                  _full((1, DIFF_DV))],
        out_specs=pl.BlockSpec((None, tq, 2 * LANES), lambda b, h, i: (b, i, h)),
        out_shape=jax.ShapeDtypeStruct((B, L, BRANCH_W), F32),
        scratch_shapes=[pltpu.VMEM((4 * tq, LANES), BF16), pltpu.VMEM((4 * tq, 1), F32), pltpu.VMEM((4 * tq, 1), F32),
                        pltpu.VMEM((4 * tq, LANES), F32), pltpu.VMEM((4 * tq, CHUNK), F32),
                        pltpu.VMEM((4 * tq, CHUNK), F32)],
        compiler_params=pltpu.CompilerParams(dimension_semantics=("parallel", "parallel", "arbitrary")),
        name="diff_prompt",
    )(tb, sc, dq, dk, dv, subln)


MLA_SCALE = (MLA_NOPE + MLA_ROPE) ** -0.5


def _mla_q_prep(mq, qn, wq_ref, wuk_ref, cosf, sins, qs_ref, rows):
    cq = (_rms(mq) * qn).astype(BF16)
    qh = _dot(cq, wq_ref[...])
    for h in range(MLA_HEADS):
        nope = qh[:, h * 256:h * 256 + LANES].astype(BF16)
        rope = _rotate(qh[:, h * 256 + LANES:(h + 1) * 256], cosf, sins, MLA_ROPE // 2)
        qs_ref[h * rows:(h + 1) * rows, :LANES] = _dot(nope, wuk_ref[h]).astype(BF16)
        qs_ref[h * rows:(h + 1) * rows, LANES:] = rope.astype(BF16)


def _mla_kv_prep(mkv, kvn, cosf, sins):
    c_kv = _rms(mkv[:, :LANES]) * kvn
    k_rope = _rotate(mkv[:, LANES:], cosf, sins, MLA_ROPE // 2)
    return c_kv, k_rope


def _mla_finish(l_ref, acc_ref, wuv_ref, rows):
    o = (acc_ref[...] / l_ref[...]).astype(BF16)
    return jnp.concatenate([_dot(o[h * rows:(h + 1) * rows], wuv_ref[h]) for h in range(MLA_HEADS)], axis=-1)


def _mlap_body(mq_ref, mkv_ref, cq_ref, sq_ref, ck_ref, sk_ref, qn_ref, kvn_ref, wq_ref, wuk_ref, wuv_ref,
               y_ref, lat_ref, kr_ref, kcat_ref, qs_ref, m_ref, l_ref, acc_ref):
    tq = tk = CHUNK
    qi = pl.program_id(1)

    @pl.when(qi == 0)
    def _():
        c_kv, k_rope = _mla_kv_prep(mkv_ref[...], kvn_ref[...], ck_ref[...], sk_ref[...])
        lat_ref[...] = c_kv
        kr_ref[...] = k_rope[:, :MLA_ROPE]
        kcat_ref[:, :LANES] = c_kv.astype(BF16)
        kcat_ref[:, LANES:] = k_rope.astype(BF16)

    _mla_q_prep(mq_ref[...], qn_ref[...], wq_ref, wuk_ref, cq_ref[...], sq_ref[...], qs_ref, tq)
    _softmax_init(m_ref, l_ref, acc_ref)

    def step(j, diag):
        start = pl.multiple_of(j * tk, tk)
        kc = kcat_ref[pl.ds(start, tk), :]
        s = _dot_nt(qs_ref[...], kc) * MLA_SCALE
        if diag:
            ri = lax.broadcasted_iota(jnp.int32, (MLA_HEADS * tq, tk), 0) % tq
            ci = lax.broadcasted_iota(jnp.int32, (MLA_HEADS * tq, tk), 1)
            s = jnp.where(ri >= ci, s, NEG_INF)
        _softmax_step(s, kc[:, :LANES], m_ref, l_ref, acc_ref)

    def far(j, carry):
        step(j, False)
        return carry

    lax.fori_loop(0, qi, far, 0)
    step(qi, True)
    y_ref[...] = _mla_finish(l_ref, acc_ref, wuv_ref, tq)


def _mla_prompt(mq, mkv, cos, sin, w):
    B, L, _ = mq.shape
    tq = CHUNK
    return pl.pallas_call(
        _mlap_body,
        grid=(B, L // tq),
        in_specs=[pl.BlockSpec((None, tq, 256), lambda b, i: (b, i, 0)),
                  pl.BlockSpec((None, L, 256), lambda b, i: (b, 0, 0)),
                  pl.BlockSpec((tq, LANES), lambda b, i: (i, 0)), pl.BlockSpec((tq, LANES), lambda b, i: (i, 0)),
                  _full((L, LANES)), _full((L, LANES)),
                  _full((1, MLA_Q_RANK)), _full((1, MLA_KV_RANK)),
                  _full((MLA_Q_RANK, MLA_HEADS * 256)), _full((MLA_HEADS, LANES, LANES)),
                  _full((MLA_HEADS, LANES, LANES))],
        out_specs=[pl.BlockSpec((None, tq, BRANCH_W), lambda b, i: (b, i, 0)),
                   pl.BlockSpec((None, L, MLA_KV_RANK), lambda b, i: (b, 0, 0)),
                   pl.BlockSpec((None, L, MLA_ROPE), lambda b, i: (b, 0, 0))],
        out_shape=[jax.ShapeDtypeStruct((B, L, BRANCH_W), F32), jax.ShapeDtypeStruct((B, L, MLA_KV_RANK), F32),
                   jax.ShapeDtypeStruct((B, L, MLA_ROPE), F32)],
        scratch_shapes=[pltpu.VMEM((L, 2 * LANES), BF16), pltpu.VMEM((MLA_HEADS * tq, 2 * LANES), BF16),
                        pltpu.VMEM((MLA_HEADS * tq, 1), F32), pltpu.VMEM((MLA_HEADS * tq, 1), F32),
                        pltpu.VMEM((MLA_HEADS * tq, LANES), F32)],
        compiler_params=pltpu.CompilerParams(dimension_semantics=("parallel", "arbitrary")),
        name="mla_prompt",
    )(mq, mkv, cos, sin, cos, sin, w["mla_q_norm"], w["mla_kv_norm"], w["mla_wq"], w["mla_wuk"], w["mla_wuv"])


SROWS = 8


def _diffs_body(pt_ref, tb_ref, sc_ref, q_ref, kn_ref, vn_ref, sub_ref, *rest, npg, past_len):
    k_refs = rest[:npg]
    v_refs = rest[npg:2 * npg]
    y_ref, qs_ref, m_ref, l_ref, acc_ref = rest[2 * npg:]
    j = pl.program_id(1)
    nj = pl.num_programs(1)
    R = 4 * SROWS
    width = npg * PAGE

    @pl.when(j == 0)
    def _():
        for h in range(DIFF_KVH):
            _diff_stack_q(q_ref[:, h * 256:(h + 1) * 256], qs_ref.at[h], SROWS)
        _softmax_init(m_ref, l_ref, acc_ref)

    def scores(h):
        hs = slice(h * LANES, (h + 1) * LANES)
        return jnp.concatenate([_dot_nt(qs_ref[h], k_refs[i][:, hs].astype(BF16)) for i in range(npg)], axis=-1)

    def update(h, s):
        hs = slice(h * LANES, (h + 1) * LANES)
        m_prev = m_ref[h]
        m_new = jnp.maximum(m_prev, jnp.max(s, axis=-1, keepdims=True))
        alpha = jnp.exp(m_prev - m_new)
        p = jnp.exp(s - m_new).astype(BF16)
        l_ref[h] = alpha * l_ref[h] + jnp.sum(p.astype(F32), axis=-1, keepdims=True)
        pv = _dot(p[:, :PAGE], v_refs[0][:, hs].astype(BF16))
        for i in range(1, npg):
            pv = pv + _dot(p[:, i * PAGE:(i + 1) * PAGE], v_refs[i][:, hs].astype(BF16))
        acc_ref[h] = alpha * acc_ref[h] + pv
        m_ref[h] = m_new

    @pl.when(j < nj - 1)
    def _():
        for h in range(DIFF_KVH):
            update(h, scores(h))

    @pl.when(j == nj - 1)
    def _():
        t_row = lax.broadcasted_iota(jnp.int32, (2 * SROWS, width), 0) % SROWS
        ci = lax.broadcasted_iota(jnp.int32, (2 * SROWS, width), 1)
        dist = past_len + t_row - ((nj - 1) * width + ci)
        t_new = lax.broadcasted_iota(jnp.int32, (2 * SROWS, PAGE), 0) % SROWS
        c_new = lax.broadcasted_iota(jnp.int32, (2 * SROWS, PAGE), 1)
        for h in range(DIFF_KVH):
            hs = slice(h * LANES, (h + 1) * LANES)
            bias = jnp.concatenate([_rel_bias_minus_far(dist, tb_ref, h * 2 + g) for g in range(2)], axis=0)
            update(h, scores(h) + bias)
            kn = _pad_rows(kn_ref[:, hs], PAGE).astype(BF16)
            vn = _pad_rows(vn_ref[:, hs], PAGE).astype(BF16)
            bias_n = jnp.concatenate(
                [jnp.where(t_new >= c_new, _rel_bias_minus_far(jnp.maximum(t_new - c_new, 0), tb_ref, h * 2 + g), NEG_INF)
                 for g in range(2)], axis=0)
            s = _dot_nt(qs_ref[h], kn) + bias_n
            m_prev = m_ref[h]
            m_new = jnp.maximum(m_prev, jnp.max(s, axis=-1, keepdims=True))
            alpha = jnp.exp(m_prev - m_new)
            p = jnp.exp(s - m_new).astype(BF16)
            l_ref[h] = alpha * l_ref[h] + jnp.sum(p.astype(F32), axis=-1, keepdims=True)
            acc_ref[h] = alpha * acc_ref[h] + _dot(p, vn)
            m_ref[h] = m_new
            y_ref[:, h * 256:(h + 1) * 256] = _diff_finish(l_ref.at[h], acc_ref.at[h], sc_ref, sub_ref[...], SROWS)


def _pages_per_step(n_pages):
    for p in (16, 8, 4, 2, 1):
        if n_pages % p == 0:
            return p


def _diff_sample(pt, tb, sc, dq, dk, dv, subln, cache_k, cache_v, layer):
    B = dq.shape[0]
    n_pages = pt.shape[1]
    npg = _pages_per_step(n_pages)
    R = 4 * SROWS
    tok = lambda width: pl.BlockSpec((None, SROWS, width), lambda b, j, pt: (b, 0, 0))
    page = lambda i: pl.BlockSpec((None, None, PAGE, 256), lambda b, j, pt: (pt[b, j * npg + i], layer, 0, 0))
    grid_spec = pltpu.PrefetchScalarGridSpec(
        num_scalar_prefetch=1,
        grid=(B, n_pages // npg),
        in_specs=[_SMEM, _SMEM, tok(512), tok(256), tok(256), pl.BlockSpec((1, DIFF_DV), lambda b, j, pt: (0, 0))]
        + [page(i) for i in range(npg)] + [page(i) for i in range(npg)],
        out_specs=tok(BRANCH_W),
        scratch_shapes=[pltpu.VMEM((DIFF_KVH, R, LANES), BF16), pltpu.VMEM((DIFF_KVH, R, 1), F32),
                        pltpu.VMEM((DIFF_KVH, R, 1), F32), pltpu.VMEM((DIFF_KVH, R, LANES), F32)],
    )
    return pl.pallas_call(
        functools.partial(_diffs_body, npg=npg, past_len=n_pages * PAGE),
        grid_spec=grid_spec,
        out_shape=jax.ShapeDtypeStruct((B, SROWS, BRANCH_W), F32),
        compiler_params=pltpu.CompilerParams(dimension_semantics=("parallel", "arbitrary"),
                                             vmem_limit_bytes=VMEM_LIMIT),
        name="diff_sample",
    )(pt, tb, sc, dq, dk, dv, subln, *([cache_k] * npg), *([cache_v] * npg))


def _mlas_body(pt_ref, mq_ref, mkv_ref, cos_ref, sin_ref, qn_ref, kvn_ref, wq_ref, wuk_ref, wuv_ref, *rest, npg):
    c_refs = rest[:npg]
    r_refs = rest[npg:2 * npg]
    y_ref, lat_ref, kr_ref, qs_ref, kn_ref, m_ref, l_ref, acc_ref = rest[2 * npg:]
    j = pl.program_id(1)
    nj = pl.num_programs(1)
    R = MLA_HEADS * SROWS

    @pl.when(j == 0)
    def _():
        c_kv, k_rope = _mla_kv_prep(mkv_ref[...], kvn_ref[...], cos_ref[...], sin_ref[...])
        lat_ref[...] = c_kv
        kr_ref[...] = k_rope[:, :MLA_ROPE]
        kn_ref[...] = jnp.zeros(kn_ref.shape, BF16)
        kn_ref[0:SROWS, :LANES] = c_kv.astype(BF16)
        kn_ref[0:SROWS, LANES:] = k_rope.astype(BF16)
        _mla_q_prep(mq_ref[...], qn_ref[...], wq_ref, wuk_ref, cos_ref[...], sin_ref[...], qs_ref, SROWS)
        _softmax_init(m_ref, l_ref, acc_ref)

    q_lat = qs_ref[:, :LANES]
    q_rope = qs_ref[:, LANES:LANES + MLA_ROPE]
    s = jnp.concatenate([_dot_nt(q_lat, c_refs[i][...].astype(BF16)) + _dot_nt(q_rope, r_refs[i][...].astype(BF16))
                         for i in range(npg)], axis=-1) * MLA_SCALE
    m_prev = m_ref[...]
    m_new = jnp.maximum(m_prev, jnp.max(s, axis=-1, keepdims=True))
    alpha = jnp.exp(m_prev - m_new)
    p = jnp.exp(s - m_new).astype(BF16)
    l_ref[...] = alpha * l_ref[...] + jnp.sum(p.astype(F32), axis=-1, keepdims=True)
    pv = _dot(p[:, :PAGE], c_refs[0][...].astype(BF16))
    for i in range(1, npg):
        pv = pv + _dot(p[:, i * PAGE:(i + 1) * PAGE], c_refs[i][...].astype(BF16))
    acc_ref[...] = alpha * acc_ref[...] + pv
    m_ref[...] = m_new

    @pl.when(j == nj - 1)
    def _():
        kc = kn_ref[...]
        t_new = lax.broadcasted_iota(jnp.int32, (R, PAGE), 0) % SROWS
        c_new = lax.broadcasted_iota(jnp.int32, (R, PAGE), 1)
        sn = jnp.where(t_new >= c_new, _dot_nt(qs_ref[...], kc) * MLA_SCALE, NEG_INF)
        _softmax_step(sn, kc[:, :LANES], m_ref, l_ref, acc_ref)
        y_ref[...] = _mla_finish(l_ref, acc_ref, wuv_ref, SROWS)


def _mla_sample(pt, mq, mkv, cos, sin, w, cache_c, cache_r, layer):
    B = mq.shape[0]
    n_pages = pt.shape[1]
    npg = _pages_per_step(n_pages)
    R = MLA_HEADS * SROWS
    tok = lambda width: pl.BlockSpec((None, SROWS, width), lambda b, j, pt: (b, 0, 0))
    full = lambda *shape: pl.BlockSpec(shape, lambda b, j, pt: (0,) * len(shape))
    page = lambda i, width: pl.BlockSpec((None, None, PAGE, width), lambda b, j, pt: (pt[b, j * npg + i], layer, 0, 0))
    grid_spec = pltpu.PrefetchScalarGridSpec(
        num_scalar_prefetch=1,
        grid=(B, n_pages // npg),
        in_specs=[tok(256), tok(256), full(SROWS, LANES), full(SROWS, LANES), full(1, MLA_Q_RANK), full(1, MLA_KV_RANK),
                  full(MLA_Q_RANK, MLA_HEADS * 256), full(MLA_HEADS, LANES, LANES), full(MLA_HEADS, LANES, LANES)]
        + [page(i, MLA_KV_RANK) for i in range(npg)] + [page(i, MLA_ROPE) for i in range(npg)],
        out_specs=[tok(BRANCH_W), tok(MLA_KV_RANK), tok(MLA_ROPE)],
        scratch_shapes=[pltpu.VMEM((R, 2 * LANES), BF16), pltpu.VMEM((PAGE, 2 * LANES), BF16),
                        pltpu.VMEM((R, 1), F32), pltpu.VMEM((R, 1), F32), pltpu.VMEM((R, LANES), F32)],
    )
    return pl.pallas_call(
        functools.partial(_mlas_body, npg=npg),
        grid_spec=grid_spec,
        out_shape=[jax.ShapeDtypeStruct((B, SROWS, BRANCH_W), F32), jax.ShapeDtypeStruct((B, SROWS, MLA_KV_RANK), F32),
                   jax.ShapeDtypeStruct((B, SROWS, MLA_ROPE), F32)],
        compiler_params=pltpu.CompilerParams(dimension_semantics=("parallel", "arbitrary"),
                                             vmem_limit_bytes=VMEM_LIMIT),
        name="mla_sample",
    )(pt, mq, mkv, cos, sin, w["mla_q_norm"], w["mla_kv_norm"], w["mla_wq"], w["mla_wuk"], w["mla_wuv"],
      *([cache_c] * npg), *([cache_r] * npg))


def _merge_body(x_ref, y0_ref, y1_ref, y2_ref, y3_ref, gt_ref, wb_ref, wo_ref, o_ref):
    mix = None
    for i, y_ref in enumerate((y0_ref, y1_ref, y2_ref, y3_ref)):
        proj = _dot(y_ref[...].astype(BF16), wb_ref[i])
        gate = gt_ref[:, i * D_MODEL:(i + 1) * D_MODEL].astype(F32)
        term = proj / (1.0 + jnp.exp(-gate))
        mix = term if mix is None else mix + term
    o_ref[...] = x_ref[...] + _dot(mix.astype(BF16), wo_ref[...])


def _merge(x, ys, gates, wb, wo):
    T = x.shape[0]
    tm = min(512, T)
    row = lambda width: pl.BlockSpec((tm, width), lambda i: (i, 0))
    return pl.pallas_call(
        _merge_body,
        grid=(T // tm,),
        in_specs=[row(D_MODEL)] + [row(BRANCH_W)] * 4 + [row(N_BRANCH * D_MODEL),
                                                       _resident((N_BRANCH, BRANCH_W, D_MODEL)),
                                                       _resident((D_MODEL, D_MODEL))],
        out_specs=row(D_MODEL),
        out_shape=jax.ShapeDtypeStruct((T, D_MODEL), F32),
        compiler_params=pltpu.CompilerParams(dimension_semantics=("parallel",), vmem_limit_bytes=VMEM_LIMIT),
        name="merge",
    )(x, *ys, gates, wb, wo)


def _rope_tables(pos, half):
    inv_freq = ROPE_BASE ** (-jnp.arange(half, dtype=F32) / half)
    ang = pos.astype(F32)[:, None] * inv_freq[None, :]
    cos, sin = jnp.cos(ang), jnp.sin(ang)
    return jnp.concatenate([cos, cos], axis=-1), jnp.concatenate([-sin, sin], axis=-1)


def _layer_weights(l, p):
    w = {}
    f = lambda a: a.astype(BF16)
    for n in ("ffn1_w1", "ffn1_w3", "ffn1_w2", "ffn2_w1", "ffn2_w3", "ffn2_w2", "w_out"):
        w[n] = f(p[n][l])
    w["w_branch"] = f(p["w_branch"][l])
    for n in ("norm_ffn1", "norm_mix", "norm_ffn2", "ssm_conv_b", "ssm_norm", "ret_norm", "diff_subln",
              "mla_q_norm", "mla_kv_norm"):
        w[n] = p[n][l][None, :]
    offs = np.cumsum((0,) + IN_SIZES)
    piece = lambda i: p["w_in"][l][:, offs[i]:offs[i + 1]]
    padc = lambda a, width: jnp.pad(a, ((0, 0), (0, width - a.shape[1])))
    cols = [piece(0), piece(1), padc(piece(2), LANES), piece(3), piece(4), piece(5), piece(6), piece(7), piece(8),
            piece(9), piece(10), padc(piece(11), 2 * LANES), piece(12)]
    w["w_in"] = f(jnp.concatenate(cols, axis=1))
    w["conv_w"] = p["ssm_conv_w"][l]
    w["conv_b"] = w["ssm_conv_b"]
    w["dt_bias"] = padc(p["ssm_dt_bias"][l][None, :], LANES)
    w["dt_biasT"] = p["ssm_dt_bias"][l][:, None]
    w["a_log"] = padc(p["ssm_a_log"][l][None, :], LANES)
    w["a_logT"] = p["ssm_a_log"][l][:, None]
    w["d_exp"] = jnp.repeat(p["ssm_d"][l], SSM_HEAD_DIM)[None, :]
    expand = np.zeros((LANES, BRANCH_W), np.float32)
    for h in range(SSM_HEADS):
        expand[h, h * SSM_HEAD_DIM:(h + 1) * SSM_HEAD_DIM] = 1.0
    w["expand"] = jnp.asarray(expand)
    wq = p["mla_w_q_up"][l]
    zq = jnp.zeros((MLA_Q_RANK, MLA_HEADS, LANES - MLA_NOPE), F32)
    zr = jnp.zeros((MLA_Q_RANK, MLA_HEADS, LANES - MLA_ROPE), F32)
    w["mla_wq"] = f(jnp.concatenate([wq[..., :MLA_NOPE], zq, wq[..., MLA_NOPE:], zr], axis=-1)
                    .reshape(MLA_Q_RANK, MLA_HEADS * 256))
    wuk = jnp.transpose(p["mla_w_uk"][l], (1, 2, 0))
    w["mla_wuk"] = f(jnp.pad(wuk, ((0, 0), (0, LANES - MLA_NOPE), (0, 0))))
    w["mla_wuv"] = f(jnp.transpose(p["mla_w_uv"][l], (1, 0, 2)))
    lam_init = 0.8 - 0.6 * math.exp(-0.3 * l)
    lw = p["diff_lambda"][l].astype(F32)
    lam = jnp.exp(jnp.sum(lw[0] * lw[1])) - jnp.exp(jnp.sum(lw[2] * lw[3])) + lam_init
    w["diff_sc"] = jnp.stack([lam, jnp.asarray(1.0 - lam_init, F32)]).astype(F32)
    return w


def _ssm_state_to_pairs(s):
    B = s.shape[0]
    return s.reshape(B, 4, 2, SSM_STATE, SSM_HEAD_DIM).transpose(0, 1, 3, 2, 4).reshape(B, 4, SSM_STATE, 2 * SSM_HEAD_DIM)


def _ssm_state_from_pairs(s):
    B = s.shape[0]
    return s.reshape(B, 4, SSM_STATE, 2, SSM_HEAD_DIM).transpose(0, 1, 3, 2, 4).reshape(B, SSM_HEADS, SSM_STATE, SSM_HEAD_DIM)


def _trunk(x, w_layers, rel_bias, final_norm, rows, valid, pos, conv0, ssm0, ret0, paged):
    B, L, _ = x.shape
    T = B * L
    depth = len(w_layers)
    xt = x.reshape(T, D_MODEL)
    ret_cos, ret_sin = _rope_tables(pos, RET_QK // 2)
    ret_cos, ret_sin = jnp.tile(ret_cos, (1, 4)), jnp.tile(ret_sin, (1, 4))
    mc, ms = _rope_tables(pos, MLA_ROPE // 2)
    mla_cos = jnp.pad(mc, ((0, 0), (0, LANES - MLA_ROPE)))
    mla_sin = jnp.pad(ms, ((0, 0), (0, LANES - MLA_ROPE)))
    nc = L // rows
    outs = []
    for l in range(depth):
        w = w_layers[l]
        xt = _ffn(xt, w["norm_ffn1"], w["ffn1_w1"], w["ffn1_w3"], w["ffn1_w2"])
        pr = _inproj(xt, w["norm_mix"], w["w_in"])
        r3 = lambda a: a.reshape(B, L, a.shape[-1])
        dt = r3(pr["dt"])
        dtT = jnp.transpose(dt[:, :, :SSM_HEADS].reshape(B, nc, rows, SSM_HEADS), (0, 3, 1, 2))
        dtT = jnp.pad(dtT, ((0, 0), (0, 0), (0, 0), (0, CHUNK - rows))).reshape(B, SSM_HEADS, nc * CHUNK)
        dtp = dt if rows == CHUNK else jnp.pad(dt, ((0, 0), (0, CHUNK - rows), (0, 0)))
        y_ssm, conv_new, ssm_new = _ssd(r3(pr["xbc"]), r3(pr["z"]), dtp, dtT, conv0[l], ssm0[l], w, rows, valid)
        y_ret, ret_new = _ret(r3(pr["rq"]), r3(pr["rk"]), r3(pr["rv"]), r3(pr["rg"]), ret_cos, ret_sin, ret0[l],
                              w["ret_norm"], rows, valid)
        dk, dv = r3(pr["dk"]), r3(pr["dv"])
        if paged is None:
            y_diff = _diff_prompt(rel_bias, w["diff_sc"], r3(pr["dq"]), dk, dv, w["diff_subln"])
            y_mla, lat, kr = _mla_prompt(r3(pr["mq"]), r3(pr["mkv"]), mla_cos, mla_sin, w)
        else:
            pt, ck, cv, cc, cr = paged
            y_diff = _diff_sample(pt, rel_bias, w["diff_sc"], r3(pr["dq"]), dk, dv, w["diff_subln"], ck, cv, l)
            y_mla, lat, kr = _mla_sample(pt, r3(pr["mq"]), r3(pr["mkv"]), mla_cos, mla_sin, w, cc, cr, l)
        ys = [a.reshape(T, BRANCH_W) for a in (y_ssm, y_ret, y_diff, y_mla)]
        xt = _merge(xt, ys, pr["gates"], w["w_branch"], w["w_out"])
        fg = final_norm if l == depth - 1 else None
        xt = _ffn(xt, w["norm_ffn2"], w["ffn2_w1"], w["ffn2_w3"], w["ffn2_w2"], fg)
        outs.append((dk, dv, lat, kr, conv_new, ssm_new, ret_new))
    return xt.reshape(B, L, D_MODEL), outs


def kernel(x_prompt, x_sample, cache_diff_k, cache_diff_v, cache_mla_latent, cache_mla_krope, state_ssm_conv, state_ssm, state_retention, page_table, norm_ffn1, ffn1_w1, ffn1_w3, ffn1_w2, norm_mix, w_in, ssm_conv_w, ssm_conv_b, ssm_dt_bias, ssm_a_log, ssm_d, ssm_norm, ret_norm, diff_lambda, diff_subln, mla_q_norm, mla_w_q_up, mla_kv_norm, mla_w_uk, mla_w_uv, rel_bias, w_branch, w_out, norm_ffn2, ffn2_w1, ffn2_w3, ffn2_w2, final_norm):
    p = dict(norm_ffn1=norm_ffn1, ffn1_w1=ffn1_w1, ffn1_w3=ffn1_w3, ffn1_w2=ffn1_w2, norm_mix=norm_mix, w_in=w_in,
             ssm_conv_w=ssm_conv_w, ssm_conv_b=ssm_conv_b, ssm_dt_bias=ssm_dt_bias, ssm_a_log=ssm_a_log, ssm_d=ssm_d,
             ssm_norm=ssm_norm, ret_norm=ret_norm, diff_lambda=diff_lambda, diff_subln=diff_subln,
             mla_q_norm=mla_q_norm, mla_w_q_up=mla_w_q_up, mla_kv_norm=mla_kv_norm, mla_w_uk=mla_w_uk,
             mla_w_uv=mla_w_uv, w_branch=w_branch, w_out=w_out, norm_ffn2=norm_ffn2, ffn2_w1=ffn2_w1,
             ffn2_w3=ffn2_w3, ffn2_w2=ffn2_w2)
    depth = w_in.shape[0]
    w_layers = [_layer_weights(l, p) for l in range(depth)]
    fnorm = final_norm[None, :]
    rel = rel_bias.astype(F32)

    Bp, Lp, _ = x_prompt.shape
    zc = jnp.zeros((Bp, CONV_W - 1, SSM_CONV_DIM), F32)
    zs = jnp.zeros((Bp, 4, LANES, LANES), F32)
    zr = jnp.zeros((Bp, 2 * LANES, LANES), F32)
    y_prompt, rows_p = _trunk(x_prompt, w_layers, rel, fnorm, CHUNK, CHUNK, jnp.arange(Lp, dtype=jnp.int32),
                              [zc] * depth, [zs] * depth, [zr] * depth, None)

    Bs, Ls, _ = x_sample.shape
    n_pages = page_table.shape[1]
    past_len = n_pages * PAGE
    xs = jnp.pad(x_sample, ((0, 0), (0, SROWS - Ls), (0, 0)))
    pos_s = past_len + jnp.arange(SROWS, dtype=jnp.int32)
    n_pool = cache_diff_k.shape[0]
    paged = (page_table.astype(jnp.int32),
             cache_diff_k.reshape(n_pool, depth, PAGE, 256), cache_diff_v.reshape(n_pool, depth, PAGE, 256),
             cache_mla_latent, cache_mla_krope)
    conv_s = [state_ssm_conv[:, l] for l in range(depth)]
    ssm_s = [_ssm_state_to_pairs(state_ssm[:, l]) for l in range(depth)]
    ret_s = [state_retention[:, l].reshape(Bs, 2 * LANES, LANES) for l in range(depth)]
    y_sample, rows_s = _trunk(xs, w_layers, rel, fnorm, SROWS, Ls, pos_s, conv_s, ssm_s, ret_s, paged)

    def collect(rows, B, L, keep):
        st = lambda i: jnp.stack([r[i] for r in rows], axis=1)
        dk = st(0)[:, :, :keep].reshape(B, depth, keep, DIFF_KVH, 2, DIFF_DH)
        dv = st(1)[:, :, :keep].reshape(B, depth, keep, DIFF_KVH, DIFF_DV)
        lat = st(2)[:, :, :keep]
        kr = st(3)[:, :, :keep]
        conv = st(4)
        ssm = jnp.stack([_ssm_state_from_pairs(r[5]) for r in rows], axis=1)
        ret = st(6).reshape(B, depth, RET_HEADS, RET_QK, RET_V)
        return dk, dv, lat, kr, conv, ssm, ret

    return (y_prompt, y_sample[:, :Ls]) + collect(rows_p, Bp, Lp, Lp) + collect(rows_s, Bs, SROWS, Ls)
```

```python
import functools
import math

import numpy as np
import jax
import jax.numpy as jnp
from jax import lax
from jax.experimental import pallas as pl
from jax.experimental.pallas import tpu as pltpu

F32 = jnp.float32
BF16 = jnp.bfloat16
NEG_INF = float("-inf")

D_MODEL = 1024
D_FF = 2816
BRANCH_W = 512
N_BRANCH = 4
SSM_HEADS = 8
SSM_HEAD_DIM = 64
SSM_STATE = 128
CONV_W = 4
SSM_CONV_DIM = 1024
RET_HEADS = 4
RET_QK = 64
RET_V = 128
DIFF_KVH = 2
DIFF_DH = 64
DIFF_DV = 128
MLA_HEADS = 4
MLA_Q_RANK = 256
MLA_KV_RANK = 128
MLA_NOPE = 64
MLA_ROPE = 32
PAGE = 128
REL_BUCKETS = 32
REL_MAX_EXACT = 16
REL_MAX_DIST = 128
ROPE_BASE = 10000.0
NORM_EPS = 1e-6
IN_SIZES = (512, 1024, 8, 256, 256, 512, 512, 512, 256, 256, 256, 160, 4096)

LANES = 128
CHUNK = 128
VMEM_LIMIT = 56 * 1024 * 1024


def _rel_lower_bounds():
    lb = list(range(REL_MAX_EXACT))
    d = np.arange(REL_MAX_EXACT, 4 * REL_MAX_DIST).astype(np.float32)
    large = REL_MAX_EXACT + (np.log(d / np.float32(REL_MAX_EXACT)) / np.float32(math.log(REL_MAX_DIST / REL_MAX_EXACT))
                             * np.float32(REL_BUCKETS - REL_MAX_EXACT)).astype(np.int32)
    large = np.minimum(large, REL_BUCKETS - 1)
    for bk in range(REL_MAX_EXACT, REL_BUCKETS):
        lb.append(int(REL_MAX_EXACT + np.argmax(large >= bk)))
    return tuple(lb)


REL_LB = _rel_lower_bounds()
REL_FAR = REL_LB[-1]
RET_LOG_GAMMA = tuple(math.log1p(-(2.0 ** (-5.0 - h))) for h in range(RET_HEADS))


def _dot(a, b, precision=None):
    return jnp.dot(a, b, preferred_element_type=F32, precision=precision)


def _dot_nt(a, b):
    return lax.dot_general(a, b, (((1,), (1,)), ((), ())), preferred_element_type=F32)


def _dot_tn(a, b):
    return lax.dot_general(a, b, (((0,), (0,)), ((), ())), preferred_element_type=F32)


def _rms(x):
    return x * lax.rsqrt(jnp.mean(x * x, axis=-1, keepdims=True) + NORM_EPS)


def _silu(x):
    return x / (1.0 + jnp.exp(-x))


def _softplus(x):
    return jnp.maximum(x, 0.0) + jnp.log1p(jnp.exp(-jnp.abs(x)))


def _pad_rows(x, rows):
    if x.shape[0] == rows:
        return x
    return jnp.concatenate([x, jnp.zeros((rows - x.shape[0],) + x.shape[1:], x.dtype)], axis=0)


def _rotate(x, cosf, sins, half):
    outs = []
    for s in range(x.shape[-1] // LANES):
        xs = x[:, s * LANES:(s + 1) * LANES]
        lane = lax.broadcasted_iota(jnp.int32, xs.shape, 1)
        first = (lane % (2 * half)) < half
        partner = jnp.where(first, pltpu.roll(xs, LANES - half, axis=1), pltpu.roll(xs, half, axis=1))
        outs.append(partner)
    partner = outs[0] if len(outs) == 1 else jnp.concatenate(outs, axis=-1)
    return x * cosf + partner * sins


def _full(shape):
    return pl.BlockSpec(shape, lambda *_: (0,) * len(shape))


def _resident(shape):
    return pl.BlockSpec(shape, lambda *_: (0,) * len(shape), pipeline_mode=pl.Buffered(1))


_SMEM = pl.BlockSpec(memory_space=pltpu.SMEM)


def _ffn_body(*refs, fc, final):
    if final:
        x_ref, g_ref, w1_ref, w3_ref, w2_ref, fg_ref, o_ref = refs
    else:
        x_ref, g_ref, w1_ref, w3_ref, w2_ref, o_ref = refs
    x = x_ref[...]
    xn = (_rms(x) * g_ref[...]).astype(BF16)
    acc = x
    for c in range(D_FF // fc):
        sl = slice(c * fc, (c + 1) * fc)
        h1 = _dot(xn, w1_ref[:, sl])
        h3 = _dot(xn, w3_ref[:, sl])
        a = (_silu(h1) * h3).astype(BF16)
        acc = acc + 0.5 * _dot(a, w2_ref[sl, :])
    if final:
        acc = _rms(acc) * fg_ref[...]
    o_ref[...] = acc


def _ffn(x, g, w1, w3, w2, final_g=None):
    T = x.shape[0]
    tm = min(512, T)
    final = final_g is not None
    in_specs = [pl.BlockSpec((tm, D_MODEL), lambda i: (i, 0)), _full((1, D_MODEL)),
                _resident((D_MODEL, D_FF)), _resident((D_MODEL, D_FF)), _resident((D_FF, D_MODEL))]
    args = [x, g, w1, w3, w2]
    if final:
        in_specs.append(_full((1, D_MODEL)))
        args.append(final_g)
    return pl.pallas_call(
        functools.partial(_ffn_body, fc=D_FF // 2, final=final),
        grid=(T // tm,),
        in_specs=in_specs,
        out_specs=pl.BlockSpec((tm, D_MODEL), lambda i: (i, 0)),
        out_shape=jax.ShapeDtypeStruct((T, D_MODEL), F32),
        compiler_params=pltpu.CompilerParams(dimension_semantics=("parallel",), vmem_limit_bytes=VMEM_LIMIT),
        name="ffn",
    )(*args)


PROJ_OUT = (("z", 512, F32), ("xbc", 1024, F32), ("dt", 128, F32), ("rq", 256, F32), ("rk", 256, F32),
            ("rv", 512, F32), ("rg", 512, F32), ("dq", 512, F32), ("dk", 256, F32), ("dv", 256, F32),
            ("mq", 256, F32), ("mkv", 256, F32), ("gates", 4096, BF16))
PROJ_W = sum(w for _, w, _ in PROJ_OUT)


def _inproj_body(x_ref, g_ref, w_ref, *o_refs):
    xn = (_rms(x_ref[...]) * g_ref[...]).astype(BF16)
    off = 0
    for (_, w, dt), o_ref in zip(PROJ_OUT, o_refs):
        o_ref[...] = _dot(xn, w_ref[:, off:off + w]).astype(dt)
        off += w


def _inproj(x, g, w):
    T = x.shape[0]
    tm = min(256, T)
    outs = pl.pallas_call(
        _inproj_body,
        grid=(T // tm,),
        in_specs=[pl.BlockSpec((tm, D_MODEL), lambda i: (i, 0)), _full((1, D_MODEL)), _resident((D_MODEL, PROJ_W))],
        out_specs=[pl.BlockSpec((tm, w_), lambda i: (i, 0)) for _, w_, _ in PROJ_OUT],
        out_shape=[jax.ShapeDtypeStruct((T, w_), dt) for _, w_, dt in PROJ_OUT],
        compiler_params=pltpu.CompilerParams(dimension_semantics=("parallel",), vmem_limit_bytes=VMEM_LIMIT),
        name="inproj",
    )(x, g, w)
    return {n: o for (n, _, _), o in zip(PROJ_OUT, outs)}


def _ssd_body(xbc_ref, z_ref, dt_ref, dtT_ref, cb0_ref, s0_ref, cw_ref, cbias_ref, dtb_ref, dtbT_ref,
              alog_ref, alogT_ref, dexp_ref, nw_ref, e_ref,
              y_ref, cnew_ref, snew_ref, xp_ref, s_ref, *, rows, valid):
    Q = CHUNK
    c = pl.program_id(1)

    @pl.when(c == 0)
    def _():
        xp_ref[0:8, :] = jnp.zeros((8, SSM_CONV_DIM), F32)
        xp_ref[5:8, :] = cb0_ref[...]
        s_ref[...] = s0_ref[...]

    xp_ref[8:8 + Q, :] = _pad_rows(xbc_ref[...], Q)
    conv = cbias_ref[...] + xp_ref[5:5 + Q, :] * cw_ref[0:1, :]
    for w in range(1, CONV_W):
        conv = conv + xp_ref[5 + w:5 + w + Q, :] * cw_ref[w:w + 1, :]
    u = _silu(conv)
    last3 = xp_ref[8 + valid - 3:8 + valid, :]
    xp_ref[5:8, :] = last3
    cnew_ref[...] = last3

    row = lax.broadcasted_iota(jnp.int32, (Q, Q), 0)
    col = lax.broadcasted_iota(jnp.int32, (Q, Q), 1)
    causal = row >= col
    hi = lax.Precision.HIGHEST

    dtv = _softplus(dt_ref[...] + dtb_ref[...])
    dtv = jnp.where(lax.broadcasted_iota(jnp.int32, dtv.shape, 0) < valid, dtv, 0.0)
    la = -jnp.exp(alog_ref[...]) * dtv
    cum = _dot(causal.astype(F32), la, hi)
    dtT = _softplus(dtT_ref[...] + dtbT_ref[...])
    dtT = jnp.where(lax.broadcasted_iota(jnp.int32, dtT.shape, 1) < valid, dtT, 0.0)
    laT = -jnp.exp(alogT_ref[...]) * dtT
    cumT = _dot(laT, (row <= col).astype(F32), hi)
    cum_exp = _dot(cum, e_ref[...], hi)
    dt_exp = _dot(dtv, e_ref[...], hi)
    cl_exp = cum_exp[Q - 1:Q, :]
    ecum = jnp.exp(cum_exp)
    tail = jnp.exp(cl_exp - cum_exp)
    sdecay = jnp.exp(cl_exp)

    xs = u[:, :BRANCH_W]
    v = xs * dt_exp
    vb = v.astype(BF16)
    vt = (v * tail).astype(BF16)
    lane = lax.broadcasted_iota(jnp.int32, (Q, LANES), 1)
    ys = []
    for g in range(2):
        bg = u[:, 512 + g * 128:512 + (g + 1) * 128].astype(BF16)
        cg = u[:, 768 + g * 128:768 + (g + 1) * 128].astype(BF16)
        gmat = _dot_nt(cg, bg)
        for pp in range(2):
            p = g * 2 + pp
            sl = slice(p * LANES, (p + 1) * LANES)
            s_pair = s_ref[p]
            inter = _dot(cg, s_pair.astype(BF16))
            yh = []
            for hh in range(2):
                h = 2 * p + hh
                seg = cum[:, h:h + 1] - cumT[h:h + 1, :]
                dec = jnp.exp(jnp.where(causal, seg, NEG_INF))
                yh.append(_dot((gmat * dec).astype(BF16), vb[:, sl]))
            ys.append(jnp.where(lane < SSM_HEAD_DIM, yh[0], yh[1]) + inter * ecum[:, sl])
            s_ref[p] = s_pair * sdecay[:, sl] + _dot_tn(bg, vt[:, sl])
    y = jnp.concatenate(ys, axis=-1) + xs * dexp_ref[...]
    y = y * _silu(_pad_rows(z_ref[...], Q))
    half = BRANCH_W // 2
    y = jnp.concatenate([_rms(y[:, :half]), _rms(y[:, half:])], axis=-1) * nw_ref[...]
    y_ref[...] = y[:rows]
    snew_ref[...] = s_ref[...]


def _ssd(xbc, z, dt, dtT, cb0, s0, w, rows, valid):
    B, L, _ = xbc.shape
    nc = L // rows
    seq = lambda width: pl.BlockSpec((None, rows, width), lambda b, c: (b, c, 0))
    per_b = lambda *shape: pl.BlockSpec((None,) + shape, lambda b, c: (b,) + (0,) * len(shape))
    return pl.pallas_call(
        functools.partial(_ssd_body, rows=rows, valid=valid),
        grid=(B, nc),
        in_specs=[seq(SSM_CONV_DIM), seq(BRANCH_W), pl.BlockSpec((None, CHUNK, LANES), lambda b, c: (b, c, 0)),
                  pl.BlockSpec((None, SSM_HEADS, CHUNK), lambda b, c: (b, 0, c)),
                  per_b(CONV_W - 1, SSM_CONV_DIM), per_b(4, LANES, LANES),
                  _full((CONV_W, SSM_CONV_DIM)), _full((1, SSM_CONV_DIM)), _full((1, LANES)), _full((SSM_HEADS, 1)),
                  _full((1, LANES)), _full((SSM_HEADS, 1)), _full((1, BRANCH_W)), _full((1, BRANCH_W)),
                  _full((LANES, BRANCH_W))],
        out_specs=[seq(BRANCH_W), per_b(CONV_W - 1, SSM_CONV_DIM), per_b(4, LANES, LANES)],
        out_shape=[jax.ShapeDtypeStruct((B, L, BRANCH_W), F32),
                   jax.ShapeDtypeStruct((B, CONV_W - 1, SSM_CONV_DIM), F32),
                   jax.ShapeDtypeStruct((B, 4, LANES, LANES), F32)],
        scratch_shapes=[pltpu.VMEM((8 + CHUNK, SSM_CONV_DIM), F32), pltpu.VMEM((4, LANES, LANES), F32)],
        compiler_params=pltpu.CompilerParams(dimension_semantics=("parallel", "arbitrary")),
        name="ssd",
    )(xbc, z, dt, dtT, cb0, s0, w["conv_w"], w["conv_b"], w["dt_bias"], w["dt_biasT"], w["a_log"], w["a_logT"],
      w["d_exp"], w["ssm_norm"], w["expand"])


def _ret_body(q_ref, k_ref, v_ref, g_ref, cos_ref, sin_ref, s0_ref, nw_ref, y_ref, snew_ref, s_ref, *, rows, valid):
    Q = CHUNK
    c = pl.program_id(1)

    @pl.when(c == 0)
    def _():
        s_ref[...] = s0_ref[...]

    cosf = _pad_rows(cos_ref[...], Q)
    sins = _pad_rows(sin_ref[...], Q)
    q = _rotate(_pad_rows(q_ref[...], Q), cosf, sins, RET_QK // 2)
    k = _rotate(_pad_rows(k_ref[...], Q), cosf, sins, RET_QK // 2) * (RET_QK ** -0.5)
    v = _pad_rows(v_ref[...], Q)
    gate = _pad_rows(g_ref[...], Q)
    ri = lax.broadcasted_iota(jnp.int32, (Q, 1), 0)
    ci = lax.broadcasted_iota(jnp.int32, (1, Q), 1)
    cnt_i = jnp.minimum(ri + 1, valid).astype(F32)
    cnt_j = jnp.minimum(ci + 1, valid).astype(F32)
    causal = ri >= ci
    lane = lax.broadcasted_iota(jnp.int32, (1, LANES), 1)
    ys = []
    for h in range(RET_HEADS):
        lg = RET_LOG_GAMMA[h]
        p, hh = divmod(h, 2)
        mine = (lane // RET_QK) == hh
        qm = jnp.where(mine, q[:, p * LANES:(p + 1) * LANES], 0.0).astype(BF16)
        km = jnp.where(mine, k[:, p * LANES:(p + 1) * LANES], 0.0).astype(BF16)
        dec = jnp.exp(jnp.where(causal, (cnt_i - cnt_j) * lg, NEG_INF))
        sc = (_dot_nt(qm, km) * dec).astype(BF16)
        vh = jnp.where(ri < valid, v[:, h * RET_V:(h + 1) * RET_V], 0.0)
        s_stack = s_ref[p * LANES:(p + 1) * LANES, :]
        y = _dot(sc, vh.astype(BF16)) + _dot(qm, s_stack.astype(BF16)) * jnp.exp(cnt_i * lg)
        tail = jnp.exp((valid - cnt_i) * lg)
        upd = _dot_tn(km, (vh * tail).astype(BF16))
        rs = slice(h * RET_QK, (h + 1) * RET_QK)
        s_ref[rs, :] = s_ref[rs, :] * math.exp(valid * lg) + upd[hh * RET_QK:(hh + 1) * RET_QK, :]
        mu = jnp.mean(y, axis=-1, keepdims=True)
        yc = y - mu
        var = jnp.mean(yc * yc, axis=-1, keepdims=True)
        ys.append(yc * lax.rsqrt(var + NORM_EPS))
    y = jnp.concatenate(ys, axis=-1) * nw_ref[...] * _silu(gate)
    y_ref[...] = y[:rows]
    snew_ref[...] = s_ref[...]


def _ret(q, k, v, g, cos, sin, s0, nw, rows, valid):
    B, L, _ = q.shape
    nc = L // rows
    seq = lambda width: pl.BlockSpec((None, rows, width), lambda b, c: (b, c, 0))
    tab = pl.BlockSpec((rows, 2 * LANES), lambda b, c: (c, 0))
    st = pl.BlockSpec((None, 2 * LANES, LANES), lambda b, c: (b, 0, 0))
    return pl.pallas_call(
        functools.partial(_ret_body, rows=rows, valid=valid),
        grid=(B, nc),
        in_specs=[seq(256), seq(256), seq(512), seq(512), tab, tab, st, _full((1, BRANCH_W))],
        out_specs=[seq(BRANCH_W), st],
        out_shape=[jax.ShapeDtypeStruct((B, L, BRANCH_W), F32), jax.ShapeDtypeStruct((B, 2 * LANES, LANES), F32)],
        scratch_shapes=[pltpu.VMEM((2 * LANES, LANES), F32)],
        compiler_params=pltpu.CompilerParams(dimension_semantics=("parallel", "arbitrary")),
        name="retention",
    )(q, k, v, g, cos, sin, s0, nw)


def _rel_bias_minus_far(dist, tb_ref, head):
    far = tb_ref[REL_BUCKETS - 1, head]
    val = jnp.full(dist.shape, tb_ref[0, head] - far, F32)
    for bk in range(1, REL_BUCKETS):
        val = jnp.where(dist >= REL_LB[bk], tb_ref[bk, head] - far, val)
    return val


def _softmax_step(s, v_bf16, m_ref, l_ref, acc_ref):
    m_prev = m_ref[...]
    m_new = jnp.maximum(m_prev, jnp.max(s, axis=-1, keepdims=True))
    alpha = jnp.exp(m_prev - m_new)
    p = jnp.exp(s - m_new)
    l_ref[...] = alpha * l_ref[...] + jnp.sum(p, axis=-1, keepdims=True)
    acc_ref[...] = alpha * acc_ref[...] + _dot(p.astype(BF16), v_bf16)
    m_ref[...] = m_new


def _softmax_init(m_ref, l_ref, acc_ref):
    m_ref[...] = jnp.full(m_ref.shape, NEG_INF, F32)
    l_ref[...] = jnp.zeros(l_ref.shape, F32)
    acc_ref[...] = jnp.zeros(acc_ref.shape, F32)


def _diff_stack_q(q, qs_ref, rows):
    lane = lax.broadcasted_iota(jnp.int32, (1, LANES), 1)
    for g in range(2):
        qg = q[:, g * LANES:(g + 1) * LANES] * (DIFF_DH ** -0.5)
        for m in range(2):
            r = (g * 2 + m) * rows
            qs_ref[r:r + rows, :] = jnp.where((lane // DIFF_DH) == m, qg, 0.0).astype(BF16)


def _diff_finish(o, sc_ref, subln, rows):
    outs = []
    for g in range(2):
        og = o[(2 * g) * rows:(2 * g + 1) * rows] - sc_ref[0] * o[(2 * g + 1) * rows:(2 * g + 2) * rows]
        outs.append(_rms(og) * subln * sc_ref[1])
    return jnp.concatenate(outs, axis=-1)


FAR_TILES = 4


def _causal_sweep(qi, visit):
    n_far = jnp.maximum(qi - 1, 0)
    rem = n_far % FAR_TILES
    for r in range(1, FAR_TILES):
        @pl.when(rem == r)
        def _(r=r):
            visit(0, r * CHUNK, False)

    def far(i, carry):
        visit(pl.multiple_of((rem + i * FAR_TILES) * CHUNK, CHUNK), FAR_TILES * CHUNK, False)
        return carry

    lax.fori_loop(0, n_far // FAR_TILES, far, 0)

    @pl.when(qi == 0)
    def _():
        visit(0, CHUNK, True)

    @pl.when(qi >= 1)
    def _():
        visit(pl.multiple_of((qi - 1) * CHUNK, CHUNK), 2 * CHUNK, True)


def _two_pass_attention(qi, score_fn, value_fn, m_ref, l_ref, acc_ref, halves):
    m_ref[...] = jnp.full(m_ref.shape, NEG_INF, F32)

    def track_max(start, width, near):
        for rows in halves:
            s = score_fn(rows, start, width, near)
            mx = s[:, :CHUNK]
            for c in range(1, width // CHUNK):
                mx = jnp.maximum(mx, s[:, c * CHUNK:(c + 1) * CHUNK])
            m_ref[rows, :] = jnp.maximum(m_ref[rows, :], mx)

    _causal_sweep(qi, track_max)
    m_ref[...] = jnp.broadcast_to(jnp.max(m_ref[...], axis=-1, keepdims=True), m_ref.shape)
    l_ref[...] = jnp.zeros(l_ref.shape, F32)
    acc_ref[...] = jnp.zeros(acc_ref.shape, F32)

    def accumulate(start, width, near):
        v = value_fn(start, width)
        for rows in halves:
            s = score_fn(rows, start, width, near)
            m = m_ref[rows, :]
            ps = [jnp.exp(s[:, c * CHUNK:(c + 1) * CHUNK] - m) for c in range(width // CHUNK)]
            lsum = ps[0]
            for pc in ps[1:]:
                lsum = lsum + pc
            l_ref[rows, :] += lsum
            p = ps[0] if len(ps) == 1 else jnp.concatenate(ps, axis=-1)
            acc_ref[rows, :] += _dot(p.astype(BF16), v)

    _causal_sweep(qi, accumulate)
    return acc_ref[...] / jnp.sum(l_ref[...], axis=-1, keepdims=True)


def _diffp_body(tb_ref, sc_ref, q_ref, k_ref, v_ref, sub_ref, y_ref,
                qs_ref, m_ref, l_ref, acc_ref, tn_ref):
    tq = CHUNK
    h = pl.program_id(1)
    qi = pl.program_id(2)

    @pl.when(qi == 0)
    def _():
        ri = lax.broadcasted_iota(jnp.int32, (tq, 2 * CHUNK), 0)
        ci = lax.broadcasted_iota(jnp.int32, (tq, 2 * CHUNK), 1)
        dist = ri + CHUNK - ci
        for g in range(2):
            bias = jnp.where(dist >= 0, _rel_bias_minus_far(jnp.maximum(dist, 0), tb_ref, h * 2 + g), NEG_INF)
            for m in range(2):
                r = (g * 2 + m) * tq
                tn_ref[r:r + tq, :] = bias

    _diff_stack_q(q_ref[...], qs_ref, tq)

    def score_fn(rows, start, width, near):
        s = _dot_nt(qs_ref[rows, :], k_ref[pl.ds(start, width), :].astype(BF16))
        if near:
            s = s + tn_ref[rows, 2 * CHUNK - width:]
        return s

    def value_fn(start, width):
        return v_ref[pl.ds(start, width), :].astype(BF16)

    halves = (slice(0, 2 * tq), slice(2 * tq, 4 * tq))
    o = _two_pass_attention(qi, score_fn, value_fn, m_ref, l_ref, acc_ref, halves)
    y_ref[...] = _diff_finish(o, sc_ref, sub_ref[...], tq)


def _diff_prompt(tb, sc, dq, dk, dv, subln):
    B, L, _ = dq.shape
    tq = CHUNK
    assert CHUNK + 1 >= REL_FAR and L % tq == 0
    return pl.pallas_call(
        _diffp_body,
        grid=(B, DIFF_KVH, L // tq),
        in_specs=[_SMEM, _SMEM,
                  pl.BlockSpec((None, tq, 2 * LANES), lambda b, h, i: (b, i, h)),
                  pl.BlockSpec((None, L, LANES), lambda b, h, i: (b, 0, h)),
                  pl.BlockSpec((None, L, LANES), lambda b, h, i: (b, 0, h)),
                  _full((1, DIFF_DV))],
        out_specs=pl.BlockSpec((None, tq, 2 * LANES), lambda b, h, i: (b, i, h)),
        out_shape=jax.ShapeDtypeStruct((B, L, BRANCH_W), F32),
        scratch_shapes=[pltpu.VMEM((4 * tq, LANES), BF16), pltpu.VMEM((4 * tq, CHUNK), F32),
                        pltpu.VMEM((4 * tq, CHUNK), F32), pltpu.VMEM((4 * tq, LANES), F32),
                        pltpu.VMEM((4 * tq, 2 * CHUNK), F32)],
        compiler_params=pltpu.CompilerParams(dimension_semantics=("parallel", "parallel", "arbitrary")),
        name="diff_prompt",
    )(tb, sc, dq, dk, dv, subln)


MLA_SCALE = (MLA_NOPE + MLA_ROPE) ** -0.5


def _mla_q_prep(mq, qn, wq_ref, wuk_ref, cosf, sins, qs_ref, rows):
    cq = (_rms(mq) * qn).astype(BF16)
    qh = _dot(cq, wq_ref[...])
    for h in range(MLA_HEADS):
        nope = qh[:, h * 256:h * 256 + LANES].astype(BF16)
        rope = _rotate(qh[:, h * 256 + LANES:(h + 1) * 256], cosf, sins, MLA_ROPE // 2)
        qs_ref[h * rows:(h + 1) * rows, :LANES] = (_dot(nope, wuk_ref[h]) * MLA_SCALE).astype(BF16)
        qs_ref[h * rows:(h + 1) * rows, LANES:] = (rope * MLA_SCALE).astype(BF16)


def _mla_kv_prep(mkv, kvn, cosf, sins):
    c_kv = _rms(mkv[:, :LANES]) * kvn
    k_rope = _rotate(mkv[:, LANES:], cosf, sins, MLA_ROPE // 2)
    return c_kv, k_rope


def _mla_finish(o, wuv_ref, rows):
    o = o.astype(BF16)
    return jnp.concatenate([_dot(o[h * rows:(h + 1) * rows], wuv_ref[h]) for h in range(MLA_HEADS)], axis=-1)


def _mlap_body(mq_ref, mkv_ref, cq_ref, sq_ref, ck_ref, sk_ref, qn_ref, kvn_ref, wq_ref, wuk_ref, wuv_ref,
               y_ref, lat_ref, kr_ref, kcat_ref, qs_ref, m_ref, l_ref, acc_ref):
    tq = tk = CHUNK
    qi = pl.program_id(1)

    @pl.when(qi == 0)
    def _():
        c_kv, k_rope = _mla_kv_prep(mkv_ref[...], kvn_ref[...], ck_ref[...], sk_ref[...])
        lat_ref[...] = c_kv
        kr_ref[...] = k_rope[:, :MLA_ROPE]
        kcat_ref[:, :LANES] = c_kv.astype(BF16)
        kcat_ref[:, LANES:] = k_rope.astype(BF16)

    _mla_q_prep(mq_ref[...], qn_ref[...], wq_ref, wuk_ref, cq_ref[...], sq_ref[...], qs_ref, tq)

    def score_fn(rows, start, width, near):
        s = _dot_nt(qs_ref[rows, :], kcat_ref[pl.ds(start, width), :])
        if near:
            n = rows.stop - rows.start
            ri = lax.broadcasted_iota(jnp.int32, (n, width), 0) % tq
            ci = lax.broadcasted_iota(jnp.int32, (n, width), 1) - (width - CHUNK)
            s = jnp.where(ri >= ci, s, NEG_INF)
        return s

    def value_fn(start, width):
        return kcat_ref[pl.ds(start, width), :LANES]

    halves = (slice(0, 2 * tq), slice(2 * tq, 4 * tq))
    o = _two_pass_attention(qi, score_fn, value_fn, m_ref, l_ref, acc_ref, halves)
    y_ref[...] = _mla_finish(o, wuv_ref, tq)


def _mla_prompt(mq, mkv, cos, sin, w):
    B, L, _ = mq.shape
    tq = CHUNK
    return pl.pallas_call(
        _mlap_body,
        grid=(B, L // tq),
        in_specs=[pl.BlockSpec((None, tq, 256), lambda b, i: (b, i, 0)),
                  pl.BlockSpec((None, L, 256), lambda b, i: (b, 0, 0)),
                  pl.BlockSpec((tq, LANES), lambda b, i: (i, 0)), pl.BlockSpec((tq, LANES), lambda b, i: (i, 0)),
                  _full((L, LANES)), _full((L, LANES)),
                  _full((1, MLA_Q_RANK)), _full((1, MLA_KV_RANK)),
                  _full((MLA_Q_RANK, MLA_HEADS * 256)), _full((MLA_HEADS, LANES, LANES)),
                  _full((MLA_HEADS, LANES, LANES))],
        out_specs=[pl.BlockSpec((None, tq, BRANCH_W), lambda b, i: (b, i, 0)),
                   pl.BlockSpec((None, L, MLA_KV_RANK), lambda b, i: (b, 0, 0)),
                   pl.BlockSpec((None, L, MLA_ROPE), lambda b, i: (b, 0, 0))],
        out_shape=[jax.ShapeDtypeStruct((B, L, BRANCH_W), F32), jax.ShapeDtypeStruct((B, L, MLA_KV_RANK), F32),
                   jax.ShapeDtypeStruct((B, L, MLA_ROPE), F32)],
        scratch_shapes=[pltpu.VMEM((L, 2 * LANES), BF16), pltpu.VMEM((MLA_HEADS * tq, 2 * LANES), BF16),
                        pltpu.VMEM((MLA_HEADS * tq, CHUNK), F32), pltpu.VMEM((MLA_HEADS * tq, CHUNK), F32),
                        pltpu.VMEM((MLA_HEADS * tq, LANES), F32)],
        compiler_params=pltpu.CompilerParams(dimension_semantics=("parallel", "arbitrary")),
        name="mla_prompt",
    )(mq, mkv, cos, sin, cos, sin, w["mla_q_norm"], w["mla_kv_norm"], w["mla_wq"], w["mla_wuk"], w["mla_wuv"])


SROWS = 8


def _diffs_body(pt_ref, tb_ref, sc_ref, q_ref, kn_ref, vn_ref, sub_ref, *rest, npg, past_len):
    kt_refs = rest[:npg]
    v_refs = rest[npg:2 * npg]
    y_ref, qs_ref, m_ref, l_ref, acc_ref = rest[2 * npg:]
    j = pl.program_id(1)
    nj = pl.num_programs(1)
    width = npg * PAGE

    @pl.when(j == 0)
    def _():
        for h in range(DIFF_KVH):
            _diff_stack_q(q_ref[:, h * 256:(h + 1) * 256], qs_ref.at[h], SROWS)
        _softmax_init(m_ref, l_ref, acc_ref)

    def visit(h, bias):
        kt = jnp.concatenate([kt_refs[i][h * LANES:(h + 1) * LANES, :].astype(BF16) for i in range(npg)], axis=1)
        s = _dot(qs_ref[h], kt)
        if bias is not None:
            s = s + bias
        v = jnp.concatenate([v_refs[i][pl.ds(h, PAGE, stride=DIFF_KVH), :].astype(BF16) for i in range(npg)], axis=0)
        _softmax_step(s, v, m_ref.at[h], l_ref.at[h], acc_ref.at[h])

    @pl.when(j < nj - 1)
    def _():
        for h in range(DIFF_KVH):
            visit(h, None)

    @pl.when(j == nj - 1)
    def _():
        t_row = lax.broadcasted_iota(jnp.int32, (2 * SROWS, width), 0) % SROWS
        ci = lax.broadcasted_iota(jnp.int32, (2 * SROWS, width), 1)
        dist = past_len + t_row - ((nj - 1) * width + ci)
        t_new = lax.broadcasted_iota(jnp.int32, (2 * SROWS, PAGE), 0) % SROWS
        c_new = lax.broadcasted_iota(jnp.int32, (2 * SROWS, PAGE), 1)
        for h in range(DIFF_KVH):
            hs = slice(h * LANES, (h + 1) * LANES)
            visit(h, jnp.concatenate([_rel_bias_minus_far(dist, tb_ref, h * 2 + g) for g in range(2)], axis=0))
            kn = _pad_rows(kn_ref[:, hs], PAGE).astype(BF16)
            vn = _pad_rows(vn_ref[:, hs], PAGE).astype(BF16)
            bias_n = jnp.concatenate(
                [jnp.where(t_new >= c_new, _rel_bias_minus_far(jnp.maximum(t_new - c_new, 0), tb_ref, h * 2 + g), NEG_INF)
                 for g in range(2)], axis=0)
            _softmax_step(_dot_nt(qs_ref[h], kn) + bias_n, vn, m_ref.at[h], l_ref.at[h], acc_ref.at[h])
            y_ref[:, h * 256:(h + 1) * 256] = _diff_finish(acc_ref[h] / l_ref[h], sc_ref, sub_ref[...], SROWS)


def _pages_per_step(n_pages, want):
    return math.gcd(n_pages, want)


def _diff_sample(pt, tb, sc, dq, dk, dv, subln, cache_kt, cache_v, layer):
    B = dq.shape[0]
    n_pages = pt.shape[1]
    npg = _pages_per_step(n_pages, 16)
    R = 4 * SROWS
    tok = lambda width: pl.BlockSpec((None, SROWS, width), lambda b, j, pt: (b, 0, 0))
    page = lambda i: pl.BlockSpec((None, None, 2 * PAGE, LANES), lambda b, j, pt: (pt[b, j * npg + i], layer, 0, 0))
    grid_spec = pltpu.PrefetchScalarGridSpec(
        num_scalar_prefetch=1,
        grid=(B, n_pages // npg),
        in_specs=[_SMEM, _SMEM, tok(512), tok(256), tok(256), pl.BlockSpec((1, DIFF_DV), lambda b, j, pt: (0, 0))]
        + [page(i) for i in range(npg)] + [page(i) for i in range(npg)],
        out_specs=tok(BRANCH_W),
        scratch_shapes=[pltpu.VMEM((DIFF_KVH, R, LANES), BF16), pltpu.VMEM((DIFF_KVH, R, 1), F32),
                        pltpu.VMEM((DIFF_KVH, R, 1), F32), pltpu.VMEM((DIFF_KVH, R, LANES), F32)],
    )
    return pl.pallas_call(
        functools.partial(_diffs_body, npg=npg, past_len=n_pages * PAGE),
        grid_spec=grid_spec,
        out_shape=jax.ShapeDtypeStruct((B, SROWS, BRANCH_W), F32),
        compiler_params=pltpu.CompilerParams(dimension_semantics=("parallel", "arbitrary"),
                                             vmem_limit_bytes=VMEM_LIMIT),
        name="diff_sample",
    )(pt, tb, sc, dq, dk, dv, subln, *([cache_kt] * npg), *([cache_v] * npg))


def _mlas_body(pt_ref, mq_ref, mkv_ref, cos_ref, sin_ref, qn_ref, kvn_ref, wq_ref, wuk_ref, wuv_ref, *rest, npg):
    c_refs = rest[:npg]
    rt_refs = rest[npg:2 * npg]
    y_ref, lat_ref, kr_ref, qs_ref, kn_ref, m_ref, l_ref, acc_ref = rest[2 * npg:]
    j = pl.program_id(1)
    nj = pl.num_programs(1)
    R = MLA_HEADS * SROWS

    @pl.when(j == 0)
    def _():
        c_kv, k_rope = _mla_kv_prep(mkv_ref[...], kvn_ref[...], cos_ref[...], sin_ref[...])
        lat_ref[...] = c_kv
        kr_ref[...] = k_rope[:, :MLA_ROPE]
        kn_ref[...] = jnp.zeros(kn_ref.shape, BF16)
        kn_ref[0:SROWS, :LANES] = c_kv.astype(BF16)
        kn_ref[0:SROWS, LANES:] = k_rope.astype(BF16)
        _mla_q_prep(mq_ref[...], qn_ref[...], wq_ref, wuk_ref, cos_ref[...], sin_ref[...], qs_ref, SROWS)
        _softmax_init(m_ref, l_ref, acc_ref)

    lat = jnp.concatenate([c_refs[i][...].astype(BF16) for i in range(npg)], axis=0)
    krt = jnp.concatenate([rt_refs[i][...].astype(BF16) for i in range(npg)], axis=1)
    s = _dot_nt(qs_ref[:, :LANES], lat) + _dot(qs_ref[:, LANES:LANES + MLA_ROPE], krt)
    _softmax_step(s, lat, m_ref, l_ref, acc_ref)

    @pl.when(j == nj - 1)
    def _():
        kc = kn_ref[...]
        t_new = lax.broadcasted_iota(jnp.int32, (R, PAGE), 0) % SROWS
        c_new = lax.broadcasted_iota(jnp.int32, (R, PAGE), 1)
        sn = jnp.where(t_new >= c_new, _dot_nt(qs_ref[...], kc), NEG_INF)
        _softmax_step(sn, kc[:, :LANES], m_ref, l_ref, acc_ref)
        y_ref[...] = _mla_finish(acc_ref[...] / l_ref[...], wuv_ref, SROWS)


def _mla_sample(pt, mq, mkv, cos, sin, w, cache_c, cache_rt, layer):
    B = mq.shape[0]
    n_pages = pt.shape[1]
    npg = _pages_per_step(n_pages, 32)
    R = MLA_HEADS * SROWS
    tok = lambda width: pl.BlockSpec((None, SROWS, width), lambda b, j, pt: (b, 0, 0))
    full = lambda *shape: pl.BlockSpec(shape, lambda b, j, pt: (0,) * len(shape))
    page = lambda i, rows: pl.BlockSpec((None, None, rows, LANES), lambda b, j, pt: (pt[b, j * npg + i], layer, 0, 0))
    grid_spec = pltpu.PrefetchScalarGridSpec(
        num_scalar_prefetch=1,
        grid=(B, n_pages // npg),
        in_specs=[tok(256), tok(256), full(SROWS, LANES), full(SROWS, LANES), full(1, MLA_Q_RANK), full(1, MLA_KV_RANK),
                  full(MLA_Q_RANK, MLA_HEADS * 256), full(MLA_HEADS, LANES, LANES), full(MLA_HEADS, LANES, LANES)]
        + [page(i, PAGE) for i in range(npg)] + [page(i, MLA_ROPE) for i in range(npg)],
        out_specs=[tok(BRANCH_W), tok(MLA_KV_RANK), tok(MLA_ROPE)],
        scratch_shapes=[pltpu.VMEM((R, 2 * LANES), BF16), pltpu.VMEM((PAGE, 2 * LANES), BF16),
                        pltpu.VMEM((R, 1), F32), pltpu.VMEM((R, 1), F32), pltpu.VMEM((R, LANES), F32)],
    )
    return pl.pallas_call(
        functools.partial(_mlas_body, npg=npg),
        grid_spec=grid_spec,
        out_shape=[jax.ShapeDtypeStruct((B, SROWS, BRANCH_W), F32), jax.ShapeDtypeStruct((B, SROWS, MLA_KV_RANK), F32),
                   jax.ShapeDtypeStruct((B, SROWS, MLA_ROPE), F32)],
        compiler_params=pltpu.CompilerParams(dimension_semantics=("parallel", "arbitrary"),
                                             vmem_limit_bytes=VMEM_LIMIT),
        name="mla_sample",
    )(pt, mq, mkv, cos, sin, w["mla_q_norm"], w["mla_kv_norm"], w["mla_wq"], w["mla_wuk"], w["mla_wuv"],
      *([cache_c] * npg), *([cache_rt] * npg))


def _merge_body(x_ref, y0_ref, y1_ref, y2_ref, y3_ref, gt_ref, wb_ref, wo_ref, o_ref):
    mix = None
    for i, y_ref in enumerate((y0_ref, y1_ref, y2_ref, y3_ref)):
        proj = _dot(y_ref[...].astype(BF16), wb_ref[i])
        gate = gt_ref[:, i * D_MODEL:(i + 1) * D_MODEL].astype(F32)
        term = proj / (1.0 + jnp.exp(-gate))
        mix = term if mix is None else mix + term
    o_ref[...] = x_ref[...] + _dot(mix.astype(BF16), wo_ref[...])


def _merge(x, ys, gates, wb, wo):
    T = x.shape[0]
    tm = min(512, T)
    row = lambda width: pl.BlockSpec((tm, width), lambda i: (i, 0))
    return pl.pallas_call(
        _merge_body,
        grid=(T // tm,),
        in_specs=[row(D_MODEL)] + [row(BRANCH_W)] * 4 + [row(N_BRANCH * D_MODEL),
                                                       _resident((N_BRANCH, BRANCH_W, D_MODEL)),
                                                       _resident((D_MODEL, D_MODEL))],
        out_specs=row(D_MODEL),
        out_shape=jax.ShapeDtypeStruct((T, D_MODEL), F32),
        compiler_params=pltpu.CompilerParams(dimension_semantics=("parallel",), vmem_limit_bytes=VMEM_LIMIT),
        name="merge",
    )(x, *ys, gates, wb, wo)


def _rope_tables(pos, half):
    inv_freq = ROPE_BASE ** (-jnp.arange(half, dtype=F32) / half)
    ang = pos.astype(F32)[:, None] * inv_freq[None, :]
    cos, sin = jnp.cos(ang), jnp.sin(ang)
    return jnp.concatenate([cos, cos], axis=-1), jnp.concatenate([-sin, sin], axis=-1)


def _layer_weights(l, p):
    w = {}
    f = lambda a: a.astype(BF16)
    for n in ("ffn1_w1", "ffn1_w3", "ffn1_w2", "ffn2_w1", "ffn2_w3", "ffn2_w2", "w_out"):
        w[n] = f(p[n][l])
    w["w_branch"] = f(p["w_branch"][l])
    for n in ("norm_ffn1", "norm_mix", "norm_ffn2", "ssm_conv_b", "ssm_norm", "ret_norm", "diff_subln",
              "mla_q_norm", "mla_kv_norm"):
        w[n] = p[n][l][None, :]
    offs = np.cumsum((0,) + IN_SIZES)
    piece = lambda i: p["w_in"][l][:, offs[i]:offs[i + 1]]
    padc = lambda a, width: jnp.pad(a, ((0, 0), (0, width - a.shape[1])))
    cols = [piece(0), piece(1), padc(piece(2), LANES), piece(3), piece(4), piece(5), piece(6), piece(7), piece(8),
            piece(9), piece(10), padc(piece(11), 2 * LANES), piece(12)]
    w["w_in"] = f(jnp.concatenate(cols, axis=1))
    w["conv_w"] = p["ssm_conv_w"][l]
    w["conv_b"] = w["ssm_conv_b"]
    w["dt_bias"] = padc(p["ssm_dt_bias"][l][None, :], LANES)
    w["dt_biasT"] = p["ssm_dt_bias"][l][:, None]
    w["a_log"] = padc(p["ssm_a_log"][l][None, :], LANES)
    w["a_logT"] = p["ssm_a_log"][l][:, None]
    w["d_exp"] = jnp.repeat(p["ssm_d"][l], SSM_HEAD_DIM)[None, :]
    expand = np.zeros((LANES, BRANCH_W), np.float32)
    for h in range(SSM_HEADS):
        expand[h, h * SSM_HEAD_DIM:(h + 1) * SSM_HEAD_DIM] = 1.0
    w["expand"] = jnp.asarray(expand)
    wq = p["mla_w_q_up"][l]
    zq = jnp.zeros((MLA_Q_RANK, MLA_HEADS, LANES - MLA_NOPE), F32)
    zr = jnp.zeros((MLA_Q_RANK, MLA_HEADS, LANES - MLA_ROPE), F32)
    w["mla_wq"] = f(jnp.concatenate([wq[..., :MLA_NOPE], zq, wq[..., MLA_NOPE:], zr], axis=-1)
                    .reshape(MLA_Q_RANK, MLA_HEADS * 256))
    wuk = jnp.transpose(p["mla_w_uk"][l], (1, 2, 0))
    w["mla_wuk"] = f(jnp.pad(wuk, ((0, 0), (0, LANES - MLA_NOPE), (0, 0))))
    w["mla_wuv"] = f(jnp.transpose(p["mla_w_uv"][l], (1, 0, 2)))
    lam_init = 0.8 - 0.6 * math.exp(-0.3 * l)
    lw = p["diff_lambda"][l].astype(F32)
    lam = jnp.exp(jnp.sum(lw[0] * lw[1])) - jnp.exp(jnp.sum(lw[2] * lw[3])) + lam_init
    w["diff_sc"] = jnp.stack([lam, jnp.asarray(1.0 - lam_init, F32)]).astype(F32)
    return w


def _ssm_state_to_pairs(s):
    B = s.shape[0]
    return s.reshape(B, 4, 2, SSM_STATE, SSM_HEAD_DIM).transpose(0, 1, 3, 2, 4).reshape(B, 4, SSM_STATE, 2 * SSM_HEAD_DIM)


def _ssm_state_from_pairs(s):
    B = s.shape[0]
    return s.reshape(B, 4, SSM_STATE, 2, SSM_HEAD_DIM).transpose(0, 1, 3, 2, 4).reshape(B, SSM_HEADS, SSM_STATE, SSM_HEAD_DIM)


def _trunk(x, w_layers, rel_bias, final_norm, rows, valid, pos, conv0, ssm0, ret0, paged):
    B, L, _ = x.shape
    T = B * L
    depth = len(w_layers)
    xt = x.reshape(T, D_MODEL)
    ret_cos, ret_sin = _rope_tables(pos, RET_QK // 2)
    ret_cos, ret_sin = jnp.tile(ret_cos, (1, 4)), jnp.tile(ret_sin, (1, 4))
    mc, ms = _rope_tables(pos, MLA_ROPE // 2)
    mla_cos = jnp.pad(mc, ((0, 0), (0, LANES - MLA_ROPE)))
    mla_sin = jnp.pad(ms, ((0, 0), (0, LANES - MLA_ROPE)))
    nc = L // rows
    outs = []
    for l in range(depth):
        w = w_layers[l]
        xt = _ffn(xt, w["norm_ffn1"], w["ffn1_w1"], w["ffn1_w3"], w["ffn1_w2"])
        pr = _inproj(xt, w["norm_mix"], w["w_in"])
        r3 = lambda a: a.reshape(B, L, a.shape[-1])
        dt = r3(pr["dt"])
        dtT = jnp.transpose(dt[:, :, :SSM_HEADS].reshape(B, nc, rows, SSM_HEADS), (0, 3, 1, 2))
        dtT = jnp.pad(dtT, ((0, 0), (0, 0), (0, 0), (0, CHUNK - rows))).reshape(B, SSM_HEADS, nc * CHUNK)
        dtp = dt if rows == CHUNK else jnp.pad(dt, ((0, 0), (0, CHUNK - rows), (0, 0)))
        y_ssm, conv_new, ssm_new = _ssd(r3(pr["xbc"]), r3(pr["z"]), dtp, dtT, conv0[l], ssm0[l], w, rows, valid)
        y_ret, ret_new = _ret(r3(pr["rq"]), r3(pr["rk"]), r3(pr["rv"]), r3(pr["rg"]), ret_cos, ret_sin, ret0[l],
                              w["ret_norm"], rows, valid)
        dk, dv = r3(pr["dk"]), r3(pr["dv"])
        if paged is None:
            y_diff = _diff_prompt(rel_bias, w["diff_sc"], r3(pr["dq"]), dk, dv, w["diff_subln"])
            y_mla, lat, kr = _mla_prompt(r3(pr["mq"]), r3(pr["mkv"]), mla_cos, mla_sin, w)
        else:
            pt, ck, cv, cc, cr = paged
            y_diff = _diff_sample(pt, rel_bias, w["diff_sc"], r3(pr["dq"]), dk, dv, w["diff_subln"], ck, cv, l)
            y_mla, lat, kr = _mla_sample(pt, r3(pr["mq"]), r3(pr["mkv"]), mla_cos, mla_sin, w, cc, cr, l)
        ys = [a.reshape(T, BRANCH_W) for a in (y_ssm, y_ret, y_diff, y_mla)]
        xt = _merge(xt, ys, pr["gates"], w["w_branch"], w["w_out"])
        fg = final_norm if l == depth - 1 else None
        xt = _ffn(xt, w["norm_ffn2"], w["ffn2_w1"], w["ffn2_w3"], w["ffn2_w2"], fg)
        outs.append((dk, dv, lat, kr, conv_new, ssm_new, ret_new))
    return xt.reshape(B, L, D_MODEL), outs


def kernel(x_prompt, x_sample, cache_diff_k, cache_diff_v, cache_mla_latent, cache_mla_krope, state_ssm_conv, state_ssm, state_retention, page_table, norm_ffn1, ffn1_w1, ffn1_w3, ffn1_w2, norm_mix, w_in, ssm_conv_w, ssm_conv_b, ssm_dt_bias, ssm_a_log, ssm_d, ssm_norm, ret_norm, diff_lambda, diff_subln, mla_q_norm, mla_w_q_up, mla_kv_norm, mla_w_uk, mla_w_uv, rel_bias, w_branch, w_out, norm_ffn2, ffn2_w1, ffn2_w3, ffn2_w2, final_norm):
    p = dict(norm_ffn1=norm_ffn1, ffn1_w1=ffn1_w1, ffn1_w3=ffn1_w3, ffn1_w2=ffn1_w2, norm_mix=norm_mix, w_in=w_in,
             ssm_conv_w=ssm_conv_w, ssm_conv_b=ssm_conv_b, ssm_dt_bias=ssm_dt_bias, ssm_a_log=ssm_a_log, ssm_d=ssm_d,
             ssm_norm=ssm_norm, ret_norm=ret_norm, diff_lambda=diff_lambda, diff_subln=diff_subln,
             mla_q_norm=mla_q_norm, mla_w_q_up=mla_w_q_up, mla_kv_norm=mla_kv_norm, mla_w_uk=mla_w_uk,
             mla_w_uv=mla_w_uv, w_branch=w_branch, w_out=w_out, norm_ffn2=norm_ffn2, ffn2_w1=ffn2_w1,
             ffn2_w3=ffn2_w3, ffn2_w2=ffn2_w2)
    depth = w_in.shape[0]
    w_layers = [_layer_weights(l, p) for l in range(depth)]
    fnorm = final_norm[None, :]
    rel = rel_bias.astype(F32)

    Bp, Lp, _ = x_prompt.shape
    zc = jnp.zeros((Bp, CONV_W - 1, SSM_CONV_DIM), F32)
    zs = jnp.zeros((Bp, 4, LANES, LANES), F32)
    zr = jnp.zeros((Bp, 2 * LANES, LANES), F32)
    y_prompt, rows_p = _trunk(x_prompt, w_layers, rel, fnorm, CHUNK, CHUNK, jnp.arange(Lp, dtype=jnp.int32),
                              [zc] * depth, [zs] * depth, [zr] * depth, None)

    Bs, Ls, _ = x_sample.shape
    n_pages = page_table.shape[1]
    past_len = n_pages * PAGE
    xs = jnp.pad(x_sample, ((0, 0), (0, SROWS - Ls), (0, 0)))
    pos_s = past_len + jnp.arange(SROWS, dtype=jnp.int32)
    n_pool = cache_diff_k.shape[0]
    paged = (page_table.astype(jnp.int32),
             jnp.transpose(cache_diff_k, (0, 1, 3, 4, 5, 2)).reshape(n_pool, depth, 2 * PAGE, LANES),
             cache_diff_v.reshape(n_pool, depth, 2 * PAGE, LANES),
             cache_mla_latent, jnp.transpose(cache_mla_krope, (0, 1, 3, 2)))
    conv_s = [state_ssm_conv[:, l] for l in range(depth)]
    ssm_s = [_ssm_state_to_pairs(state_ssm[:, l]) for l in range(depth)]
    ret_s = [state_retention[:, l].reshape(Bs, 2 * LANES, LANES) for l in range(depth)]
    y_sample, rows_s = _trunk(xs, w_layers, rel, fnorm, SROWS, Ls, pos_s, conv_s, ssm_s, ret_s, paged)

    def collect(rows, B, L, keep):
        st = lambda i: jnp.stack([r[i] for r in rows], axis=1)
        dk = st(0)[:, :, :keep].reshape(B, depth, keep, DIFF_KVH, 2, DIFF_DH)
        dv = st(1)[:, :, :keep].reshape(B, depth, keep, DIFF_KVH, DIFF_DV)
        lat = st(2)[:, :, :keep]
        kr = st(3)[:, :, :keep]
        conv = st(4)
        ssm = jnp.stack([_ssm_state_from_pairs(r[5]) for r in rows], axis=1)
        ret = st(6).reshape(B, depth, RET_HEADS, RET_QK, RET_V)
        return dk, dv, lat, kr, conv, ssm, ret

    return (y_prompt, y_sample[:, :Ls]) + collect(rows_p, Bp, Lp, Lp) + collect(rows_s, Bs, SROWS, Ls)
```

```python
import functools
import math

import numpy as np
import jax
import jax.numpy as jnp
from jax import lax
from jax.experimental import pallas as pl
from jax.experimental.pallas import tpu as pltpu

F32 = jnp.float32
BF16 = jnp.bfloat16
NEG_INF = float("-inf")

D_MODEL = 1024
D_FF = 2816
BRANCH_W = 512
N_BRANCH = 4
SSM_HEADS = 8
SSM_HEAD_DIM = 64
SSM_STATE = 128
CONV_W = 4
SSM_CONV_DIM = 1024
RET_HEADS = 4
RET_QK = 64
RET_V = 128
DIFF_KVH = 2
DIFF_DH = 64
DIFF_DV = 128
MLA_HEADS = 4
MLA_Q_RANK = 256
MLA_KV_RANK = 128
MLA_NOPE = 64
MLA_ROPE = 32
PAGE = 128
REL_BUCKETS = 32
REL_MAX_EXACT = 16
REL_MAX_DIST = 128
ROPE_BASE = 10000.0
NORM_EPS = 1e-6
IN_SIZES = (512, 1024, 8, 256, 256, 512, 512, 512, 256, 256, 256, 160, 4096)

LANES = 128
CHUNK = 128
VMEM_LIMIT = 56 * 1024 * 1024


def _rel_lower_bounds():
    lb = list(range(REL_MAX_EXACT))
    d = np.arange(REL_MAX_EXACT, 4 * REL_MAX_DIST).astype(np.float32)
    large = REL_MAX_EXACT + (np.log(d / np.float32(REL_MAX_EXACT)) / np.float32(math.log(REL_MAX_DIST / REL_MAX_EXACT))
                             * np.float32(REL_BUCKETS - REL_MAX_EXACT)).astype(np.int32)
    large = np.minimum(large, REL_BUCKETS - 1)
    for bk in range(REL_MAX_EXACT, REL_BUCKETS):
        lb.append(int(REL_MAX_EXACT + np.argmax(large >= bk)))
    return tuple(lb)


REL_LB = _rel_lower_bounds()
REL_FAR = REL_LB[-1]
RET_LOG_GAMMA = tuple(math.log1p(-(2.0 ** (-5.0 - h))) for h in range(RET_HEADS))


def _dot(a, b, precision=None):
    return jnp.dot(a, b, preferred_element_type=F32, precision=precision)


def _dot_nt(a, b):
    return lax.dot_general(a, b, (((1,), (1,)), ((), ())), preferred_element_type=F32)


def _dot_tn(a, b):
    return lax.dot_general(a, b, (((0,), (0,)), ((), ())), preferred_element_type=F32)


def _rms(x):
    return x * lax.rsqrt(jnp.mean(x * x, axis=-1, keepdims=True) + NORM_EPS)


def _silu(x):
    return x / (1.0 + jnp.exp(-x))


def _softplus(x):
    return jnp.maximum(x, 0.0) + jnp.log1p(jnp.exp(-jnp.abs(x)))


def _pad_rows(x, rows):
    if x.shape[0] == rows:
        return x
    return jnp.concatenate([x, jnp.zeros((rows - x.shape[0],) + x.shape[1:], x.dtype)], axis=0)


def _rotate(x, cosf, sins, half):
    outs = []
    for s in range(x.shape[-1] // LANES):
        xs = x[:, s * LANES:(s + 1) * LANES]
        lane = lax.broadcasted_iota(jnp.int32, xs.shape, 1)
        first = (lane % (2 * half)) < half
        partner = jnp.where(first, pltpu.roll(xs, LANES - half, axis=1), pltpu.roll(xs, half, axis=1))
        outs.append(partner)
    partner = outs[0] if len(outs) == 1 else jnp.concatenate(outs, axis=-1)
    return x * cosf + partner * sins


def _full(shape):
    return pl.BlockSpec(shape, lambda *_: (0,) * len(shape))


def _resident(shape):
    return pl.BlockSpec(shape, lambda *_: (0,) * len(shape), pipeline_mode=pl.Buffered(1))


_SMEM = pl.BlockSpec(memory_space=pltpu.SMEM)


def _ffn_body(*refs, fc, final):
    if final:
        x_ref, g_ref, w1_ref, w3_ref, w2_ref, fg_ref, o_ref = refs
    else:
        x_ref, g_ref, w1_ref, w3_ref, w2_ref, o_ref = refs
    x = x_ref[...]
    xn = (_rms(x) * g_ref[...]).astype(BF16)
    acc = x
    for c in range(D_FF // fc):
        sl = slice(c * fc, (c + 1) * fc)
        h1 = _dot(xn, w1_ref[:, sl])
        h3 = _dot(xn, w3_ref[:, sl])
        a = (_silu(h1) * h3).astype(BF16)
        acc = acc + 0.5 * _dot(a, w2_ref[sl, :])
    if final:
        acc = _rms(acc) * fg_ref[...]
    o_ref[...] = acc


def _ffn(x, g, w1, w3, w2, final_g=None):
    T = x.shape[0]
    tm = min(512, T)
    final = final_g is not None
    in_specs = [pl.BlockSpec((tm, D_MODEL), lambda i: (i, 0)), _full((1, D_MODEL)),
                _resident((D_MODEL, D_FF)), _resident((D_MODEL, D_FF)), _resident((D_FF, D_MODEL))]
    args = [x, g, w1, w3, w2]
    if final:
        in_specs.append(_full((1, D_MODEL)))
        args.append(final_g)
    return pl.pallas_call(
        functools.partial(_ffn_body, fc=D_FF // 2, final=final),
        grid=(T // tm,),
        in_specs=in_specs,
        out_specs=pl.BlockSpec((tm, D_MODEL), lambda i: (i, 0)),
        out_shape=jax.ShapeDtypeStruct((T, D_MODEL), F32),
        compiler_params=pltpu.CompilerParams(dimension_semantics=("parallel",), vmem_limit_bytes=VMEM_LIMIT),
        name="ffn",
    )(*args)


PROJ_OUT = (("z", 512, BF16), ("xbc", 1024, F32), ("dt", 128, F32), ("rq", 256, BF16), ("rk", 256, BF16),
            ("rv", 512, BF16), ("rg", 512, BF16), ("dq", 512, BF16), ("dk", 256, F32), ("dv", 256, F32),
            ("mq", 256, BF16), ("mkv", 256, F32), ("gates", 4096, BF16))
PROJ_W = sum(w for _, w, _ in PROJ_OUT)


def _inproj_body(x_ref, g_ref, w_ref, *o_refs):
    xn = (_rms(x_ref[...]) * g_ref[...]).astype(BF16)
    off = 0
    for (_, w, dt), o_ref in zip(PROJ_OUT, o_refs):
        o_ref[...] = _dot(xn, w_ref[:, off:off + w]).astype(dt)
        off += w


def _inproj(x, g, w):
    T = x.shape[0]
    tm = min(512, T)
    outs = pl.pallas_call(
        _inproj_body,
        grid=(T // tm,),
        in_specs=[pl.BlockSpec((tm, D_MODEL), lambda i: (i, 0)), _full((1, D_MODEL)), _resident((D_MODEL, PROJ_W))],
        out_specs=[pl.BlockSpec((tm, w_), lambda i: (i, 0)) for _, w_, _ in PROJ_OUT],
        out_shape=[jax.ShapeDtypeStruct((T, w_), dt) for _, w_, dt in PROJ_OUT],
        compiler_params=pltpu.CompilerParams(dimension_semantics=("parallel",), vmem_limit_bytes=VMEM_LIMIT),
        name="inproj",
    )(x, g, w)
    return {n: o for (n, _, _), o in zip(PROJ_OUT, outs)}


def _ssd_body(xbc_ref, z_ref, dt_ref, dtT_ref, cb0_ref, s0_ref, cw_ref, cbias_ref, dtb_ref, dtbT_ref,
              alog_ref, alogT_ref, dexp_ref, nw_ref, e_ref,
              y_ref, cnew_ref, snew_ref, xp_ref, s_ref, *, rows, valid):
    Q = CHUNK
    c = pl.program_id(1)

    @pl.when(c == 0)
    def _():
        xp_ref[0:8, :] = jnp.zeros((8, SSM_CONV_DIM), F32)
        xp_ref[5:8, :] = cb0_ref[...]
        s_ref[...] = s0_ref[...]

    xp_ref[8:8 + Q, :] = _pad_rows(xbc_ref[...], Q)
    conv = cbias_ref[...] + xp_ref[5:5 + Q, :] * cw_ref[0:1, :]
    for w in range(1, CONV_W):
        conv = conv + xp_ref[5 + w:5 + w + Q, :] * cw_ref[w:w + 1, :]
    u = _silu(conv)
    last3 = xp_ref[8 + valid - 3:8 + valid, :]
    xp_ref[5:8, :] = last3
    cnew_ref[...] = last3

    row = lax.broadcasted_iota(jnp.int32, (Q, Q), 0)
    col = lax.broadcasted_iota(jnp.int32, (Q, Q), 1)
    causal = row >= col
    hi = lax.Precision.HIGHEST

    dtv = _softplus(dt_ref[...] + dtb_ref[...])
    dtv = jnp.where(lax.broadcasted_iota(jnp.int32, dtv.shape, 0) < valid, dtv, 0.0)
    la = -jnp.exp(alog_ref[...]) * dtv
    cum = _dot(causal.astype(F32), la, hi)
    dtT = _softplus(dtT_ref[...] + dtbT_ref[...])
    dtT = jnp.where(lax.broadcasted_iota(jnp.int32, dtT.shape, 1) < valid, dtT, 0.0)
    laT = -jnp.exp(alogT_ref[...]) * dtT
    cumT = _dot(laT, (row <= col).astype(F32), hi)
    cum_exp = _dot(cum, e_ref[...], hi)
    dt_exp = _dot(dtv, e_ref[...], hi)
    cl_exp = cum_exp[Q - 1:Q, :]
    ecum = jnp.exp(cum_exp)
    tail = jnp.exp(cl_exp - cum_exp)
    sdecay = jnp.exp(cl_exp)

    xs = u[:, :BRANCH_W]
    v = xs * dt_exp
    vb = v.astype(BF16)
    vt = (v * tail).astype(BF16)
    lane = lax.broadcasted_iota(jnp.int32, (Q, LANES), 1)
    ys = []
    for g in range(2):
        bg = u[:, 512 + g * 128:512 + (g + 1) * 128].astype(BF16)
        cg = u[:, 768 + g * 128:768 + (g + 1) * 128].astype(BF16)
        gmat = _dot_nt(cg, bg)
        for pp in range(2):
            p = g * 2 + pp
            sl = slice(p * LANES, (p + 1) * LANES)
            s_pair = s_ref[p]
            inter = _dot(cg, s_pair.astype(BF16))
            yh = []
            for hh in range(2):
                h = 2 * p + hh
                seg = cum[:, h:h + 1] - cumT[h:h + 1, :]
                dec = jnp.exp(jnp.where(causal, seg, NEG_INF))
                yh.append(_dot((gmat * dec).astype(BF16), vb[:, sl]))
            ys.append(jnp.where(lane < SSM_HEAD_DIM, yh[0], yh[1]) + inter * ecum[:, sl])
            s_ref[p] = s_pair * sdecay[:, sl] + _dot_tn(bg, vt[:, sl])
    y = jnp.concatenate(ys, axis=-1) + xs * dexp_ref[...]
    y = y * _silu(_pad_rows(z_ref[...].astype(F32), Q))
    half = BRANCH_W // 2
    y = jnp.concatenate([_rms(y[:, :half]), _rms(y[:, half:])], axis=-1) * nw_ref[...]
    y_ref[...] = y[:rows]
    snew_ref[...] = s_ref[...]


def _ssd(xbc, z, dt, dtT, cb0, s0, w, rows, valid):
    B, L, _ = xbc.shape
    nc = L // rows
    seq = lambda width: pl.BlockSpec((None, rows, width), lambda b, c: (b, c, 0))
    per_b = lambda *shape: pl.BlockSpec((None,) + shape, lambda b, c: (b,) + (0,) * len(shape))
    return pl.pallas_call(
        functools.partial(_ssd_body, rows=rows, valid=valid),
        grid=(B, nc),
        in_specs=[seq(SSM_CONV_DIM), seq(BRANCH_W), pl.BlockSpec((None, CHUNK, LANES), lambda b, c: (b, c, 0)),
                  pl.BlockSpec((None, SSM_HEADS, CHUNK), lambda b, c: (b, 0, c)),
                  per_b(CONV_W - 1, SSM_CONV_DIM), per_b(4, LANES, LANES),
                  _full((CONV_W, SSM_CONV_DIM)), _full((1, SSM_CONV_DIM)), _full((1, LANES)), _full((SSM_HEADS, 1)),
                  _full((1, LANES)), _full((SSM_HEADS, 1)), _full((1, BRANCH_W)), _full((1, BRANCH_W)),
                  _full((LANES, BRANCH_W))],
        out_specs=[seq(BRANCH_W), per_b(CONV_W - 1, SSM_CONV_DIM), per_b(4, LANES, LANES)],
        out_shape=[jax.ShapeDtypeStruct((B, L, BRANCH_W), F32),
                   jax.ShapeDtypeStruct((B, CONV_W - 1, SSM_CONV_DIM), F32),
                   jax.ShapeDtypeStruct((B, 4, LANES, LANES), F32)],
        scratch_shapes=[pltpu.VMEM((8 + CHUNK, SSM_CONV_DIM), F32), pltpu.VMEM((4, LANES, LANES), F32)],
        compiler_params=pltpu.CompilerParams(dimension_semantics=("parallel", "arbitrary")),
        name="ssd",
    )(xbc, z, dt, dtT, cb0, s0, w["conv_w"], w["conv_b"], w["dt_bias"], w["dt_biasT"], w["a_log"], w["a_logT"],
      w["d_exp"], w["ssm_norm"], w["expand"])


def _ret_body(q_ref, k_ref, v_ref, g_ref, cos_ref, sin_ref, s0_ref, nw_ref, y_ref, snew_ref, s_ref, *, rows, valid):
    Q = CHUNK
    c = pl.program_id(1)

    @pl.when(c == 0)
    def _():
        s_ref[...] = s0_ref[...]

    cosf = _pad_rows(cos_ref[...], Q)
    sins = _pad_rows(sin_ref[...], Q)
    q = _rotate(_pad_rows(q_ref[...].astype(F32), Q), cosf, sins, RET_QK // 2)
    k = _rotate(_pad_rows(k_ref[...].astype(F32), Q), cosf, sins, RET_QK // 2) * (RET_QK ** -0.5)
    v = _pad_rows(v_ref[...].astype(F32), Q)
    gate = _pad_rows(g_ref[...].astype(F32), Q)
    ri = lax.broadcasted_iota(jnp.int32, (Q, 1), 0)
    ci = lax.broadcasted_iota(jnp.int32, (1, Q), 1)
    cnt_i = jnp.minimum(ri + 1, valid).astype(F32)
    cnt_j = jnp.minimum(ci + 1, valid).astype(F32)
    causal = ri >= ci
    lane = lax.broadcasted_iota(jnp.int32, (1, LANES), 1)
    ys = []
    for h in range(RET_HEADS):
        lg = RET_LOG_GAMMA[h]
        p, hh = divmod(h, 2)
        mine = (lane // RET_QK) == hh
        qm = jnp.where(mine, q[:, p * LANES:(p + 1) * LANES], 0.0).astype(BF16)
        km = jnp.where(mine, k[:, p * LANES:(p + 1) * LANES], 0.0).astype(BF16)
        dec = jnp.exp(jnp.where(causal, (cnt_i - cnt_j) * lg, NEG_INF))
        sc = (_dot_nt(qm, km) * dec).astype(BF16)
        vh = jnp.where(ri < valid, v[:, h * RET_V:(h + 1) * RET_V], 0.0)
        s_stack = s_ref[p * LANES:(p + 1) * LANES, :]
        y = _dot(sc, vh.astype(BF16)) + _dot(qm, s_stack.astype(BF16)) * jnp.exp(cnt_i * lg)
        tail = jnp.exp((valid - cnt_i) * lg)
        upd = _dot_tn(km, (vh * tail).astype(BF16))
        rs = slice(h * RET_QK, (h + 1) * RET_QK)
        s_ref[rs, :] = s_ref[rs, :] * math.exp(valid * lg) + upd[hh * RET_QK:(hh + 1) * RET_QK, :]
        mu = jnp.mean(y, axis=-1, keepdims=True)
        yc = y - mu
        var = jnp.mean(yc * yc, axis=-1, keepdims=True)
        ys.append(yc * lax.rsqrt(var + NORM_EPS))
    y = jnp.concatenate(ys, axis=-1) * nw_ref[...] * _silu(gate)
    y_ref[...] = y[:rows]
    snew_ref[...] = s_ref[...]


def _ret(q, k, v, g, cos, sin, s0, nw, rows, valid):
    B, L, _ = q.shape
    nc = L // rows
    seq = lambda width: pl.BlockSpec((None, rows, width), lambda b, c: (b, c, 0))
    tab = pl.BlockSpec((rows, 2 * LANES), lambda b, c: (c, 0))
    st = pl.BlockSpec((None, 2 * LANES, LANES), lambda b, c: (b, 0, 0))
    return pl.pallas_call(
        functools.partial(_ret_body, rows=rows, valid=valid),
        grid=(B, nc),
        in_specs=[seq(256), seq(256), seq(512), seq(512), tab, tab, st, _full((1, BRANCH_W))],
        out_specs=[seq(BRANCH_W), st],
        out_shape=[jax.ShapeDtypeStruct((B, L, BRANCH_W), F32), jax.ShapeDtypeStruct((B, 2 * LANES, LANES), F32)],
        scratch_shapes=[pltpu.VMEM((2 * LANES, LANES), F32)],
        compiler_params=pltpu.CompilerParams(dimension_semantics=("parallel", "arbitrary")),
        name="retention",
    )(q, k, v, g, cos, sin, s0, nw)


def _rel_bias_minus_far(dist, tb_ref, head):
    far = tb_ref[REL_BUCKETS - 1, head]
    val = jnp.full(dist.shape, tb_ref[0, head] - far, F32)
    for bk in range(1, REL_BUCKETS):
        val = jnp.where(dist >= REL_LB[bk], tb_ref[bk, head] - far, val)
    return val


def _softmax_step(s, v_bf16, m_ref, l_ref, acc_ref):
    m_prev = m_ref[...]
    m_new = jnp.maximum(m_prev, jnp.max(s, axis=-1, keepdims=True))
    alpha = jnp.exp(m_prev - m_new)
    p = jnp.exp(s - m_new)
    l_ref[...] = alpha * l_ref[...] + jnp.sum(p, axis=-1, keepdims=True)
    acc_ref[...] = alpha * acc_ref[...] + _dot(p.astype(BF16), v_bf16)
    m_ref[...] = m_new


def _softmax_init(m_ref, l_ref, acc_ref):
    m_ref[...] = jnp.full(m_ref.shape, NEG_INF, F32)
    l_ref[...] = jnp.zeros(l_ref.shape, F32)
    acc_ref[...] = jnp.zeros(acc_ref.shape, F32)


def _diff_stack_q(q, qs_ref, rows):
    lane = lax.broadcasted_iota(jnp.int32, (1, LANES), 1)
    for g in range(2):
        qg = q[:, g * LANES:(g + 1) * LANES] * (DIFF_DH ** -0.5)
        for m in range(2):
            r = (g * 2 + m) * rows
            qs_ref[r:r + rows, :] = jnp.where((lane // DIFF_DH) == m, qg, 0.0).astype(BF16)


def _diff_finish(o, sc_ref, subln, rows):
    outs = []
    for g in range(2):
        og = o[(2 * g) * rows:(2 * g + 1) * rows] - sc_ref[0] * o[(2 * g + 1) * rows:(2 * g + 2) * rows]
        outs.append(_rms(og) * subln * sc_ref[1])
    return jnp.concatenate(outs, axis=-1)


FAR_TILES = 4


def _causal_sweep(qi, visit):
    n_far = jnp.maximum(qi - 1, 0)
    rem = n_far % FAR_TILES
    for r in range(1, FAR_TILES):
        @pl.when(rem == r)
        def _(r=r):
            visit(0, r, False)

    def far(i, carry):
        visit(rem + i * FAR_TILES, FAR_TILES, False)
        return carry

    lax.fori_loop(0, n_far // FAR_TILES, far, 0)

    @pl.when(qi == 0)
    def _():
        visit(0, 1, True)

    @pl.when(qi >= 1)
    def _():
        visit(qi - 1, 2, True)


def _key_rows(tile0, ntiles):
    start = tile0 * CHUNK
    return pl.ds(start if isinstance(start, int) else pl.multiple_of(start, CHUNK), ntiles * CHUNK)


def _two_pass_attention(qi, score_fn, value_fn, s_ref, m_ref, l_ref, acc_ref, halves):
    m_ref[...] = jnp.full(m_ref.shape, NEG_INF, F32)

    def track_max(tile0, ntiles, near):
        for rows in halves:
            s = score_fn(rows, tile0, ntiles, near)
            mx = None
            for c in range(ntiles):
                sc = s[:, c * CHUNK:(c + 1) * CHUNK]
                s_ref[tile0 + c, rows, :] = sc
                mx = sc if mx is None else jnp.maximum(mx, sc)
            m_ref[rows, :] = jnp.maximum(m_ref[rows, :], mx)

    _causal_sweep(qi, track_max)
    m_ref[...] = jnp.broadcast_to(jnp.max(m_ref[...], axis=-1, keepdims=True), m_ref.shape)
    l_ref[...] = jnp.zeros(l_ref.shape, F32)
    acc_ref[...] = jnp.zeros(acc_ref.shape, F32)

    def accumulate(tile0, ntiles, near):
        v = value_fn(tile0, ntiles)
        for rows in halves:
            m = m_ref[rows, :]
            ps = [jnp.exp(s_ref[tile0 + c, rows, :] - m) for c in range(ntiles)]
            lsum = ps[0]
            for pc in ps[1:]:
                lsum = lsum + pc
            l_ref[rows, :] += lsum
            p = ps[0] if len(ps) == 1 else jnp.concatenate(ps, axis=-1)
            acc_ref[rows, :] += _dot(p.astype(BF16), v)

    _causal_sweep(qi, accumulate)
    return acc_ref[...] / jnp.sum(l_ref[...], axis=-1, keepdims=True)


def _diffp_body(tb_ref, sc_ref, q_ref, k_ref, v_ref, sub_ref, y_ref,
                qs_ref, s_ref, m_ref, l_ref, acc_ref, tn_ref):
    tq = CHUNK
    h = pl.program_id(1)
    qi = pl.program_id(2)

    @pl.when(qi == 0)
    def _():
        ri = lax.broadcasted_iota(jnp.int32, (tq, 2 * CHUNK), 0)
        ci = lax.broadcasted_iota(jnp.int32, (tq, 2 * CHUNK), 1)
        dist = ri + CHUNK - ci
        for g in range(2):
            bias = jnp.where(dist >= 0, _rel_bias_minus_far(jnp.maximum(dist, 0), tb_ref, h * 2 + g), NEG_INF)
            for m in range(2):
                r = (g * 2 + m) * tq
                tn_ref[r:r + tq, :] = bias

    _diff_stack_q(q_ref[...].astype(F32), qs_ref, tq)

    def score_fn(rows, tile0, ntiles, near):
        s = _dot_nt(qs_ref[rows, :], k_ref[_key_rows(tile0, ntiles), :].astype(BF16))
        if near:
            s = s + tn_ref[rows, (2 - ntiles) * CHUNK:]
        return s

    def value_fn(tile0, ntiles):
        return v_ref[_key_rows(tile0, ntiles), :].astype(BF16)

    halves = (slice(0, 2 * tq), slice(2 * tq, 4 * tq))
    o = _two_pass_attention(qi, score_fn, value_fn, s_ref, m_ref, l_ref, acc_ref, halves)
    y_ref[...] = _diff_finish(o, sc_ref, sub_ref[...], tq)


def _diff_prompt(tb, sc, dq, dk, dv, subln):
    B, L, _ = dq.shape
    tq = CHUNK
    assert CHUNK + 1 >= REL_FAR and L % tq == 0
    return pl.pallas_call(
        _diffp_body,
        grid=(B, DIFF_KVH, L // tq),
        in_specs=[_SMEM, _SMEM,
                  pl.BlockSpec((None, tq, 2 * LANES), lambda b, h, i: (b, i, h)),
                  pl.BlockSpec((None, L, LANES), lambda b, h, i: (b, 0, h)),
                  pl.BlockSpec((None, L, LANES), lambda b, h, i: (b, 0, h)),
                  _full((1, DIFF_DV))],
        out_specs=pl.BlockSpec((None, tq, 2 * LANES), lambda b, h, i: (b, i, h)),
        out_shape=jax.ShapeDtypeStruct((B, L, BRANCH_W), F32),
        scratch_shapes=[pltpu.VMEM((4 * tq, LANES), BF16), pltpu.VMEM((L // CHUNK, 4 * tq, CHUNK), F32),
                        pltpu.VMEM((4 * tq, CHUNK), F32), pltpu.VMEM((4 * tq, CHUNK), F32),
                        pltpu.VMEM((4 * tq, LANES), F32), pltpu.VMEM((4 * tq, 2 * CHUNK), F32)],
        compiler_params=pltpu.CompilerParams(dimension_semantics=("parallel", "parallel", "arbitrary"),
                                             vmem_limit_bytes=VMEM_LIMIT),
        name="diff_prompt",
    )(tb, sc, dq, dk, dv, subln)


MLA_SCALE = (MLA_NOPE + MLA_ROPE) ** -0.5


def _mla_q_prep(mq, qn, wq_ref, wuk_ref, cosf, sins, qs_ref, rows):
    cq = (_rms(mq) * qn).astype(BF16)
    qh = _dot(cq, wq_ref[...])
    for h in range(MLA_HEADS):
        nope = qh[:, h * 256:h * 256 + LANES].astype(BF16)
        rope = _rotate(qh[:, h * 256 + LANES:(h + 1) * 256], cosf, sins, MLA_ROPE // 2)
        qs_ref[h * rows:(h + 1) * rows, :LANES] = (_dot(nope, wuk_ref[h]) * MLA_SCALE).astype(BF16)
        qs_ref[h * rows:(h + 1) * rows, LANES:] = (rope * MLA_SCALE).astype(BF16)


def _mla_kv_prep(mkv, kvn, cosf, sins):
    c_kv = _rms(mkv[:, :LANES]) * kvn
    k_rope = _rotate(mkv[:, LANES:], cosf, sins, MLA_ROPE // 2)
    return c_kv, k_rope


def _mla_finish(o, wuv_ref, rows):
    o = o.astype(BF16)
    return jnp.concatenate([_dot(o[h * rows:(h + 1) * rows], wuv_ref[h]) for h in range(MLA_HEADS)], axis=-1)


def _mlap_body(mq_ref, mkv_ref, cq_ref, sq_ref, ck_ref, sk_ref, qn_ref, kvn_ref, wq_ref, wuk_ref, wuv_ref,
               y_ref, lat_ref, kr_ref, kcat_ref, qs_ref, s_ref, m_ref, l_ref, acc_ref):
    tq = CHUNK
    qi = pl.program_id(1)

    @pl.when(qi == 0)
    def _():
        c_kv, k_rope = _mla_kv_prep(mkv_ref[...], kvn_ref[...], ck_ref[...], sk_ref[...])
        lat_ref[...] = c_kv
        kr_ref[...] = k_rope[:, :MLA_ROPE]
        kcat_ref[:, :LANES] = c_kv.astype(BF16)
        kcat_ref[:, LANES:] = k_rope.astype(BF16)

    _mla_q_prep(mq_ref[...].astype(F32), qn_ref[...], wq_ref, wuk_ref, cq_ref[...], sq_ref[...], qs_ref, tq)

    def score_fn(rows, tile0, ntiles, near):
        s = _dot_nt(qs_ref[rows, :], kcat_ref[_key_rows(tile0, ntiles), :])
        if near:
            n = rows.stop - rows.start
            ri = lax.broadcasted_iota(jnp.int32, (n, ntiles * CHUNK), 0) % tq
            ci = lax.broadcasted_iota(jnp.int32, (n, ntiles * CHUNK), 1) - (ntiles - 1) * CHUNK
            s = jnp.where(ri >= ci, s, NEG_INF)
        return s

    def value_fn(tile0, ntiles):
        return kcat_ref[_key_rows(tile0, ntiles), :LANES]

    halves = (slice(0, 2 * tq), slice(2 * tq, 4 * tq))
    o = _two_pass_attention(qi, score_fn, value_fn, s_ref, m_ref, l_ref, acc_ref, halves)
    y_ref[...] = _mla_finish(o, wuv_ref, tq)


def _mla_prompt(mq, mkv, cos, sin, w):
    B, L, _ = mq.shape
    tq = CHUNK
    return pl.pallas_call(
        _mlap_body,
        grid=(B, L // tq),
        in_specs=[pl.BlockSpec((None, tq, 256), lambda b, i: (b, i, 0)),
                  pl.BlockSpec((None, L, 256), lambda b, i: (b, 0, 0)),
                  pl.BlockSpec((tq, LANES), lambda b, i: (i, 0)), pl.BlockSpec((tq, LANES), lambda b, i: (i, 0)),
                  _full((L, LANES)), _full((L, LANES)),
                  _full((1, MLA_Q_RANK)), _full((1, MLA_KV_RANK)),
                  _full((MLA_Q_RANK, MLA_HEADS * 256)), _full((MLA_HEADS, LANES, LANES)),
                  _full((MLA_HEADS, LANES, LANES))],
        out_specs=[pl.BlockSpec((None, tq, BRANCH_W), lambda b, i: (b, i, 0)),
                   pl.BlockSpec((None, L, MLA_KV_RANK), lambda b, i: (b, 0, 0)),
                   pl.BlockSpec((None, L, MLA_ROPE), lambda b, i: (b, 0, 0))],
        out_shape=[jax.ShapeDtypeStruct((B, L, BRANCH_W), F32), jax.ShapeDtypeStruct((B, L, MLA_KV_RANK), F32),
                   jax.ShapeDtypeStruct((B, L, MLA_ROPE), F32)],
        scratch_shapes=[pltpu.VMEM((L, 2 * LANES), BF16), pltpu.VMEM((MLA_HEADS * tq, 2 * LANES), BF16),
                        pltpu.VMEM((L // CHUNK, MLA_HEADS * tq, CHUNK), F32),
                        pltpu.VMEM((MLA_HEADS * tq, CHUNK), F32), pltpu.VMEM((MLA_HEADS * tq, CHUNK), F32),
                        pltpu.VMEM((MLA_HEADS * tq, LANES), F32)],
        compiler_params=pltpu.CompilerParams(dimension_semantics=("parallel", "arbitrary"),
                                             vmem_limit_bytes=VMEM_LIMIT),
        name="mla_prompt",
    )(mq, mkv, cos, sin, cos, sin, w["mla_q_norm"], w["mla_kv_norm"], w["mla_wq"], w["mla_wuk"], w["mla_wuv"])


SROWS = 8


def _diffs_body(pt_ref, tb_ref, sc_ref, q_ref, kn_ref, vn_ref, sub_ref, *rest, npg, past_len):
    kt_refs = rest[:npg]
    v_refs = rest[npg:2 * npg]
    y_ref, qs_ref, m_ref, l_ref, acc_ref = rest[2 * npg:]
    j = pl.program_id(1)
    nj = pl.num_programs(1)
    width = npg * PAGE

    @pl.when(j == 0)
    def _():
        for h in range(DIFF_KVH):
            _diff_stack_q(q_ref[:, h * 256:(h + 1) * 256].astype(F32), qs_ref.at[h], SROWS)
        _softmax_init(m_ref, l_ref, acc_ref)

    def visit(h, bias):
        kt = jnp.concatenate([kt_refs[i][h * LANES:(h + 1) * LANES, :].astype(BF16) for i in range(npg)], axis=1)
        s = _dot(qs_ref[h], kt)
        if bias is not None:
            s = s + bias
        v = jnp.concatenate([v_refs[i][pl.ds(h, PAGE, stride=DIFF_KVH), :].astype(BF16) for i in range(npg)], axis=0)
        _softmax_step(s, v, m_ref.at[h], l_ref.at[h], acc_ref.at[h])

    @pl.when(j < nj - 1)
    def _():
        for h in range(DIFF_KVH):
            visit(h, None)

    @pl.when(j == nj - 1)
    def _():
        t_row = lax.broadcasted_iota(jnp.int32, (2 * SROWS, width), 0) % SROWS
        ci = lax.broadcasted_iota(jnp.int32, (2 * SROWS, width), 1)
        dist = past_len + t_row - ((nj - 1) * width + ci)
        t_new = lax.broadcasted_iota(jnp.int32, (2 * SROWS, PAGE), 0) % SROWS
        c_new = lax.broadcasted_iota(jnp.int32, (2 * SROWS, PAGE), 1)
        for h in range(DIFF_KVH):
            hs = slice(h * LANES, (h + 1) * LANES)
            visit(h, jnp.concatenate([_rel_bias_minus_far(dist, tb_ref, h * 2 + g) for g in range(2)], axis=0))
            kn = _pad_rows(kn_ref[:, hs], PAGE).astype(BF16)
            vn = _pad_rows(vn_ref[:, hs], PAGE).astype(BF16)
            bias_n = jnp.concatenate(
                [jnp.where(t_new >= c_new, _rel_bias_minus_far(jnp.maximum(t_new - c_new, 0), tb_ref, h * 2 + g), NEG_INF)
                 for g in range(2)], axis=0)
            _softmax_step(_dot_nt(qs_ref[h], kn) + bias_n, vn, m_ref.at[h], l_ref.at[h], acc_ref.at[h])
            y_ref[:, h * 256:(h + 1) * 256] = _diff_finish(acc_ref[h] / l_ref[h], sc_ref, sub_ref[...], SROWS)


def _pages_per_step(n_pages, want):
    return math.gcd(n_pages, want)


def _diff_sample(pt, tb, sc, dq, dk, dv, subln, cache_kt, cache_v, layer):
    B = dq.shape[0]
    n_pages = pt.shape[1]
    npg = _pages_per_step(n_pages, 16)
    R = 4 * SROWS
    tok = lambda width: pl.BlockSpec((None, SROWS, width), lambda b, j, pt: (b, 0, 0))
    page = lambda i: pl.BlockSpec((None, None, 2 * PAGE, LANES), lambda b, j, pt: (pt[b, j * npg + i], layer, 0, 0))
    grid_spec = pltpu.PrefetchScalarGridSpec(
        num_scalar_prefetch=1,
        grid=(B, n_pages // npg),
        in_specs=[_SMEM, _SMEM, tok(512), tok(256), tok(256), pl.BlockSpec((1, DIFF_DV), lambda b, j, pt: (0, 0))]
        + [page(i) for i in range(npg)] + [page(i) for i in range(npg)],
        out_specs=tok(BRANCH_W),
        scratch_shapes=[pltpu.VMEM((DIFF_KVH, R, LANES), BF16), pltpu.VMEM((DIFF_KVH, R, 1), F32),
                        pltpu.VMEM((DIFF_KVH, R, 1), F32), pltpu.VMEM((DIFF_KVH, R, LANES), F32)],
    )
    return pl.pallas_call(
        functools.partial(_diffs_body, npg=npg, past_len=n_pages * PAGE),
        grid_spec=grid_spec,
        out_shape=jax.ShapeDtypeStruct((B, SROWS, BRANCH_W), F32),
        compiler_params=pltpu.CompilerParams(dimension_semantics=("parallel", "arbitrary"),
                                             vmem_limit_bytes=VMEM_LIMIT),
        name="diff_sample",
    )(pt, tb, sc, dq, dk, dv, subln, *([cache_kt] * npg), *([cache_v] * npg))


def _mlas_body(pt_ref, mq_ref, mkv_ref, cos_ref, sin_ref, qn_ref, kvn_ref, wq_ref, wuk_ref, wuv_ref, *rest, npg):
    c_refs = rest[:npg]
    rt_refs = rest[npg:2 * npg]
    y_ref, lat_ref, kr_ref, qs_ref, kn_ref, m_ref, l_ref, acc_ref = rest[2 * npg:]
    j = pl.program_id(1)
    nj = pl.num_programs(1)
    R = MLA_HEADS * SROWS

    @pl.when(j == 0)
    def _():
        c_kv, k_rope = _mla_kv_prep(mkv_ref[...], kvn_ref[...], cos_ref[...], sin_ref[...])
        lat_ref[...] = c_kv
        kr_ref[...] = k_rope[:, :MLA_ROPE]
        kn_ref[...] = jnp.zeros(kn_ref.shape, BF16)
        kn_ref[0:SROWS, :LANES] = c_kv.astype(BF16)
        kn_ref[0:SROWS, LANES:] = k_rope.astype(BF16)
        _mla_q_prep(mq_ref[...].astype(F32), qn_ref[...], wq_ref, wuk_ref, cos_ref[...], sin_ref[...], qs_ref, SROWS)
        _softmax_init(m_ref, l_ref, acc_ref)

    lat = jnp.concatenate([c_refs[i][...].astype(BF16) for i in range(npg)], axis=0)
    krt = jnp.concatenate([rt_refs[i][...].astype(BF16) for i in range(npg)], axis=1)
    s = _dot_nt(qs_ref[:, :LANES], lat) + _dot(qs_ref[:, LANES:LANES + MLA_ROPE], krt)
    _softmax_step(s, lat, m_ref, l_ref, acc_ref)

    @pl.when(j == nj - 1)
    def _():
        kc = kn_ref[...]
        t_new = lax.broadcasted_iota(jnp.int32, (R, PAGE), 0) % SROWS
        c_new = lax.broadcasted_iota(jnp.int32, (R, PAGE), 1)
        sn = jnp.where(t_new >= c_new, _dot_nt(qs_ref[...], kc), NEG_INF)
        _softmax_step(sn, kc[:, :LANES], m_ref, l_ref, acc_ref)
        y_ref[...] = _mla_finish(acc_ref[...] / l_ref[...], wuv_ref, SROWS)


def _mla_sample(pt, mq, mkv, cos, sin, w, cache_c, cache_rt, layer):
    B = mq.shape[0]
    n_pages = pt.shape[1]
    npg = _pages_per_step(n_pages, 32)
    R = MLA_HEADS * SROWS
    tok = lambda width: pl.BlockSpec((None, SROWS, width), lambda b, j, pt: (b, 0, 0))
    full = lambda *shape: pl.BlockSpec(shape, lambda b, j, pt: (0,) * len(shape))
    page = lambda i, rows: pl.BlockSpec((None, None, rows, LANES), lambda b, j, pt: (pt[b, j * npg + i], layer, 0, 0))
    grid_spec = pltpu.PrefetchScalarGridSpec(
        num_scalar_prefetch=1,
        grid=(B, n_pages // npg),
        in_specs=[tok(256), tok(256), full(SROWS, LANES), full(SROWS, LANES), full(1, MLA_Q_RANK), full(1, MLA_KV_RANK),
                  full(MLA_Q_RANK, MLA_HEADS * 256), full(MLA_HEADS, LANES, LANES), full(MLA_HEADS, LANES, LANES)]
        + [page(i, PAGE) for i in range(npg)] + [page(i, MLA_ROPE) for i in range(npg)],
        out_specs=[tok(BRANCH_W), tok(MLA_KV_RANK), tok(MLA_ROPE)],
        scratch_shapes=[pltpu.VMEM((R, 2 * LANES), BF16), pltpu.VMEM((PAGE, 2 * LANES), BF16),
                        pltpu.VMEM((R, 1), F32), pltpu.VMEM((R, 1), F32), pltpu.VMEM((R, LANES), F32)],
    )
    return pl.pallas_call(
        functools.partial(_mlas_body, npg=npg),
        grid_spec=grid_spec,
        out_shape=[jax.ShapeDtypeStruct((B, SROWS, BRANCH_W), F32), jax.ShapeDtypeStruct((B, SROWS, MLA_KV_RANK), F32),
                   jax.ShapeDtypeStruct((B, SROWS, MLA_ROPE), F32)],
        compiler_params=pltpu.CompilerParams(dimension_semantics=("parallel", "arbitrary"),
                                             vmem_limit_bytes=VMEM_LIMIT),
        name="mla_sample",
    )(pt, mq, mkv, cos, sin, w["mla_q_norm"], w["mla_kv_norm"], w["mla_wq"], w["mla_wuk"], w["mla_wuv"],
      *([cache_c] * npg), *([cache_rt] * npg))


def _merge_body(x_ref, y0_ref, y1_ref, y2_ref, y3_ref, gt_ref, wb_ref, wo_ref, o_ref):
    mix = None
    for i, y_ref in enumerate((y0_ref, y1_ref, y2_ref, y3_ref)):
        proj = _dot(y_ref[...].astype(BF16), wb_ref[i])
        gate = gt_ref[:, i * D_MODEL:(i + 1) * D_MODEL].astype(F32)
        term = proj / (1.0 + jnp.exp(-gate))
        mix = term if mix is None else mix + term
    o_ref[...] = x_ref[...] + _dot(mix.astype(BF16), wo_ref[...])


def _merge(x, ys, gates, wb, wo):
    T = x.shape[0]
    tm = min(512, T)
    row = lambda width: pl.BlockSpec((tm, width), lambda i: (i, 0))
    return pl.pallas_call(
        _merge_body,
        grid=(T // tm,),
        in_specs=[row(D_MODEL)] + [row(BRANCH_W)] * 4 + [row(N_BRANCH * D_MODEL),
                                                       _resident((N_BRANCH, BRANCH_W, D_MODEL)),
                                                       _resident((D_MODEL, D_MODEL))],
        out_specs=row(D_MODEL),
        out_shape=jax.ShapeDtypeStruct((T, D_MODEL), F32),
        compiler_params=pltpu.CompilerParams(dimension_semantics=("parallel",), vmem_limit_bytes=VMEM_LIMIT),
        name="merge",
    )(x, *ys, gates, wb, wo)


def _rope_tables(pos, half):
    inv_freq = ROPE_BASE ** (-jnp.arange(half, dtype=F32) / half)
    ang = pos.astype(F32)[:, None] * inv_freq[None, :]
    cos, sin = jnp.cos(ang), jnp.sin(ang)
    return jnp.concatenate([cos, cos], axis=-1), jnp.concatenate([-sin, sin], axis=-1)


def _layer_weights(l, p):
    w = {}
    f = lambda a: a.astype(BF16)
    for n in ("ffn1_w1", "ffn1_w3", "ffn1_w2", "ffn2_w1", "ffn2_w3", "ffn2_w2", "w_out"):
        w[n] = f(p[n][l])
    w["w_branch"] = f(p["w_branch"][l])
    for n in ("norm_ffn1", "norm_mix", "norm_ffn2", "ssm_conv_b", "ssm_norm", "ret_norm", "diff_subln",
              "mla_q_norm", "mla_kv_norm"):
        w[n] = p[n][l][None, :]
    offs = np.cumsum((0,) + IN_SIZES)
    piece = lambda i: p["w_in"][l][:, offs[i]:offs[i + 1]]
    padc = lambda a, width: jnp.pad(a, ((0, 0), (0, width - a.shape[1])))
    cols = [piece(0), piece(1), padc(piece(2), LANES), piece(3), piece(4), piece(5), piece(6), piece(7), piece(8),
            piece(9), piece(10), padc(piece(11), 2 * LANES), piece(12)]
    w["w_in"] = f(jnp.concatenate(cols, axis=1))
    w["conv_w"] = p["ssm_conv_w"][l]
    w["conv_b"] = w["ssm_conv_b"]
    w["dt_bias"] = padc(p["ssm_dt_bias"][l][None, :], LANES)
    w["dt_biasT"] = p["ssm_dt_bias"][l][:, None]
    w["a_log"] = padc(p["ssm_a_log"][l][None, :], LANES)
    w["a_logT"] = p["ssm_a_log"][l][:, None]
    w["d_exp"] = jnp.repeat(p["ssm_d"][l], SSM_HEAD_DIM)[None, :]
    expand = np.zeros((LANES, BRANCH_W), np.float32)
    for h in range(SSM_HEADS):
        expand[h, h * SSM_HEAD_DIM:(h + 1) * SSM_HEAD_DIM] = 1.0
    w["expand"] = jnp.asarray(expand)
    wq = p["mla_w_q_up"][l]
    zq = jnp.zeros((MLA_Q_RANK, MLA_HEADS, LANES - MLA_NOPE), F32)
    zr = jnp.zeros((MLA_Q_RANK, MLA_HEADS, LANES - MLA_ROPE), F32)
    w["mla_wq"] = f(jnp.concatenate([wq[..., :MLA_NOPE], zq, wq[..., MLA_NOPE:], zr], axis=-1)
                    .reshape(MLA_Q_RANK, MLA_HEADS * 256))
    wuk = jnp.transpose(p["mla_w_uk"][l], (1, 2, 0))
    w["mla_wuk"] = f(jnp.pad(wuk, ((0, 0), (0, LANES - MLA_NOPE), (0, 0))))
    w["mla_wuv"] = f(jnp.transpose(p["mla_w_uv"][l], (1, 0, 2)))
    lam_init = 0.8 - 0.6 * math.exp(-0.3 * l)
    lw = p["diff_lambda"][l].astype(F32)
    lam = jnp.exp(jnp.sum(lw[0] * lw[1])) - jnp.exp(jnp.sum(lw[2] * lw[3])) + lam_init
    w["diff_sc"] = jnp.stack([lam, jnp.asarray(1.0 - lam_init, F32)]).astype(F32)
    return w


def _ssm_state_to_pairs(s):
    B = s.shape[0]
    return s.reshape(B, 4, 2, SSM_STATE, SSM_HEAD_DIM).transpose(0, 1, 3, 2, 4).reshape(B, 4, SSM_STATE, 2 * SSM_HEAD_DIM)


def _ssm_state_from_pairs(s):
    B = s.shape[0]
    return s.reshape(B, 4, SSM_STATE, 2, SSM_HEAD_DIM).transpose(0, 1, 3, 2, 4).reshape(B, SSM_HEADS, SSM_STATE, SSM_HEAD_DIM)


def _trunk(x, w_layers, rel_bias, final_norm, rows, valid, pos, conv0, ssm0, ret0, paged):
    B, L, _ = x.shape
    T = B * L
    depth = len(w_layers)
    xt = x.reshape(T, D_MODEL)
    ret_cos, ret_sin = _rope_tables(pos, RET_QK // 2)
    ret_cos, ret_sin = jnp.tile(ret_cos, (1, 4)), jnp.tile(ret_sin, (1, 4))
    mc, ms = _rope_tables(pos, MLA_ROPE // 2)
    mla_cos = jnp.pad(mc, ((0, 0), (0, LANES - MLA_ROPE)))
    mla_sin = jnp.pad(ms, ((0, 0), (0, LANES - MLA_ROPE)))
    nc = L // rows
    outs = []
    for l in range(depth):
        w = w_layers[l]
        xt = _ffn(xt, w["norm_ffn1"], w["ffn1_w1"], w["ffn1_w3"], w["ffn1_w2"])
        pr = _inproj(xt, w["norm_mix"], w["w_in"])
        r3 = lambda a: a.reshape(B, L, a.shape[-1])
        dt = r3(pr["dt"])
        dtT = jnp.transpose(dt[:, :, :SSM_HEADS].reshape(B, nc, rows, SSM_HEADS), (0, 3, 1, 2))
        dtT = jnp.pad(dtT, ((0, 0), (0, 0), (0, 0), (0, CHUNK - rows))).reshape(B, SSM_HEADS, nc * CHUNK)
        dtp = dt if rows == CHUNK else jnp.pad(dt, ((0, 0), (0, CHUNK - rows), (0, 0)))
        y_ssm, conv_new, ssm_new = _ssd(r3(pr["xbc"]), r3(pr["z"]), dtp, dtT, conv0[l], ssm0[l], w, rows, valid)
        y_ret, ret_new = _ret(r3(pr["rq"]), r3(pr["rk"]), r3(pr["rv"]), r3(pr["rg"]), ret_cos, ret_sin, ret0[l],
                              w["ret_norm"], rows, valid)
        dk, dv = r3(pr["dk"]), r3(pr["dv"])
        if paged is None:
            y_diff = _diff_prompt(rel_bias, w["diff_sc"], r3(pr["dq"]), dk, dv, w["diff_subln"])
            y_mla, lat, kr = _mla_prompt(r3(pr["mq"]), r3(pr["mkv"]), mla_cos, mla_sin, w)
        else:
            pt, ck, cv, cc, cr = paged
            y_diff = _diff_sample(pt, rel_bias, w["diff_sc"], r3(pr["dq"]), dk, dv, w["diff_subln"], ck, cv, l)
            y_mla, lat, kr = _mla_sample(pt, r3(pr["mq"]), r3(pr["mkv"]), mla_cos, mla_sin, w, cc, cr, l)
        ys = [a.reshape(T, BRANCH_W) for a in (y_ssm, y_ret, y_diff, y_mla)]
        xt = _merge(xt, ys, pr["gates"], w["w_branch"], w["w_out"])
        fg = final_norm if l == depth - 1 else None
        xt = _ffn(xt, w["norm_ffn2"], w["ffn2_w1"], w["ffn2_w3"], w["ffn2_w2"], fg)
        outs.append((dk, dv, lat, kr, conv_new, ssm_new, ret_new))
    return xt.reshape(B, L, D_MODEL), outs


def kernel(x_prompt, x_sample, cache_diff_k, cache_diff_v, cache_mla_latent, cache_mla_krope, state_ssm_conv, state_ssm, state_retention, page_table, norm_ffn1, ffn1_w1, ffn1_w3, ffn1_w2, norm_mix, w_in, ssm_conv_w, ssm_conv_b, ssm_dt_bias, ssm_a_log, ssm_d, ssm_norm, ret_norm, diff_lambda, diff_subln, mla_q_norm, mla_w_q_up, mla_kv_norm, mla_w_uk, mla_w_uv, rel_bias, w_branch, w_out, norm_ffn2, ffn2_w1, ffn2_w3, ffn2_w2, final_norm):
    p = dict(norm_ffn1=norm_ffn1, ffn1_w1=ffn1_w1, ffn1_w3=ffn1_w3, ffn1_w2=ffn1_w2, norm_mix=norm_mix, w_in=w_in,
             ssm_conv_w=ssm_conv_w, ssm_conv_b=ssm_conv_b, ssm_dt_bias=ssm_dt_bias, ssm_a_log=ssm_a_log, ssm_d=ssm_d,
             ssm_norm=ssm_norm, ret_norm=ret_norm, diff_lambda=diff_lambda, diff_subln=diff_subln,
             mla_q_norm=mla_q_norm, mla_w_q_up=mla_w_q_up, mla_kv_norm=mla_kv_norm, mla_w_uk=mla_w_uk,
             mla_w_uv=mla_w_uv, w_branch=w_branch, w_out=w_out, norm_ffn2=norm_ffn2, ffn2_w1=ffn2_w1,
             ffn2_w3=ffn2_w3, ffn2_w2=ffn2_w2)
    depth = w_in.shape[0]
    w_layers = [_layer_weights(l, p) for l in range(depth)]
    fnorm = final_norm[None, :]
    rel = rel_bias.astype(F32)

    Bp, Lp, _ = x_prompt.shape
    zc = jnp.zeros((Bp, CONV_W - 1, SSM_CONV_DIM), F32)
    zs = jnp.zeros((Bp, 4, LANES, LANES), F32)
    zr = jnp.zeros((Bp, 2 * LANES, LANES), F32)
    y_prompt, rows_p = _trunk(x_prompt, w_layers, rel, fnorm, CHUNK, CHUNK, jnp.arange(Lp, dtype=jnp.int32),
                              [zc] * depth, [zs] * depth, [zr] * depth, None)

    Bs, Ls, _ = x_sample.shape
    n_pages = page_table.shape[1]
    past_len = n_pages * PAGE
    xs = jnp.pad(x_sample, ((0, 0), (0, SROWS - Ls), (0, 0)))
    pos_s = past_len + jnp.arange(SROWS, dtype=jnp.int32)
    n_pool = cache_diff_k.shape[0]
    paged = (page_table.astype(jnp.int32),
             jnp.transpose(cache_diff_k, (0, 1, 3, 4, 5, 2)).reshape(n_pool, depth, 2 * PAGE, LANES),
             cache_diff_v.reshape(n_pool, depth, 2 * PAGE, LANES),
             cache_mla_latent, jnp.transpose(cache_mla_krope, (0, 1, 3, 2)))
    conv_s = [state_ssm_conv[:, l] for l in range(depth)]
    ssm_s = [_ssm_state_to_pairs(state_ssm[:, l]) for l in range(depth)]
    ret_s = [state_retention[:, l].reshape(Bs, 2 * LANES, LANES) for l in range(depth)]
    y_sample, rows_s = _trunk(xs, w_layers, rel, fnorm, SROWS, Ls, pos_s, conv_s, ssm_s, ret_s, paged)

    def collect(rows, B, L, keep):
        st = lambda i: jnp.stack([r[i] for r in rows], axis=1)
        dk = st(0)[:, :, :keep].reshape(B, depth, keep, DIFF_KVH, 2, DIFF_DH)
        dv = st(1)[:, :, :keep].reshape(B, depth, keep, DIFF_KVH, DIFF_DV)
        lat = st(2)[:, :, :keep]
        kr = st(3)[:, :, :keep]
        conv = st(4)
        ssm = jnp.stack([_ssm_state_from_pairs(r[5]) for r in rows], axis=1)
        ret = st(6).reshape(B, depth, RET_HEADS, RET_QK, RET_V)
        return dk, dv, lat, kr, conv, ssm, ret

    return (y_prompt, y_sample[:, :Ls]) + collect(rows_p, Bp, Lp, Lp) + collect(rows_s, Bs, SROWS, Ls)
```

```python
import functools
import math

import numpy as np
import jax
import jax.numpy as jnp
from jax import lax
from jax.experimental import pallas as pl
from jax.experimental.pallas import tpu as pltpu

F32 = jnp.float32
BF16 = jnp.bfloat16
NEG_INF = float("-inf")

D_MODEL = 1024
D_FF = 2816
BRANCH_W = 512
N_BRANCH = 4
SSM_HEADS = 8
SSM_HEAD_DIM = 64
SSM_STATE = 128
CONV_W = 4
SSM_CONV_DIM = 1024
RET_HEADS = 4
RET_QK = 64
RET_V = 128
DIFF_KVH = 2
DIFF_DH = 64
DIFF_DV = 128
MLA_HEADS = 4
MLA_Q_RANK = 256
MLA_KV_RANK = 128
MLA_NOPE = 64
MLA_ROPE = 32
PAGE = 128
REL_BUCKETS = 32
REL_MAX_EXACT = 16
REL_MAX_DIST = 128
ROPE_BASE = 10000.0
NORM_EPS = 1e-6
IN_SIZES = (512, 1024, 8, 256, 256, 512, 512, 512, 256, 256, 256, 160, 4096)

LANES = 128
CHUNK = 128
VMEM_LIMIT = 56 * 1024 * 1024


def _rel_lower_bounds():
    lb = list(range(REL_MAX_EXACT))
    d = np.arange(REL_MAX_EXACT, 4 * REL_MAX_DIST).astype(np.float32)
    large = REL_MAX_EXACT + (np.log(d / np.float32(REL_MAX_EXACT)) / np.float32(math.log(REL_MAX_DIST / REL_MAX_EXACT))
                             * np.float32(REL_BUCKETS - REL_MAX_EXACT)).astype(np.int32)
    large = np.minimum(large, REL_BUCKETS - 1)
    for bk in range(REL_MAX_EXACT, REL_BUCKETS):
        lb.append(int(REL_MAX_EXACT + np.argmax(large >= bk)))
    return tuple(lb)


REL_LB = _rel_lower_bounds()
REL_FAR = REL_LB[-1]
RET_LOG_GAMMA = tuple(math.log1p(-(2.0 ** (-5.0 - h))) for h in range(RET_HEADS))


def _dot(a, b, precision=None):
    return jnp.dot(a, b, preferred_element_type=F32, precision=precision)


def _dot_nt(a, b):
    return lax.dot_general(a, b, (((1,), (1,)), ((), ())), preferred_element_type=F32)


def _dot_tn(a, b):
    return lax.dot_general(a, b, (((0,), (0,)), ((), ())), preferred_element_type=F32)


def _rms(x):
    return x * lax.rsqrt(jnp.mean(x * x, axis=-1, keepdims=True) + NORM_EPS)


def _silu(x):
    return x / (1.0 + jnp.exp(-x))


def _softplus(x):
    return jnp.maximum(x, 0.0) + jnp.log1p(jnp.exp(-jnp.abs(x)))


def _pad_rows(x, rows):
    if x.shape[0] == rows:
        return x
    return jnp.concatenate([x, jnp.zeros((rows - x.shape[0],) + x.shape[1:], x.dtype)], axis=0)


def _rotate(x, cosf, sins, half):
    outs = []
    for s in range(x.shape[-1] // LANES):
        xs = x[:, s * LANES:(s + 1) * LANES]
        lane = lax.broadcasted_iota(jnp.int32, xs.shape, 1)
        first = (lane % (2 * half)) < half
        partner = jnp.where(first, pltpu.roll(xs, LANES - half, axis=1), pltpu.roll(xs, half, axis=1))
        outs.append(partner)
    partner = outs[0] if len(outs) == 1 else jnp.concatenate(outs, axis=-1)
    return x * cosf + partner * sins


def _full(shape):
    return pl.BlockSpec(shape, lambda *_: (0,) * len(shape))


def _resident(shape):
    return pl.BlockSpec(shape, lambda *_: (0,) * len(shape), pipeline_mode=pl.Buffered(1))


_SMEM = pl.BlockSpec(memory_space=pltpu.SMEM)


def _ffn_body(*refs, fc, final):
    if final:
        x_ref, g_ref, w1_ref, w3_ref, w2_ref, fg_ref, o_ref = refs
    else:
        x_ref, g_ref, w1_ref, w3_ref, w2_ref, o_ref = refs
    x = x_ref[...]
    xn = (_rms(x) * g_ref[...]).astype(BF16)
    acc = x
    for c in range(D_FF // fc):
        sl = slice(c * fc, (c + 1) * fc)
        h1 = _dot(xn, w1_ref[:, sl])
        h3 = _dot(xn, w3_ref[:, sl])
        a = (_silu(h1) * h3).astype(BF16)
        acc = acc + 0.5 * _dot(a, w2_ref[sl, :])
    if final:
        acc = _rms(acc) * fg_ref[...]
    o_ref[...] = acc


def _ffn(x, g, w1, w3, w2, final_g=None):
    T = x.shape[0]
    tm = min(512, T)
    final = final_g is not None
    in_specs = [pl.BlockSpec((tm, D_MODEL), lambda i: (i, 0)), _full((1, D_MODEL)),
                _resident((D_MODEL, D_FF)), _resident((D_MODEL, D_FF)), _resident((D_FF, D_MODEL))]
    args = [x, g, w1, w3, w2]
    if final:
        in_specs.append(_full((1, D_MODEL)))
        args.append(final_g)
    return pl.pallas_call(
        functools.partial(_ffn_body, fc=D_FF // 2, final=final),
        grid=(T // tm,),
        in_specs=in_specs,
        out_specs=pl.BlockSpec((tm, D_MODEL), lambda i: (i, 0)),
        out_shape=jax.ShapeDtypeStruct((T, D_MODEL), F32),
        compiler_params=pltpu.CompilerParams(dimension_semantics=("parallel",), vmem_limit_bytes=VMEM_LIMIT),
        name="ffn",
    )(*args)


PROJ_OUT = (("z", 512, BF16), ("xbc", 1024, F32), ("dt", 128, F32), ("rq", 256, BF16), ("rk", 256, BF16),
            ("rv", 512, BF16), ("rg", 512, BF16), ("dq", 512, BF16), ("dk", 256, F32), ("dv", 256, F32),
            ("mq", 256, BF16), ("mkv", 256, F32), ("gates", 4096, BF16))
PROJ_W = sum(w for _, w, _ in PROJ_OUT)


def _inproj_body(x_ref, g_ref, w_ref, *o_refs):
    xn = (_rms(x_ref[...]) * g_ref[...]).astype(BF16)
    off = 0
    for (_, w, dt), o_ref in zip(PROJ_OUT, o_refs):
        o_ref[...] = _dot(xn, w_ref[:, off:off + w]).astype(dt)
        off += w


def _inproj(x, g, w):
    T = x.shape[0]
    tm = min(512, T)
    outs = pl.pallas_call(
        _inproj_body,
        grid=(T // tm,),
        in_specs=[pl.BlockSpec((tm, D_MODEL), lambda i: (i, 0)), _full((1, D_MODEL)), _resident((D_MODEL, PROJ_W))],
        out_specs=[pl.BlockSpec((tm, w_), lambda i: (i, 0)) for _, w_, _ in PROJ_OUT],
        out_shape=[jax.ShapeDtypeStruct((T, w_), dt) for _, w_, dt in PROJ_OUT],
        compiler_params=pltpu.CompilerParams(dimension_semantics=("parallel",), vmem_limit_bytes=VMEM_LIMIT),
        name="inproj",
    )(x, g, w)
    return {n: o for (n, _, _), o in zip(PROJ_OUT, outs)}


SEQ_PER_STEP = 2


def _per_sequence(body, init, n_seq_in, n_shared, *refs, **static):
    seq_in, shared, seq_rest = refs[:n_seq_in], refs[n_seq_in:n_seq_in + n_shared], refs[n_seq_in + n_shared:]
    views = [[r.at[i] for r in seq_in] + list(shared) + [r.at[i] for r in seq_rest] for i in range(SEQ_PER_STEP)]

    @pl.when(pl.program_id(1) == 0)
    def _():
        for v in views:
            init(*v)

    for v in views:
        body(*v, **static)


def _ssd_init(xbc_ref, z_ref, dt_ref, dtT_ref, cb0_ref, s0_ref, cw_ref, cbias_ref, dtb_ref, dtbT_ref,
              alog_ref, alogT_ref, dexp_ref, nw_ref, e_ref, y_ref, cnew_ref, snew_ref, xp_ref, s_ref):
    xp_ref[0:8, :] = jnp.zeros((8, SSM_CONV_DIM), F32)
    xp_ref[5:8, :] = cb0_ref[...]
    s_ref[...] = s0_ref[...]


def _ssd_chunk(xbc_ref, z_ref, dt_ref, dtT_ref, cb0_ref, s0_ref, cw_ref, cbias_ref, dtb_ref, dtbT_ref,
               alog_ref, alogT_ref, dexp_ref, nw_ref, e_ref,
               y_ref, cnew_ref, snew_ref, xp_ref, s_ref, *, rows, valid):
    Q = CHUNK
    xp_ref[8:8 + Q, :] = _pad_rows(xbc_ref[...], Q)
    conv = cbias_ref[...] + xp_ref[5:5 + Q, :] * cw_ref[0:1, :]
    for w in range(1, CONV_W):
        conv = conv + xp_ref[5 + w:5 + w + Q, :] * cw_ref[w:w + 1, :]
    u = _silu(conv)
    last3 = xp_ref[8 + valid - 3:8 + valid, :]
    xp_ref[5:8, :] = last3
    cnew_ref[...] = last3

    row = lax.broadcasted_iota(jnp.int32, (Q, Q), 0)
    col = lax.broadcasted_iota(jnp.int32, (Q, Q), 1)
    causal = row >= col
    hi = lax.Precision.HIGHEST

    dtv = _softplus(dt_ref[...] + dtb_ref[...])
    dtv = jnp.where(lax.broadcasted_iota(jnp.int32, dtv.shape, 0) < valid, dtv, 0.0)
    la = -jnp.exp(alog_ref[...]) * dtv
    cum = _dot(causal.astype(F32), la, hi)
    dtT = _softplus(dtT_ref[...] + dtbT_ref[...])
    dtT = jnp.where(lax.broadcasted_iota(jnp.int32, dtT.shape, 1) < valid, dtT, 0.0)
    laT = -jnp.exp(alogT_ref[...]) * dtT
    cumT = _dot(laT, (row <= col).astype(F32), hi)
    cum_exp = _dot(cum, e_ref[...], hi)
    dt_exp = _dot(dtv, e_ref[...], hi)
    cl_exp = cum_exp[Q - 1:Q, :]
    ecum = jnp.exp(cum_exp)
    tail = jnp.exp(cl_exp - cum_exp)
    sdecay = jnp.exp(cl_exp)

    xs = u[:, :BRANCH_W]
    v = xs * dt_exp
    vb = v.astype(BF16)
    vt = (v * tail).astype(BF16)
    lane = lax.broadcasted_iota(jnp.int32, (Q, LANES), 1)
    ys = []
    for g in range(2):
        bg = u[:, 512 + g * 128:512 + (g + 1) * 128].astype(BF16)
        cg = u[:, 768 + g * 128:768 + (g + 1) * 128].astype(BF16)
        gmat = _dot_nt(cg, bg)
        for pp in range(2):
            p = g * 2 + pp
            sl = slice(p * LANES, (p + 1) * LANES)
            s_pair = s_ref[p]
            inter = _dot(cg, s_pair.astype(BF16))
            yh = []
            for hh in range(2):
                h = 2 * p + hh
                seg = cum[:, h:h + 1] - cumT[h:h + 1, :]
                dec = jnp.exp(jnp.where(causal, seg, NEG_INF))
                yh.append(_dot((gmat * dec).astype(BF16), vb[:, sl]))
            ys.append(jnp.where(lane < SSM_HEAD_DIM, yh[0], yh[1]) + inter * ecum[:, sl])
            s_ref[p] = s_pair * sdecay[:, sl] + _dot_tn(bg, vt[:, sl])
    y = jnp.concatenate(ys, axis=-1) + xs * dexp_ref[...]
    y = y * _silu(_pad_rows(z_ref[...].astype(F32), Q))
    half = BRANCH_W // 2
    y = jnp.concatenate([_rms(y[:, :half]), _rms(y[:, half:])], axis=-1) * nw_ref[...]
    y_ref[...] = y[:rows]
    snew_ref[...] = s_ref[...]


def _ssd(xbc, z, dt, dtT, cb0, s0, w, rows, valid):
    B, L, _ = xbc.shape
    nc = L // rows
    ns = SEQ_PER_STEP
    assert B % ns == 0
    seq = lambda width: pl.BlockSpec((ns, rows, width), lambda b, c: (b, c, 0))
    per_b = lambda *shape: pl.BlockSpec((ns,) + shape, lambda b, c: (b,) + (0,) * len(shape))
    return pl.pallas_call(
        functools.partial(_per_sequence, _ssd_chunk, _ssd_init, 6, 9, rows=rows, valid=valid),
        grid=(B // ns, nc),
        in_specs=[seq(SSM_CONV_DIM), seq(BRANCH_W), pl.BlockSpec((ns, CHUNK, LANES), lambda b, c: (b, c, 0)),
                  pl.BlockSpec((ns, SSM_HEADS, CHUNK), lambda b, c: (b, 0, c)),
                  per_b(CONV_W - 1, SSM_CONV_DIM), per_b(4, LANES, LANES),
                  _full((CONV_W, SSM_CONV_DIM)), _full((1, SSM_CONV_DIM)), _full((1, LANES)), _full((SSM_HEADS, 1)),
                  _full((1, LANES)), _full((SSM_HEADS, 1)), _full((1, BRANCH_W)), _full((1, BRANCH_W)),
                  _full((LANES, BRANCH_W))],
        out_specs=[seq(BRANCH_W), per_b(CONV_W - 1, SSM_CONV_DIM), per_b(4, LANES, LANES)],
        out_shape=[jax.ShapeDtypeStruct((B, L, BRANCH_W), F32),
                   jax.ShapeDtypeStruct((B, CONV_W - 1, SSM_CONV_DIM), F32),
                   jax.ShapeDtypeStruct((B, 4, LANES, LANES), F32)],
        scratch_shapes=[pltpu.VMEM((ns, 8 + CHUNK, SSM_CONV_DIM), F32), pltpu.VMEM((ns, 4, LANES, LANES), F32)],
        compiler_params=pltpu.CompilerParams(dimension_semantics=("parallel", "arbitrary")),
        name="ssd",
    )(xbc, z, dt, dtT, cb0, s0, w["conv_w"], w["conv_b"], w["dt_bias"], w["dt_biasT"], w["a_log"], w["a_logT"],
      w["d_exp"], w["ssm_norm"], w["expand"])


def _ret_init(q_ref, k_ref, v_ref, g_ref, s0_ref, cos_ref, sin_ref, nw_ref, y_ref, snew_ref, s_ref):
    s_ref[...] = s0_ref[...]


def _ret_chunk(q_ref, k_ref, v_ref, g_ref, s0_ref, cos_ref, sin_ref, nw_ref, y_ref, snew_ref, s_ref, *, rows, valid):
    Q = CHUNK
    cosf = _pad_rows(cos_ref[...], Q)
    sins = _pad_rows(sin_ref[...], Q)
    q = _rotate(_pad_rows(q_ref[...].astype(F32), Q), cosf, sins, RET_QK // 2)
    k = _rotate(_pad_rows(k_ref[...].astype(F32), Q), cosf, sins, RET_QK // 2) * (RET_QK ** -0.5)
    v = _pad_rows(v_ref[...].astype(F32), Q)
    gate = _pad_rows(g_ref[...].astype(F32), Q)
    ri = lax.broadcasted_iota(jnp.int32, (Q, 1), 0)
    ci = lax.broadcasted_iota(jnp.int32, (1, Q), 1)
    cnt_i = jnp.minimum(ri + 1, valid).astype(F32)
    cnt_j = jnp.minimum(ci + 1, valid).astype(F32)
    causal = ri >= ci
    lane = lax.broadcasted_iota(jnp.int32, (1, LANES), 1)
    ys = []
    for h in range(RET_HEADS):
        lg = RET_LOG_GAMMA[h]
        p, hh = divmod(h, 2)
        mine = (lane // RET_QK) == hh
        qm = jnp.where(mine, q[:, p * LANES:(p + 1) * LANES], 0.0).astype(BF16)
        km = jnp.where(mine, k[:, p * LANES:(p + 1) * LANES], 0.0).astype(BF16)
        dec = jnp.exp(jnp.where(causal, (cnt_i - cnt_j) * lg, NEG_INF))
        sc = (_dot_nt(qm, km) * dec).astype(BF16)
        vh = jnp.where(ri < valid, v[:, h * RET_V:(h + 1) * RET_V], 0.0)
        s_stack = s_ref[p * LANES:(p + 1) * LANES, :]
        y = _dot(sc, vh.astype(BF16)) + _dot(qm, s_stack.astype(BF16)) * jnp.exp(cnt_i * lg)
        tail = jnp.exp((valid - cnt_i) * lg)
        upd = _dot_tn(km, (vh * tail).astype(BF16))
        rs = slice(h * RET_QK, (h + 1) * RET_QK)
        s_ref[rs, :] = s_ref[rs, :] * math.exp(valid * lg) + upd[hh * RET_QK:(hh + 1) * RET_QK, :]
        mu = jnp.mean(y, axis=-1, keepdims=True)
        yc = y - mu
        var = jnp.mean(yc * yc, axis=-1, keepdims=True)
        ys.append(yc * lax.rsqrt(var + NORM_EPS))
    y = jnp.concatenate(ys, axis=-1) * nw_ref[...] * _silu(gate)
    y_ref[...] = y[:rows]
    snew_ref[...] = s_ref[...]


def _ret(q, k, v, g, cos, sin, s0, nw, rows, valid):
    B, L, _ = q.shape
    nc = L // rows
    ns = SEQ_PER_STEP
    assert B % ns == 0
    seq = lambda width: pl.BlockSpec((ns, rows, width), lambda b, c: (b, c, 0))
    tab = pl.BlockSpec((rows, 2 * LANES), lambda b, c: (c, 0))
    st = pl.BlockSpec((ns, 2 * LANES, LANES), lambda b, c: (b, 0, 0))
    return pl.pallas_call(
        functools.partial(_per_sequence, _ret_chunk, _ret_init, 5, 3, rows=rows, valid=valid),
        grid=(B // ns, nc),
        in_specs=[seq(256), seq(256), seq(512), seq(512), st, tab, tab, _full((1, BRANCH_W))],
        out_specs=[seq(BRANCH_W), st],
        out_shape=[jax.ShapeDtypeStruct((B, L, BRANCH_W), F32), jax.ShapeDtypeStruct((B, 2 * LANES, LANES), F32)],
        scratch_shapes=[pltpu.VMEM((ns, 2 * LANES, LANES), F32)],
        compiler_params=pltpu.CompilerParams(dimension_semantics=("parallel", "arbitrary")),
        name="retention",
    )(q, k, v, g, s0, cos, sin, nw)


def _rel_bias_minus_far(dist, tb_ref, head):
    far = tb_ref[REL_BUCKETS - 1, head]
    val = jnp.full(dist.shape, tb_ref[0, head] - far, F32)
    for bk in range(1, REL_BUCKETS):
        val = jnp.where(dist >= REL_LB[bk], tb_ref[bk, head] - far, val)
    return val


def _softmax_step(s, v_bf16, m_ref, l_ref, acc_ref):
    m_prev = m_ref[...]
    m_new = jnp.maximum(m_prev, jnp.max(s, axis=-1, keepdims=True))
    alpha = jnp.exp(m_prev - m_new)
    p = jnp.exp(s - m_new)
    l_ref[...] = alpha * l_ref[...] + jnp.sum(p, axis=-1, keepdims=True)
    acc_ref[...] = alpha * acc_ref[...] + _dot(p.astype(BF16), v_bf16)
    m_ref[...] = m_new


def _softmax_init(m_ref, l_ref, acc_ref):
    m_ref[...] = jnp.full(m_ref.shape, NEG_INF, F32)
    l_ref[...] = jnp.zeros(l_ref.shape, F32)
    acc_ref[...] = jnp.zeros(acc_ref.shape, F32)


def _diff_stack_q(q, qs_ref, rows):
    lane = lax.broadcasted_iota(jnp.int32, (1, LANES), 1)
    for g in range(2):
        qg = q[:, g * LANES:(g + 1) * LANES] * (DIFF_DH ** -0.5)
        for m in range(2):
            r = (g * 2 + m) * rows
            qs_ref[r:r + rows, :] = jnp.where((lane // DIFF_DH) == m, qg, 0.0).astype(BF16)


def _diff_finish(o, sc_ref, subln, rows):
    outs = []
    for g in range(2):
        og = o[(2 * g) * rows:(2 * g + 1) * rows] - sc_ref[0] * o[(2 * g + 1) * rows:(2 * g + 2) * rows]
        outs.append(_rms(og) * subln * sc_ref[1])
    return jnp.concatenate(outs, axis=-1)


FAR_TILES = 4


Q_TILES = 2


def _causal_sweep(qi, visit):
    n_far = jnp.maximum(Q_TILES * qi - 1, 0)
    n_full = n_far // FAR_TILES
    rem = n_far - n_full * FAR_TILES

    def far(i, carry):
        visit(i * FAR_TILES, FAR_TILES, 0)
        return carry

    lax.fori_loop(0, n_full, far, 0)

    @pl.when(qi == 0)
    def _():
        visit(0, Q_TILES, Q_TILES)

    for r in sorted({(Q_TILES * i - 1) % FAR_TILES for i in range(1, FAR_TILES + 1)}):
        @pl.when(jnp.logical_and(qi >= 1, rem == r))
        def _(r=r):
            visit(n_full * FAR_TILES, r + Q_TILES + 1, Q_TILES + 1)


def _key_rows(tile0, ntiles):
    start = tile0 * CHUNK
    return pl.ds(start if isinstance(start, int) else pl.multiple_of(start, CHUNK), ntiles * CHUNK)


def _two_pass_attention(qi, score_fn, value_fn, s_ref, m_ref, l_ref, acc_ref, halves):
    m_ref[...] = jnp.full(m_ref.shape, NEG_INF, F32)

    def track_max(tile0, ntiles, near):
        for rows in halves:
            s = score_fn(rows, tile0, ntiles, near)
            mx = None
            for c in range(ntiles):
                sc = s[:, c * CHUNK:(c + 1) * CHUNK]
                s_ref[tile0 + c, rows, :] = sc
                mx = sc if mx is None else jnp.maximum(mx, sc)
            m_ref[rows, :] = jnp.maximum(m_ref[rows, :], mx)

    _causal_sweep(qi, track_max)
    m_ref[...] = jnp.broadcast_to(jnp.max(m_ref[...], axis=-1, keepdims=True), m_ref.shape)
    l_ref[...] = jnp.zeros(l_ref.shape, F32)
    acc_ref[...] = jnp.zeros(acc_ref.shape, F32)

    def accumulate(tile0, ntiles, near):
        v = value_fn(tile0, ntiles)
        for rows in halves:
            m = m_ref[rows, :]
            ps = [jnp.exp(s_ref[tile0 + c, rows, :] - m) for c in range(ntiles)]
            lsum = ps[0]
            for pc in ps[1:]:
                lsum = lsum + pc
            l_ref[rows, :] += lsum
            p = ps[0] if len(ps) == 1 else jnp.concatenate(ps, axis=-1)
            acc_ref[rows, :] += _dot(p.astype(BF16), v)

    _causal_sweep(qi, accumulate)
    return acc_ref[...] / jnp.sum(l_ref[...], axis=-1, keepdims=True)


def _diffp_body(tb_ref, sc_ref, q_ref, k_ref, v_ref, sub_ref, y_ref,
                qs_ref, s_ref, m_ref, l_ref, acc_ref, tn_ref):
    tq = Q_TILES * CHUNK
    near_w = (Q_TILES + 1) * CHUNK
    h = pl.program_id(1)
    qi = pl.program_id(2)

    @pl.when(qi == 0)
    def _():
        ri = lax.broadcasted_iota(jnp.int32, (tq, near_w), 0)
        ci = lax.broadcasted_iota(jnp.int32, (tq, near_w), 1)
        dist = ri + CHUNK - ci
        for g in range(2):
            bias = jnp.where(dist >= 0, _rel_bias_minus_far(jnp.maximum(dist, 0), tb_ref, h * 2 + g), NEG_INF)
            for m in range(2):
                r = (g * 2 + m) * tq
                tn_ref[r:r + tq, :] = bias

    _diff_stack_q(q_ref[...].astype(F32), qs_ref, tq)

    def score_fn(rows, tile0, ntiles, near):
        s = _dot_nt(qs_ref[rows, :], k_ref[_key_rows(tile0, ntiles), :].astype(BF16))
        if near:
            far_w = (ntiles - near) * CHUNK
            biased = s[:, far_w:] + tn_ref[rows, near_w - near * CHUNK:]
            s = biased if far_w == 0 else jnp.concatenate([s[:, :far_w], biased], axis=-1)
        return s

    def value_fn(tile0, ntiles):
        return v_ref[_key_rows(tile0, ntiles), :].astype(BF16)

    groups = tuple(slice(i * tq, (i + 1) * tq) for i in range(4))
    o = _two_pass_attention(qi, score_fn, value_fn, s_ref, m_ref, l_ref, acc_ref, groups)
    y_ref[...] = _diff_finish(o, sc_ref, sub_ref[...], tq)


def _diff_prompt(tb, sc, dq, dk, dv, subln):
    B, L, _ = dq.shape
    tq = Q_TILES * CHUNK
    assert CHUNK + 1 >= REL_FAR and L % tq == 0
    return pl.pallas_call(
        _diffp_body,
        grid=(B, DIFF_KVH, L // tq),
        in_specs=[_SMEM, _SMEM,
                  pl.BlockSpec((None, tq, 2 * LANES), lambda b, h, i: (b, i, h)),
                  pl.BlockSpec((None, L, LANES), lambda b, h, i: (b, 0, h)),
                  pl.BlockSpec((None, L, LANES), lambda b, h, i: (b, 0, h)),
                  _full((1, DIFF_DV))],
        out_specs=pl.BlockSpec((None, tq, 2 * LANES), lambda b, h, i: (b, i, h)),
        out_shape=jax.ShapeDtypeStruct((B, L, BRANCH_W), F32),
        scratch_shapes=[pltpu.VMEM((4 * tq, LANES), BF16), pltpu.VMEM((L // CHUNK, 4 * tq, CHUNK), F32),
                        pltpu.VMEM((4 * tq, CHUNK), F32), pltpu.VMEM((4 * tq, CHUNK), F32),
                        pltpu.VMEM((4 * tq, LANES), F32), pltpu.VMEM((4 * tq, (Q_TILES + 1) * CHUNK), F32)],
        compiler_params=pltpu.CompilerParams(dimension_semantics=("parallel", "parallel", "arbitrary"),
                                             vmem_limit_bytes=VMEM_LIMIT),
        name="diff_prompt",
    )(tb, sc, dq, dk, dv, subln)


MLA_SCALE = (MLA_NOPE + MLA_ROPE) ** -0.5


def _mla_q_prep(mq, qn, wq_ref, wuk_ref, cosf, sins, qs_ref, rows):
    cq = (_rms(mq) * qn).astype(BF16)
    qh = _dot(cq, wq_ref[...])
    for h in range(MLA_HEADS):
        nope = qh[:, h * 256:h * 256 + LANES].astype(BF16)
        rope = _rotate(qh[:, h * 256 + LANES:(h + 1) * 256], cosf, sins, MLA_ROPE // 2)
        qs_ref[h * rows:(h + 1) * rows, :LANES] = (_dot(nope, wuk_ref[h]) * MLA_SCALE).astype(BF16)
        qs_ref[h * rows:(h + 1) * rows, LANES:] = (rope * MLA_SCALE).astype(BF16)


def _mla_kv_prep(mkv, kvn, cosf, sins):
    c_kv = _rms(mkv[:, :LANES]) * kvn
    k_rope = _rotate(mkv[:, LANES:], cosf, sins, MLA_ROPE // 2)
    return c_kv, k_rope


def _mla_finish(o, wuv_ref, rows):
    o = o.astype(BF16)
    return jnp.concatenate([_dot(o[h * rows:(h + 1) * rows], wuv_ref[h]) for h in range(MLA_HEADS)], axis=-1)


def _mlap_body(mq_ref, mkv_ref, cq_ref, sq_ref, ck_ref, sk_ref, qn_ref, kvn_ref, wq_ref, wuk_ref, wuv_ref,
               y_ref, lat_ref, kr_ref, kcat_ref, qs_ref, s_ref, m_ref, l_ref, acc_ref):
    tq = Q_TILES * CHUNK
    qi = pl.program_id(1)

    @pl.when(qi == 0)
    def _():
        c_kv, k_rope = _mla_kv_prep(mkv_ref[...], kvn_ref[...], ck_ref[...], sk_ref[...])
        lat_ref[...] = c_kv
        kr_ref[...] = k_rope[:, :MLA_ROPE]
        kcat_ref[:, :LANES] = c_kv.astype(BF16)
        kcat_ref[:, LANES:] = k_rope.astype(BF16)

    _mla_q_prep(mq_ref[...].astype(F32), qn_ref[...], wq_ref, wuk_ref, cq_ref[...], sq_ref[...], qs_ref, tq)

    def score_fn(rows, tile0, ntiles, near):
        s = _dot_nt(qs_ref[rows, :], kcat_ref[_key_rows(tile0, ntiles), :])
        if near:
            ri = lax.broadcasted_iota(jnp.int32, (tq, ntiles * CHUNK), 0)
            ci = lax.broadcasted_iota(jnp.int32, (tq, ntiles * CHUNK), 1) - (ntiles - Q_TILES) * CHUNK
            s = jnp.where(ri >= ci, s, NEG_INF)
        return s

    def value_fn(tile0, ntiles):
        return kcat_ref[_key_rows(tile0, ntiles), :LANES]

    groups = tuple(slice(h * tq, (h + 1) * tq) for h in range(MLA_HEADS))
    o = _two_pass_attention(qi, score_fn, value_fn, s_ref, m_ref, l_ref, acc_ref, groups)
    y_ref[...] = _mla_finish(o, wuv_ref, tq)


def _mla_prompt(mq, mkv, cos, sin, w):
    B, L, _ = mq.shape
    tq = Q_TILES * CHUNK
    assert L % tq == 0
    return pl.pallas_call(
        _mlap_body,
        grid=(B, L // tq),
        in_specs=[pl.BlockSpec((None, tq, 256), lambda b, i: (b, i, 0)),
                  pl.BlockSpec((None, L, 256), lambda b, i: (b, 0, 0)),
                  pl.BlockSpec((tq, LANES), lambda b, i: (i, 0)), pl.BlockSpec((tq, LANES), lambda b, i: (i, 0)),
                  _full((L, LANES)), _full((L, LANES)),
                  _full((1, MLA_Q_RANK)), _full((1, MLA_KV_RANK)),
                  _full((MLA_Q_RANK, MLA_HEADS * 256)), _full((MLA_HEADS, LANES, LANES)),
                  _full((MLA_HEADS, LANES, LANES))],
        out_specs=[pl.BlockSpec((None, tq, BRANCH_W), lambda b, i: (b, i, 0)),
                   pl.BlockSpec((None, L, MLA_KV_RANK), lambda b, i: (b, 0, 0)),
                   pl.BlockSpec((None, L, MLA_ROPE), lambda b, i: (b, 0, 0))],
        out_shape=[jax.ShapeDtypeStruct((B, L, BRANCH_W), F32), jax.ShapeDtypeStruct((B, L, MLA_KV_RANK), F32),
                   jax.ShapeDtypeStruct((B, L, MLA_ROPE), F32)],
        scratch_shapes=[pltpu.VMEM((L, 2 * LANES), BF16), pltpu.VMEM((MLA_HEADS * tq, 2 * LANES), BF16),
                        pltpu.VMEM((L // CHUNK, MLA_HEADS * tq, CHUNK), F32),
                        pltpu.VMEM((MLA_HEADS * tq, CHUNK), F32), pltpu.VMEM((MLA_HEADS * tq, CHUNK), F32),
                        pltpu.VMEM((MLA_HEADS * tq, LANES), F32)],
        compiler_params=pltpu.CompilerParams(dimension_semantics=("parallel", "arbitrary"),
                                             vmem_limit_bytes=VMEM_LIMIT),
        name="mla_prompt",
    )(mq, mkv, cos, sin, cos, sin, w["mla_q_norm"], w["mla_kv_norm"], w["mla_wq"], w["mla_wuk"], w["mla_wuv"])


SROWS = 8


def _diffs_body(pt_ref, tb_ref, sc_ref, q_ref, kn_ref, vn_ref, sub_ref, *rest, npg, past_len):
    kt_refs = rest[:npg]
    v_refs = rest[npg:2 * npg]
    y_ref, qs_ref, m_ref, l_ref, acc_ref = rest[2 * npg:]
    j = pl.program_id(1)
    nj = pl.num_programs(1)
    width = npg * PAGE

    @pl.when(j == 0)
    def _():
        for h in range(DIFF_KVH):
            _diff_stack_q(q_ref[:, h * 256:(h + 1) * 256].astype(F32), qs_ref.at[h], SROWS)
        _softmax_init(m_ref, l_ref, acc_ref)

    def visit(h, bias):
        kt = jnp.concatenate([kt_refs[i][h * LANES:(h + 1) * LANES, :].astype(BF16) for i in range(npg)], axis=1)
        s = _dot(qs_ref[h], kt)
        if bias is not None:
            s = s + bias
        v = jnp.concatenate([v_refs[i][pl.ds(h, PAGE, stride=DIFF_KVH), :].astype(BF16) for i in range(npg)], axis=0)
        _softmax_step(s, v, m_ref.at[h], l_ref.at[h], acc_ref.at[h])

    @pl.when(j < nj - 1)
    def _():
        for h in range(DIFF_KVH):
            visit(h, None)

    @pl.when(j == nj - 1)
    def _():
        t_row = lax.broadcasted_iota(jnp.int32, (2 * SROWS, width), 0) % SROWS
        ci = lax.broadcasted_iota(jnp.int32, (2 * SROWS, width), 1)
        dist = past_len + t_row - ((nj - 1) * width + ci)
        t_new = lax.broadcasted_iota(jnp.int32, (2 * SROWS, PAGE), 0) % SROWS
        c_new = lax.broadcasted_iota(jnp.int32, (2 * SROWS, PAGE), 1)
        for h in range(DIFF_KVH):
            hs = slice(h * LANES, (h + 1) * LANES)
            visit(h, jnp.concatenate([_rel_bias_minus_far(dist, tb_ref, h * 2 + g) for g in range(2)], axis=0))
            kn = _pad_rows(kn_ref[:, hs], PAGE).astype(BF16)
            vn = _pad_rows(vn_ref[:, hs], PAGE).astype(BF16)
            bias_n = jnp.concatenate(
                [jnp.where(t_new >= c_new, _rel_bias_minus_far(jnp.maximum(t_new - c_new, 0), tb_ref, h * 2 + g), NEG_INF)
                 for g in range(2)], axis=0)
            _softmax_step(_dot_nt(qs_ref[h], kn) + bias_n, vn, m_ref.at[h], l_ref.at[h], acc_ref.at[h])
            y_ref[:, h * 256:(h + 1) * 256] = _diff_finish(acc_ref[h] / l_ref[h], sc_ref, sub_ref[...], SROWS)


def _pages_per_step(n_pages, want):
    return math.gcd(n_pages, want)


def _diff_sample(pt, tb, sc, dq, dk, dv, subln, cache_kt, cache_v, layer):
    B = dq.shape[0]
    n_pages = pt.shape[1]
    npg = _pages_per_step(n_pages, 16)
    R = 4 * SROWS
    tok = lambda width: pl.BlockSpec((None, SROWS, width), lambda b, j, pt: (b, 0, 0))
    page = lambda i: pl.BlockSpec((None, None, 2 * PAGE, LANES), lambda b, j, pt: (pt[b, j * npg + i], layer, 0, 0))
    grid_spec = pltpu.PrefetchScalarGridSpec(
        num_scalar_prefetch=1,
        grid=(B, n_pages // npg),
        in_specs=[_SMEM, _SMEM, tok(512), tok(256), tok(256), pl.BlockSpec((1, DIFF_DV), lambda b, j, pt: (0, 0))]
        + [page(i) for i in range(npg)] + [page(i) for i in range(npg)],
        out_specs=tok(BRANCH_W),
        scratch_shapes=[pltpu.VMEM((DIFF_KVH, R, LANES), BF16), pltpu.VMEM((DIFF_KVH, R, 1), F32),
                        pltpu.VMEM((DIFF_KVH, R, 1), F32), pltpu.VMEM((DIFF_KVH, R, LANES), F32)],
    )
    return pl.pallas_call(
        functools.partial(_diffs_body, npg=npg, past_len=n_pages * PAGE),
        grid_spec=grid_spec,
        out_shape=jax.ShapeDtypeStruct((B, SROWS, BRANCH_W), F32),
        compiler_params=pltpu.CompilerParams(dimension_semantics=("parallel", "arbitrary"),
                                             vmem_limit_bytes=VMEM_LIMIT),
        name="diff_sample",
    )(pt, tb, sc, dq, dk, dv, subln, *([cache_kt] * npg), *([cache_v] * npg))


def _mlas_body(pt_ref, mq_ref, mkv_ref, cos_ref, sin_ref, qn_ref, kvn_ref, wq_ref, wuk_ref, wuv_ref, *rest, npg):
    c_refs = rest[:npg]
    rt_refs = rest[npg:2 * npg]
    y_ref, lat_ref, kr_ref, qs_ref, kn_ref, m_ref, l_ref, acc_ref = rest[2 * npg:]
    j = pl.program_id(1)
    nj = pl.num_programs(1)
    R = MLA_HEADS * SROWS

    @pl.when(j == 0)
    def _():
        c_kv, k_rope = _mla_kv_prep(mkv_ref[...], kvn_ref[...], cos_ref[...], sin_ref[...])
        lat_ref[...] = c_kv
        kr_ref[...] = k_rope[:, :MLA_ROPE]
        kn_ref[...] = jnp.zeros(kn_ref.shape, BF16)
        kn_ref[0:SROWS, :LANES] = c_kv.astype(BF16)
        kn_ref[0:SROWS, LANES:] = k_rope.astype(BF16)
        _mla_q_prep(mq_ref[...].astype(F32), qn_ref[...], wq_ref, wuk_ref, cos_ref[...], sin_ref[...], qs_ref, SROWS)
        _softmax_init(m_ref, l_ref, acc_ref)

    lat = jnp.concatenate([c_refs[i][...].astype(BF16) for i in range(npg)], axis=0)
    krt = jnp.concatenate([rt_refs[i][...].astype(BF16) for i in range(npg)], axis=1)
    s = _dot_nt(qs_ref[:, :LANES], lat) + _dot(qs_ref[:, LANES:LANES + MLA_ROPE], krt)
    _softmax_step(s, lat, m_ref, l_ref, acc_ref)

    @pl.when(j == nj - 1)
    def _():
        kc = kn_ref[...]
        t_new = lax.broadcasted_iota(jnp.int32, (R, PAGE), 0) % SROWS
        c_new = lax.broadcasted_iota(jnp.int32, (R, PAGE), 1)
        sn = jnp.where(t_new >= c_new, _dot_nt(qs_ref[...], kc), NEG_INF)
        _softmax_step(sn, kc[:, :LANES], m_ref, l_ref, acc_ref)
        y_ref[...] = _mla_finish(acc_ref[...] / l_ref[...], wuv_ref, SROWS)


def _mla_sample(pt, mq, mkv, cos, sin, w, cache_c, cache_rt, layer):
    B = mq.shape[0]
    n_pages = pt.shape[1]
    npg = _pages_per_step(n_pages, 32)
    R = MLA_HEADS * SROWS
    tok = lambda width: pl.BlockSpec((None, SROWS, width), lambda b, j, pt: (b, 0, 0))
    full = lambda *shape: pl.BlockSpec(shape, lambda b, j, pt: (0,) * len(shape))
    page = lambda i, rows: pl.BlockSpec((None, None, rows, LANES), lambda b, j, pt: (pt[b, j * npg + i], layer, 0, 0))
    grid_spec = pltpu.PrefetchScalarGridSpec(
        num_scalar_prefetch=1,
        grid=(B, n_pages // npg),
        in_specs=[tok(256), tok(256), full(SROWS, LANES), full(SROWS, LANES), full(1, MLA_Q_RANK), full(1, MLA_KV_RANK),
                  full(MLA_Q_RANK, MLA_HEADS * 256), full(MLA_HEADS, LANES, LANES), full(MLA_HEADS, LANES, LANES)]
        + [page(i, PAGE) for i in range(npg)] + [page(i, MLA_ROPE) for i in range(npg)],
        out_specs=[tok(BRANCH_W), tok(MLA_KV_RANK), tok(MLA_ROPE)],
        scratch_shapes=[pltpu.VMEM((R, 2 * LANES), BF16), pltpu.VMEM((PAGE, 2 * LANES), BF16),
                        pltpu.VMEM((R, 1), F32), pltpu.VMEM((R, 1), F32), pltpu.VMEM((R, LANES), F32)],
    )
    return pl.pallas_call(
        functools.partial(_mlas_body, npg=npg),
        grid_spec=grid_spec,
        out_shape=[jax.ShapeDtypeStruct((B, SROWS, BRANCH_W), F32), jax.ShapeDtypeStruct((B, SROWS, MLA_KV_RANK), F32),
                   jax.ShapeDtypeStruct((B, SROWS, MLA_ROPE), F32)],
        compiler_params=pltpu.CompilerParams(dimension_semantics=("parallel", "arbitrary"),
                                             vmem_limit_bytes=VMEM_LIMIT),
        name="mla_sample",
    )(pt, mq, mkv, cos, sin, w["mla_q_norm"], w["mla_kv_norm"], w["mla_wq"], w["mla_wuk"], w["mla_wuv"],
      *([cache_c] * npg), *([cache_rt] * npg))


def _merge_body(x_ref, y0_ref, y1_ref, y2_ref, y3_ref, gt_ref, wb_ref, wo_ref, o_ref):
    mix = None
    for i, y_ref in enumerate((y0_ref, y1_ref, y2_ref, y3_ref)):
        proj = _dot(y_ref[...].astype(BF16), wb_ref[i])
        gate = gt_ref[:, i * D_MODEL:(i + 1) * D_MODEL].astype(F32)
        term = proj / (1.0 + jnp.exp(-gate))
        mix = term if mix is None else mix + term
    o_ref[...] = x_ref[...] + _dot(mix.astype(BF16), wo_ref[...])


def _merge(x, ys, gates, wb, wo):
    T = x.shape[0]
    tm = min(512, T)
    row = lambda width: pl.BlockSpec((tm, width), lambda i: (i, 0))
    return pl.pallas_call(
        _merge_body,
        grid=(T // tm,),
        in_specs=[row(D_MODEL)] + [row(BRANCH_W)] * 4 + [row(N_BRANCH * D_MODEL),
                                                       _resident((N_BRANCH, BRANCH_W, D_MODEL)),
                                                       _resident((D_MODEL, D_MODEL))],
        out_specs=row(D_MODEL),
        out_shape=jax.ShapeDtypeStruct((T, D_MODEL), F32),
        compiler_params=pltpu.CompilerParams(dimension_semantics=("parallel",), vmem_limit_bytes=VMEM_LIMIT),
        name="merge",
    )(x, *ys, gates, wb, wo)


def _rope_tables(pos, half):
    inv_freq = ROPE_BASE ** (-jnp.arange(half, dtype=F32) / half)
    ang = pos.astype(F32)[:, None] * inv_freq[None, :]
    cos, sin = jnp.cos(ang), jnp.sin(ang)
    return jnp.concatenate([cos, cos], axis=-1), jnp.concatenate([-sin, sin], axis=-1)


def _layer_weights(l, p):
    w = {}
    f = lambda a: a.astype(BF16)
    for n in ("ffn1_w1", "ffn1_w3", "ffn1_w2", "ffn2_w1", "ffn2_w3", "ffn2_w2", "w_out"):
        w[n] = f(p[n][l])
    w["w_branch"] = f(p["w_branch"][l])
    for n in ("norm_ffn1", "norm_mix", "norm_ffn2", "ssm_conv_b", "ssm_norm", "ret_norm", "diff_subln",
              "mla_q_norm", "mla_kv_norm"):
        w[n] = p[n][l][None, :]
    offs = np.cumsum((0,) + IN_SIZES)
    piece = lambda i: p["w_in"][l][:, offs[i]:offs[i + 1]]
    padc = lambda a, width: jnp.pad(a, ((0, 0), (0, width - a.shape[1])))
    cols = [piece(0), piece(1), padc(piece(2), LANES), piece(3), piece(4), piece(5), piece(6), piece(7), piece(8),
            piece(9), piece(10), padc(piece(11), 2 * LANES), piece(12)]
    w["w_in"] = f(jnp.concatenate(cols, axis=1))
    w["conv_w"] = p["ssm_conv_w"][l]
    w["conv_b"] = w["ssm_conv_b"]
    w["dt_bias"] = padc(p["ssm_dt_bias"][l][None, :], LANES)
    w["dt_biasT"] = p["ssm_dt_bias"][l][:, None]
    w["a_log"] = padc(p["ssm_a_log"][l][None, :], LANES)
    w["a_logT"] = p["ssm_a_log"][l][:, None]
    w["d_exp"] = jnp.repeat(p["ssm_d"][l], SSM_HEAD_DIM)[None, :]
    expand = np.zeros((LANES, BRANCH_W), np.float32)
    for h in range(SSM_HEADS):
        expand[h, h * SSM_HEAD_DIM:(h + 1) * SSM_HEAD_DIM] = 1.0
    w["expand"] = jnp.asarray(expand)
    wq = p["mla_w_q_up"][l]
    zq = jnp.zeros((MLA_Q_RANK, MLA_HEADS, LANES - MLA_NOPE), F32)
    zr = jnp.zeros((MLA_Q_RANK, MLA_HEADS, LANES - MLA_ROPE), F32)
    w["mla_wq"] = f(jnp.concatenate([wq[..., :MLA_NOPE], zq, wq[..., MLA_NOPE:], zr], axis=-1)
                    .reshape(MLA_Q_RANK, MLA_HEADS * 256))
    wuk = jnp.transpose(p["mla_w_uk"][l], (1, 2, 0))
    w["mla_wuk"] = f(jnp.pad(wuk, ((0, 0), (0, LANES - MLA_NOPE), (0, 0))))
    w["mla_wuv"] = f(jnp.transpose(p["mla_w_uv"][l], (1, 0, 2)))
    lam_init = 0.8 - 0.6 * math.exp(-0.3 * l)
    lw = p["diff_lambda"][l].astype(F32)
    lam = jnp.exp(jnp.sum(lw[0] * lw[1])) - jnp.exp(jnp.sum(lw[2] * lw[3])) + lam_init
    w["diff_sc"] = jnp.stack([lam, jnp.asarray(1.0 - lam_init, F32)]).astype(F32)
    return w


def _ssm_state_to_pairs(s):
    B = s.shape[0]
    return s.reshape(B, 4, 2, SSM_STATE, SSM_HEAD_DIM).transpose(0, 1, 3, 2, 4).reshape(B, 4, SSM_STATE, 2 * SSM_HEAD_DIM)


def _ssm_state_from_pairs(s):
    B = s.shape[0]
    return s.reshape(B, 4, SSM_STATE, 2, SSM_HEAD_DIM).transpose(0, 1, 3, 2, 4).reshape(B, SSM_HEADS, SSM_STATE, SSM_HEAD_DIM)


def _trunk(x, w_layers, rel_bias, final_norm, rows, valid, pos, conv0, ssm0, ret0, paged):
    B, L, _ = x.shape
    T = B * L
    depth = len(w_layers)
    xt = x.reshape(T, D_MODEL)
    ret_cos, ret_sin = _rope_tables(pos, RET_QK // 2)
    ret_cos, ret_sin = jnp.tile(ret_cos, (1, 4)), jnp.tile(ret_sin, (1, 4))
    mc, ms = _rope_tables(pos, MLA_ROPE // 2)
    mla_cos = jnp.pad(mc, ((0, 0), (0, LANES - MLA_ROPE)))
    mla_sin = jnp.pad(ms, ((0, 0), (0, LANES - MLA_ROPE)))
    nc = L // rows
    outs = []
    for l in range(depth):
        w = w_layers[l]
        xt = _ffn(xt, w["norm_ffn1"], w["ffn1_w1"], w["ffn1_w3"], w["ffn1_w2"])
        pr = _inproj(xt, w["norm_mix"], w["w_in"])
        r3 = lambda a: a.reshape(B, L, a.shape[-1])
        dt = r3(pr["dt"])
        dtT = jnp.transpose(dt[:, :, :SSM_HEADS].reshape(B, nc, rows, SSM_HEADS), (0, 3, 1, 2))
        dtT = jnp.pad(dtT, ((0, 0), (0, 0), (0, 0), (0, CHUNK - rows))).reshape(B, SSM_HEADS, nc * CHUNK)
        dtp = dt if rows == CHUNK else jnp.pad(dt, ((0, 0), (0, CHUNK - rows), (0, 0)))
        y_ssm, conv_new, ssm_new = _ssd(r3(pr["xbc"]), r3(pr["z"]), dtp, dtT, conv0[l], ssm0[l], w, rows, valid)
        y_ret, ret_new = _ret(r3(pr["rq"]), r3(pr["rk"]), r3(pr["rv"]), r3(pr["rg"]), ret_cos, ret_sin, ret0[l],
                              w["ret_norm"], rows, valid)
        dk, dv = r3(pr["dk"]), r3(pr["dv"])
        if paged is None:
            y_diff = _diff_prompt(rel_bias, w["diff_sc"], r3(pr["dq"]), dk, dv, w["diff_subln"])
            y_mla, lat, kr = _mla_prompt(r3(pr["mq"]), r3(pr["mkv"]), mla_cos, mla_sin, w)
        else:
            pt, ck, cv, cc, cr = paged
            y_diff = _diff_sample(pt, rel_bias, w["diff_sc"], r3(pr["dq"]), dk, dv, w["diff_subln"], ck, cv, l)
            y_mla, lat, kr = _mla_sample(pt, r3(pr["mq"]), r3(pr["mkv"]), mla_cos, mla_sin, w, cc, cr, l)
        ys = [a.reshape(T, BRANCH_W) for a in (y_ssm, y_ret, y_diff, y_mla)]
        xt = _merge(xt, ys, pr["gates"], w["w_branch"], w["w_out"])
        fg = final_norm if l == depth - 1 else None
        xt = _ffn(xt, w["norm_ffn2"], w["ffn2_w1"], w["ffn2_w3"], w["ffn2_w2"], fg)
        outs.append((dk, dv, lat, kr, conv_new, ssm_new, ret_new))
    return xt.reshape(B, L, D_MODEL), outs


def kernel(x_prompt, x_sample, cache_diff_k, cache_diff_v, cache_mla_latent, cache_mla_krope, state_ssm_conv, state_ssm, state_retention, page_table, norm_ffn1, ffn1_w1, ffn1_w3, ffn1_w2, norm_mix, w_in, ssm_conv_w, ssm_conv_b, ssm_dt_bias, ssm_a_log, ssm_d, ssm_norm, ret_norm, diff_lambda, diff_subln, mla_q_norm, mla_w_q_up, mla_kv_norm, mla_w_uk, mla_w_uv, rel_bias, w_branch, w_out, norm_ffn2, ffn2_w1, ffn2_w3, ffn2_w2, final_norm):
    p = dict(norm_ffn1=norm_ffn1, ffn1_w1=ffn1_w1, ffn1_w3=ffn1_w3, ffn1_w2=ffn1_w2, norm_mix=norm_mix, w_in=w_in,
             ssm_conv_w=ssm_conv_w, ssm_conv_b=ssm_conv_b, ssm_dt_bias=ssm_dt_bias, ssm_a_log=ssm_a_log, ssm_d=ssm_d,
             ssm_norm=ssm_norm, ret_norm=ret_norm, diff_lambda=diff_lambda, diff_subln=diff_subln,
             mla_q_norm=mla_q_norm, mla_w_q_up=mla_w_q_up, mla_kv_norm=mla_kv_norm, mla_w_uk=mla_w_uk,
             mla_w_uv=mla_w_uv, w_branch=w_branch, w_out=w_out, norm_ffn2=norm_ffn2, ffn2_w1=ffn2_w1,
             ffn2_w3=ffn2_w3, ffn2_w2=ffn2_w2)
    depth = w_in.shape[0]
    w_layers = [_layer_weights(l, p) for l in range(depth)]
    fnorm = final_norm[None, :]
    rel = rel_bias.astype(F32)

    Bp, Lp, _ = x_prompt.shape
    zc = jnp.zeros((Bp, CONV_W - 1, SSM_CONV_DIM), F32)
    zs = jnp.zeros((Bp, 4, LANES, LANES), F32)
    zr = jnp.zeros((Bp, 2 * LANES, LANES), F32)
    y_prompt, rows_p = _trunk(x_prompt, w_layers, rel, fnorm, CHUNK, CHUNK, jnp.arange(Lp, dtype=jnp.int32),
                              [zc] * depth, [zs] * depth, [zr] * depth, None)

    Bs, Ls, _ = x_sample.shape
    n_pages = page_table.shape[1]
    past_len = n_pages * PAGE
    xs = jnp.pad(x_sample, ((0, 0), (0, SROWS - Ls), (0, 0)))
    pos_s = past_len + jnp.arange(SROWS, dtype=jnp.int32)
    n_pool = cache_diff_k.shape[0]
    paged = (page_table.astype(jnp.int32),
             jnp.transpose(cache_diff_k, (0, 1, 3, 4, 5, 2)).reshape(n_pool, depth, 2 * PAGE, LANES),
             cache_diff_v.reshape(n_pool, depth, 2 * PAGE, LANES),
             cache_mla_latent, jnp.transpose(cache_mla_krope, (0, 1, 3, 2)))
    conv_s = [state_ssm_conv[:, l] for l in range(depth)]
    ssm_s = [_ssm_state_to_pairs(state_ssm[:, l]) for l in range(depth)]
    ret_s = [state_retention[:, l].reshape(Bs, 2 * LANES, LANES) for l in range(depth)]
    y_sample, rows_s = _trunk(xs, w_layers, rel, fnorm, SROWS, Ls, pos_s, conv_s, ssm_s, ret_s, paged)

    def collect(rows, B, L, keep):
        st = lambda i: jnp.stack([r[i] for r in rows], axis=1)
        dk = st(0)[:, :, :keep].reshape(B, depth, keep, DIFF_KVH, 2, DIFF_DH)
        dv = st(1)[:, :, :keep].reshape(B, depth, keep, DIFF_KVH, DIFF_DV)
        lat = st(2)[:, :, :keep]
        kr = st(3)[:, :, :keep]
        conv = st(4)
        ssm = jnp.stack([_ssm_state_from_pairs(r[5]) for r in rows], axis=1)
        ret = st(6).reshape(B, depth, RET_HEADS, RET_QK, RET_V)
        return dk, dv, lat, kr, conv, ssm, ret

    return (y_prompt, y_sample[:, :Ls]) + collect(rows_p, Bp, Lp, Lp) + collect(rows_s, Bs, SROWS, Ls)
```

```python
import functools
import math

import numpy as np
import jax
import jax.numpy as jnp
from jax import lax
from jax.experimental import pallas as pl
from jax.experimental.pallas import tpu as pltpu

F32 = jnp.float32
BF16 = jnp.bfloat16
NEG_INF = float("-inf")

D_MODEL = 1024
D_FF = 2816
BRANCH_W = 512
N_BRANCH = 4
SSM_HEADS = 8
SSM_HEAD_DIM = 64
SSM_STATE = 128
CONV_W = 4
SSM_CONV_DIM = 1024
RET_HEADS = 4
RET_QK = 64
RET_V = 128
DIFF_KVH = 2
DIFF_DH = 64
DIFF_DV = 128
MLA_HEADS = 4
MLA_Q_RANK = 256
MLA_KV_RANK = 128
MLA_NOPE = 64
MLA_ROPE = 32
PAGE = 128
REL_BUCKETS = 32
REL_MAX_EXACT = 16
REL_MAX_DIST = 128
ROPE_BASE = 10000.0
NORM_EPS = 1e-6
IN_SIZES = (512, 1024, 8, 256, 256, 512, 512, 512, 256, 256, 256, 160, 4096)

LANES = 128
CHUNK = 128
VMEM_LIMIT = 56 * 1024 * 1024


def _rel_lower_bounds():
    lb = list(range(REL_MAX_EXACT))
    d = np.arange(REL_MAX_EXACT, 4 * REL_MAX_DIST).astype(np.float32)
    large = REL_MAX_EXACT + (np.log(d / np.float32(REL_MAX_EXACT)) / np.float32(math.log(REL_MAX_DIST / REL_MAX_EXACT))
                             * np.float32(REL_BUCKETS - REL_MAX_EXACT)).astype(np.int32)
    large = np.minimum(large, REL_BUCKETS - 1)
    for bk in range(REL_MAX_EXACT, REL_BUCKETS):
        lb.append(int(REL_MAX_EXACT + np.argmax(large >= bk)))
    return tuple(lb)


REL_LB = _rel_lower_bounds()
REL_FAR = REL_LB[-1]
RET_LOG_GAMMA = tuple(math.log1p(-(2.0 ** (-5.0 - h))) for h in range(RET_HEADS))


def _dot(a, b, precision=None):
    return jnp.dot(a, b, preferred_element_type=F32, precision=precision)


def _dot_nt(a, b):
    return lax.dot_general(a, b, (((1,), (1,)), ((), ())), preferred_element_type=F32)


def _dot_tn(a, b):
    return lax.dot_general(a, b, (((0,), (0,)), ((), ())), preferred_element_type=F32)


def _rms(x):
    return x * lax.rsqrt(jnp.mean(x * x, axis=-1, keepdims=True) + NORM_EPS)


def _silu(x):
    return x / (1.0 + jnp.exp(-x))


def _softplus(x):
    return jnp.maximum(x, 0.0) + jnp.log1p(jnp.exp(-jnp.abs(x)))


def _pad_rows(x, rows):
    if x.shape[0] == rows:
        return x
    return jnp.concatenate([x, jnp.zeros((rows - x.shape[0],) + x.shape[1:], x.dtype)], axis=0)


def _rotate(x, cosf, sins, half):
    outs = []
    for s in range(x.shape[-1] // LANES):
        xs = x[:, s * LANES:(s + 1) * LANES]
        lane = lax.broadcasted_iota(jnp.int32, xs.shape, 1)
        first = (lane % (2 * half)) < half
        partner = jnp.where(first, pltpu.roll(xs, LANES - half, axis=1), pltpu.roll(xs, half, axis=1))
        outs.append(partner)
    partner = outs[0] if len(outs) == 1 else jnp.concatenate(outs, axis=-1)
    return x * cosf + partner * sins


def _full(shape):
    return pl.BlockSpec(shape, lambda *_: (0,) * len(shape))


def _resident(shape):
    return pl.BlockSpec(shape, lambda *_: (0,) * len(shape), pipeline_mode=pl.Buffered(1))


_SMEM = pl.BlockSpec(memory_space=pltpu.SMEM)


def _ffn_body(*refs, fc, final):
    if final:
        x_ref, g_ref, w1_ref, w3_ref, w2_ref, fg_ref, o_ref = refs
    else:
        x_ref, g_ref, w1_ref, w3_ref, w2_ref, o_ref = refs
    x = x_ref[...]
    xn = (_rms(x) * g_ref[...]).astype(BF16)
    acc = x
    for c in range(D_FF // fc):
        sl = slice(c * fc, (c + 1) * fc)
        h1 = _dot(xn, w1_ref[:, sl])
        h3 = _dot(xn, w3_ref[:, sl])
        a = (_silu(h1) * h3).astype(BF16)
        acc = acc + 0.5 * _dot(a, w2_ref[sl, :])
    if final:
        acc = _rms(acc) * fg_ref[...]
    o_ref[...] = acc


def _ffn(x, g, w1, w3, w2, final_g=None):
    T = x.shape[0]
    tm = min(512, T)
    final = final_g is not None
    in_specs = [pl.BlockSpec((tm, D_MODEL), lambda i: (i, 0)), _full((1, D_MODEL)),
                _resident((D_MODEL, D_FF)), _resident((D_MODEL, D_FF)), _resident((D_FF, D_MODEL))]
    args = [x, g, w1, w3, w2]
    if final:
        in_specs.append(_full((1, D_MODEL)))
        args.append(final_g)
    return pl.pallas_call(
        functools.partial(_ffn_body, fc=D_FF // 2, final=final),
        grid=(T // tm,),
        in_specs=in_specs,
        out_specs=pl.BlockSpec((tm, D_MODEL), lambda i: (i, 0)),
        out_shape=jax.ShapeDtypeStruct((T, D_MODEL), F32),
        compiler_params=pltpu.CompilerParams(dimension_semantics=("parallel",), vmem_limit_bytes=VMEM_LIMIT),
        name="ffn",
    )(*args)


PROJ_OUT = (("z", 512, BF16), ("xbc", 1024, F32), ("dt", 128, F32), ("rq", 256, BF16), ("rk", 256, BF16),
            ("rv", 512, BF16), ("rg", 512, BF16), ("dq", 512, BF16), ("dk", 256, F32), ("dv", 256, F32),
            ("mq", 256, BF16), ("mkv", 256, F32), ("gates", 4096, BF16))
PROJ_W = sum(w for _, w, _ in PROJ_OUT)


def _inproj_body(x_ref, g_ref, w_ref, *o_refs):
    xn = (_rms(x_ref[...]) * g_ref[...]).astype(BF16)
    off = 0
    for (_, w, dt), o_ref in zip(PROJ_OUT, o_refs):
        o_ref[...] = _dot(xn, w_ref[:, off:off + w]).astype(dt)
        off += w


def _inproj(x, g, w):
    T = x.shape[0]
    tm = min(512, T)
    outs = pl.pallas_call(
        _inproj_body,
        grid=(T // tm,),
        in_specs=[pl.BlockSpec((tm, D_MODEL), lambda i: (i, 0)), _full((1, D_MODEL)), _resident((D_MODEL, PROJ_W))],
        out_specs=[pl.BlockSpec((tm, w_), lambda i: (i, 0)) for _, w_, _ in PROJ_OUT],
        out_shape=[jax.ShapeDtypeStruct((T, w_), dt) for _, w_, dt in PROJ_OUT],
        compiler_params=pltpu.CompilerParams(dimension_semantics=("parallel",), vmem_limit_bytes=VMEM_LIMIT),
        name="inproj",
    )(x, g, w)
    return {n: o for (n, _, _), o in zip(PROJ_OUT, outs)}


SEQ_PER_STEP = 2


def _per_sequence(body, init, n_seq_in, n_shared, *refs, **static):
    seq_in, shared, seq_rest = refs[:n_seq_in], refs[n_seq_in:n_seq_in + n_shared], refs[n_seq_in + n_shared:]
    views = [[r.at[i] for r in seq_in] + list(shared) + [r.at[i] for r in seq_rest] for i in range(SEQ_PER_STEP)]

    @pl.when(pl.program_id(1) == 0)
    def _():
        for v in views:
            init(*v)

    for v in views:
        body(*v, **static)


def _ssd_init(xbc_ref, z_ref, dt_ref, dtT_ref, cb0_ref, s0_ref, cw_ref, cbias_ref, dtb_ref, dtbT_ref,
              alog_ref, alogT_ref, dexp_ref, nw_ref, e_ref, y_ref, cnew_ref, snew_ref, xp_ref, s_ref):
    xp_ref[0:8, :] = jnp.zeros((8, SSM_CONV_DIM), F32)
    xp_ref[5:8, :] = cb0_ref[...]
    s_ref[...] = s0_ref[...]


def _ssd_chunk(xbc_ref, z_ref, dt_ref, dtT_ref, cb0_ref, s0_ref, cw_ref, cbias_ref, dtb_ref, dtbT_ref,
               alog_ref, alogT_ref, dexp_ref, nw_ref, e_ref,
               y_ref, cnew_ref, snew_ref, xp_ref, s_ref, *, rows, valid):
    Q = CHUNK
    xp_ref[8:8 + Q, :] = _pad_rows(xbc_ref[...], Q)
    conv = cbias_ref[...] + xp_ref[5:5 + Q, :] * cw_ref[0:1, :]
    for w in range(1, CONV_W):
        conv = conv + xp_ref[5 + w:5 + w + Q, :] * cw_ref[w:w + 1, :]
    u = _silu(conv)
    last3 = xp_ref[8 + valid - 3:8 + valid, :]
    xp_ref[5:8, :] = last3
    cnew_ref[...] = last3

    row = lax.broadcasted_iota(jnp.int32, (Q, Q), 0)
    col = lax.broadcasted_iota(jnp.int32, (Q, Q), 1)
    causal = row >= col
    hi = lax.Precision.HIGHEST

    dtv = _softplus(dt_ref[...] + dtb_ref[...])
    dtv = jnp.where(lax.broadcasted_iota(jnp.int32, dtv.shape, 0) < valid, dtv, 0.0)
    la = -jnp.exp(alog_ref[...]) * dtv
    cum = _dot(causal.astype(F32), la, hi)
    dtT = _softplus(dtT_ref[...] + dtbT_ref[...])
    dtT = jnp.where(lax.broadcasted_iota(jnp.int32, dtT.shape, 1) < valid, dtT, 0.0)
    laT = -jnp.exp(alogT_ref[...]) * dtT
    cumT = _dot(laT, (row <= col).astype(F32), hi)
    cum_exp = _dot(cum, e_ref[...], hi)
    dt_exp = _dot(dtv, e_ref[...], hi)
    cl_exp = cum_exp[Q - 1:Q, :]
    ecum = jnp.exp(cum_exp)
    tail = jnp.exp(cl_exp - cum_exp)
    sdecay = jnp.exp(cl_exp)

    xs = u[:, :BRANCH_W]
    v = xs * dt_exp
    vb = v.astype(BF16)
    vt = (v * tail).astype(BF16)
    lane = lax.broadcasted_iota(jnp.int32, (Q, LANES), 1)
    ys = []
    for g in range(2):
        bg = u[:, 512 + g * 128:512 + (g + 1) * 128].astype(BF16)
        cg = u[:, 768 + g * 128:768 + (g + 1) * 128].astype(BF16)
        gmat = _dot_nt(cg, bg)
        for pp in range(2):
            p = g * 2 + pp
            sl = slice(p * LANES, (p + 1) * LANES)
            s_pair = s_ref[p]
            inter = _dot(cg, s_pair.astype(BF16))
            yh = []
            for hh in range(2):
                h = 2 * p + hh
                seg = cum[:, h:h + 1] - cumT[h:h + 1, :]
                dec = jnp.exp(jnp.where(causal, seg, NEG_INF))
                yh.append(_dot((gmat * dec).astype(BF16), vb[:, sl]))
            ys.append(jnp.where(lane < SSM_HEAD_DIM, yh[0], yh[1]) + inter * ecum[:, sl])
            s_ref[p] = s_pair * sdecay[:, sl] + _dot_tn(bg, vt[:, sl])
    y = jnp.concatenate(ys, axis=-1) + xs * dexp_ref[...]
    y = y * _silu(_pad_rows(z_ref[...].astype(F32), Q))
    half = BRANCH_W // 2
    y = jnp.concatenate([_rms(y[:, :half]), _rms(y[:, half:])], axis=-1) * nw_ref[...]
    y_ref[...] = y[:rows]
    snew_ref[...] = s_ref[...]


def _ssd(xbc, z, dt, dtT, cb0, s0, w, rows, valid):
    B, L, _ = xbc.shape
    nc = L // rows
    ns = SEQ_PER_STEP
    assert B % ns == 0
    seq = lambda width: pl.BlockSpec((ns, rows, width), lambda b, c: (b, c, 0))
    per_b = lambda *shape: pl.BlockSpec((ns,) + shape, lambda b, c: (b,) + (0,) * len(shape))
    return pl.pallas_call(
        functools.partial(_per_sequence, _ssd_chunk, _ssd_init, 6, 9, rows=rows, valid=valid),
        grid=(B // ns, nc),
        in_specs=[seq(SSM_CONV_DIM), seq(BRANCH_W), pl.BlockSpec((ns, CHUNK, LANES), lambda b, c: (b, c, 0)),
                  pl.BlockSpec((ns, SSM_HEADS, CHUNK), lambda b, c: (b, 0, c)),
                  per_b(CONV_W - 1, SSM_CONV_DIM), per_b(4, LANES, LANES),
                  _full((CONV_W, SSM_CONV_DIM)), _full((1, SSM_CONV_DIM)), _full((1, LANES)), _full((SSM_HEADS, 1)),
                  _full((1, LANES)), _full((SSM_HEADS, 1)), _full((1, BRANCH_W)), _full((1, BRANCH_W)),
                  _full((LANES, BRANCH_W))],
        out_specs=[seq(BRANCH_W), per_b(CONV_W - 1, SSM_CONV_DIM), per_b(4, LANES, LANES)],
        out_shape=[jax.ShapeDtypeStruct((B, L, BRANCH_W), F32),
                   jax.ShapeDtypeStruct((B, CONV_W - 1, SSM_CONV_DIM), F32),
                   jax.ShapeDtypeStruct((B, 4, LANES, LANES), F32)],
        scratch_shapes=[pltpu.VMEM((ns, 8 + CHUNK, SSM_CONV_DIM), F32), pltpu.VMEM((ns, 4, LANES, LANES), F32)],
        compiler_params=pltpu.CompilerParams(dimension_semantics=("parallel", "arbitrary")),
        name="ssd",
    )(xbc, z, dt, dtT, cb0, s0, w["conv_w"], w["conv_b"], w["dt_bias"], w["dt_biasT"], w["a_log"], w["a_logT"],
      w["d_exp"], w["ssm_norm"], w["expand"])


def _ret_init(q_ref, k_ref, v_ref, g_ref, s0_ref, cos_ref, sin_ref, nw_ref, y_ref, snew_ref, s_ref):
    s_ref[...] = s0_ref[...]


def _ret_chunk(q_ref, k_ref, v_ref, g_ref, s0_ref, cos_ref, sin_ref, nw_ref, y_ref, snew_ref, s_ref, *, rows, valid):
    Q = CHUNK
    cosf = _pad_rows(cos_ref[...], Q)
    sins = _pad_rows(sin_ref[...], Q)
    q = _rotate(_pad_rows(q_ref[...].astype(F32), Q), cosf, sins, RET_QK // 2)
    k = _rotate(_pad_rows(k_ref[...].astype(F32), Q), cosf, sins, RET_QK // 2) * (RET_QK ** -0.5)
    v = _pad_rows(v_ref[...].astype(F32), Q)
    gate = _pad_rows(g_ref[...].astype(F32), Q)
    ri = lax.broadcasted_iota(jnp.int32, (Q, 1), 0)
    ci = lax.broadcasted_iota(jnp.int32, (1, Q), 1)
    cnt_i = jnp.minimum(ri + 1, valid).astype(F32)
    cnt_j = jnp.minimum(ci + 1, valid).astype(F32)
    causal = ri >= ci
    lane = lax.broadcasted_iota(jnp.int32, (1, LANES), 1)
    ys = []
    for h in range(RET_HEADS):
        lg = RET_LOG_GAMMA[h]
        p, hh = divmod(h, 2)
        mine = (lane // RET_QK) == hh
        qm = jnp.where(mine, q[:, p * LANES:(p + 1) * LANES], 0.0).astype(BF16)
        km = jnp.where(mine, k[:, p * LANES:(p + 1) * LANES], 0.0).astype(BF16)
        dec = jnp.exp(jnp.where(causal, (cnt_i - cnt_j) * lg, NEG_INF))
        sc = (_dot_nt(qm, km) * dec).astype(BF16)
        vh = jnp.where(ri < valid, v[:, h * RET_V:(h + 1) * RET_V], 0.0)
        s_stack = s_ref[p * LANES:(p + 1) * LANES, :]
        y = _dot(sc, vh.astype(BF16)) + _dot(qm, s_stack.astype(BF16)) * jnp.exp(cnt_i * lg)
        tail = jnp.exp((valid - cnt_i) * lg)
        upd = _dot_tn(km, (vh * tail).astype(BF16))
        rs = slice(h * RET_QK, (h + 1) * RET_QK)
        s_ref[rs, :] = s_ref[rs, :] * math.exp(valid * lg) + upd[hh * RET_QK:(hh + 1) * RET_QK, :]
        mu = jnp.mean(y, axis=-1, keepdims=True)
        yc = y - mu
        var = jnp.mean(yc * yc, axis=-1, keepdims=True)
        ys.append(yc * lax.rsqrt(var + NORM_EPS))
    y = jnp.concatenate(ys, axis=-1) * nw_ref[...] * _silu(gate)
    y_ref[...] = y[:rows]
    snew_ref[...] = s_ref[...]


def _ret(q, k, v, g, cos, sin, s0, nw, rows, valid):
    B, L, _ = q.shape
    nc = L // rows
    ns = SEQ_PER_STEP
    assert B % ns == 0
    seq = lambda width: pl.BlockSpec((ns, rows, width), lambda b, c: (b, c, 0))
    tab = pl.BlockSpec((rows, 2 * LANES), lambda b, c: (c, 0))
    st = pl.BlockSpec((ns, 2 * LANES, LANES), lambda b, c: (b, 0, 0))
    return pl.pallas_call(
        functools.partial(_per_sequence, _ret_chunk, _ret_init, 5, 3, rows=rows, valid=valid),
        grid=(B // ns, nc),
        in_specs=[seq(256), seq(256), seq(512), seq(512), st, tab, tab, _full((1, BRANCH_W))],
        out_specs=[seq(BRANCH_W), st],
        out_shape=[jax.ShapeDtypeStruct((B, L, BRANCH_W), F32), jax.ShapeDtypeStruct((B, 2 * LANES, LANES), F32)],
        scratch_shapes=[pltpu.VMEM((ns, 2 * LANES, LANES), F32)],
        compiler_params=pltpu.CompilerParams(dimension_semantics=("parallel", "arbitrary")),
        name="retention",
    )(q, k, v, g, s0, cos, sin, nw)


def _rel_bias_minus_far(dist, tb_ref, head):
    far = tb_ref[REL_BUCKETS - 1, head]
    val = jnp.full(dist.shape, tb_ref[0, head] - far, F32)
    for bk in range(1, REL_BUCKETS):
        val = jnp.where(dist >= REL_LB[bk], tb_ref[bk, head] - far, val)
    return val


def _softmax_step(s, v_bf16, m_ref, l_ref, acc_ref):
    m_prev = m_ref[...]
    m_new = jnp.maximum(m_prev, jnp.max(s, axis=-1, keepdims=True))
    alpha = jnp.exp(m_prev - m_new)
    p = jnp.exp(s - m_new)
    l_ref[...] = alpha * l_ref[...] + jnp.sum(p, axis=-1, keepdims=True)
    acc_ref[...] = alpha * acc_ref[...] + _dot(p.astype(BF16), v_bf16)
    m_ref[...] = m_new


def _softmax_init(m_ref, l_ref, acc_ref):
    m_ref[...] = jnp.full(m_ref.shape, NEG_INF, F32)
    l_ref[...] = jnp.zeros(l_ref.shape, F32)
    acc_ref[...] = jnp.zeros(acc_ref.shape, F32)


def _diff_stack_q(q, qs_ref, rows):
    lane = lax.broadcasted_iota(jnp.int32, (1, LANES), 1)
    for g in range(2):
        qg = q[:, g * LANES:(g + 1) * LANES] * (DIFF_DH ** -0.5)
        for m in range(2):
            r = (g * 2 + m) * rows
            qs_ref[r:r + rows, :] = jnp.where((lane // DIFF_DH) == m, qg, 0.0).astype(BF16)


def _diff_finish(o, sc_ref, subln, rows):
    outs = []
    for g in range(2):
        og = o[(2 * g) * rows:(2 * g + 1) * rows] - sc_ref[0] * o[(2 * g + 1) * rows:(2 * g + 2) * rows]
        outs.append(_rms(og) * subln * sc_ref[1])
    return jnp.concatenate(outs, axis=-1)


FAR_TILES = 4


Q_TILES = 2


def _causal_sweep(qi, visit):
    n_far = jnp.maximum(Q_TILES * qi - 1, 0)
    n_full = n_far // FAR_TILES
    rem = n_far - n_full * FAR_TILES

    def far(i, carry):
        visit(i * FAR_TILES, FAR_TILES, 0)
        return carry

    lax.fori_loop(0, n_full, far, 0)

    @pl.when(qi == 0)
    def _():
        visit(0, Q_TILES, Q_TILES)

    for r in sorted({(Q_TILES * i - 1) % FAR_TILES for i in range(1, FAR_TILES + 1)}):
        @pl.when(jnp.logical_and(qi >= 1, rem == r))
        def _(r=r):
            visit(n_full * FAR_TILES, r + Q_TILES + 1, Q_TILES + 1)


def _key_rows(tile0, ntiles):
    start = tile0 * CHUNK
    return pl.ds(start if isinstance(start, int) else pl.multiple_of(start, CHUNK), ntiles * CHUNK)


def _two_pass_attention(qi, score_fn, value_fn, s_ref, m_ref, l_ref, acc_ref, halves):
    m_ref[...] = jnp.full(m_ref.shape, NEG_INF, F32)

    def track_max(tile0, ntiles, near):
        for rows in halves:
            s = score_fn(rows, tile0, ntiles, near)
            mx = None
            for c in range(ntiles):
                sc = s[:, c * CHUNK:(c + 1) * CHUNK]
                s_ref[tile0 + c, rows, :] = sc
                mx = sc if mx is None else jnp.maximum(mx, sc)
            m_ref[rows, :] = jnp.maximum(m_ref[rows, :], mx)

    _causal_sweep(qi, track_max)
    m_ref[...] = jnp.broadcast_to(jnp.max(m_ref[...], axis=-1, keepdims=True), m_ref.shape)
    l_ref[...] = jnp.zeros(l_ref.shape, F32)
    acc_ref[...] = jnp.zeros(acc_ref.shape, F32)

    def accumulate(tile0, ntiles, near):
        v = value_fn(tile0, ntiles)
        for rows in halves:
            m = m_ref[rows, :]
            ps = [jnp.exp(s_ref[tile0 + c, rows, :] - m) for c in range(ntiles)]
            lsum = ps[0]
            for pc in ps[1:]:
                lsum = lsum + pc
            l_ref[rows, :] += lsum
            p = ps[0] if len(ps) == 1 else jnp.concatenate(ps, axis=-1)
            acc_ref[rows, :] += _dot(p.astype(BF16), v)

    _causal_sweep(qi, accumulate)
    return acc_ref[...] / jnp.sum(l_ref[...], axis=-1, keepdims=True)


def _diffp_body(tb_ref, sc_ref, q_ref, k_ref, v_ref, sub_ref, y_ref,
                qs_ref, s_ref, m_ref, l_ref, acc_ref, tn_ref):
    tq = Q_TILES * CHUNK
    near_w = (Q_TILES + 1) * CHUNK
    h = pl.program_id(1)
    qi = pl.program_id(2)

    @pl.when(qi == 0)
    def _():
        ri = lax.broadcasted_iota(jnp.int32, (tq, near_w), 0)
        ci = lax.broadcasted_iota(jnp.int32, (tq, near_w), 1)
        dist = ri + CHUNK - ci
        for g in range(2):
            bias = jnp.where(dist >= 0, _rel_bias_minus_far(jnp.maximum(dist, 0), tb_ref, h * 2 + g), NEG_INF)
            for m in range(2):
                r = (g * 2 + m) * tq
                tn_ref[r:r + tq, :] = bias

    _diff_stack_q(q_ref[...].astype(F32), qs_ref, tq)

    def score_fn(rows, tile0, ntiles, near):
        s = _dot_nt(qs_ref[rows, :], k_ref[_key_rows(tile0, ntiles), :].astype(BF16))
        if near:
            far_w = (ntiles - near) * CHUNK
            biased = s[:, far_w:] + tn_ref[rows, near_w - near * CHUNK:]
            s = biased if far_w == 0 else jnp.concatenate([s[:, :far_w], biased], axis=-1)
        return s

    def value_fn(tile0, ntiles):
        return v_ref[_key_rows(tile0, ntiles), :].astype(BF16)

    groups = tuple(slice(i * tq, (i + 1) * tq) for i in range(4))
    o = _two_pass_attention(qi, score_fn, value_fn, s_ref, m_ref, l_ref, acc_ref, groups)
    y_ref[...] = _diff_finish(o, sc_ref, sub_ref[...], tq)


def _diff_prompt(tb, sc, dq, dk, dv, subln):
    B, L, _ = dq.shape
    tq = Q_TILES * CHUNK
    assert CHUNK + 1 >= REL_FAR and L % tq == 0
    return pl.pallas_call(
        _diffp_body,
        grid=(B, DIFF_KVH, L // tq),
        in_specs=[_SMEM, _SMEM,
                  pl.BlockSpec((None, tq, 2 * LANES), lambda b, h, i: (b, i, h)),
                  pl.BlockSpec((None, L, LANES), lambda b, h, i: (b, 0, h)),
                  pl.BlockSpec((None, L, LANES), lambda b, h, i: (b, 0, h)),
                  _full((1, DIFF_DV))],
        out_specs=pl.BlockSpec((None, tq, 2 * LANES), lambda b, h, i: (b, i, h)),
        out_shape=jax.ShapeDtypeStruct((B, L, BRANCH_W), F32),
        scratch_shapes=[pltpu.VMEM((4 * tq, LANES), BF16), pltpu.VMEM((L // CHUNK, 4 * tq, CHUNK), F32),
                        pltpu.VMEM((4 * tq, CHUNK), F32), pltpu.VMEM((4 * tq, CHUNK), F32),
                        pltpu.VMEM((4 * tq, LANES), F32), pltpu.VMEM((4 * tq, (Q_TILES + 1) * CHUNK), F32)],
        compiler_params=pltpu.CompilerParams(dimension_semantics=("parallel", "parallel", "arbitrary"),
                                             vmem_limit_bytes=VMEM_LIMIT),
        name="diff_prompt",
    )(tb, sc, dq, dk, dv, subln)


MLA_SCALE = (MLA_NOPE + MLA_ROPE) ** -0.5


def _mla_q_prep(mq, qn, wq_ref, wuk_ref, cosf, sins, qs_ref, rows):
    cq = (_rms(mq) * qn).astype(BF16)
    qh = _dot(cq, wq_ref[...])
    for h in range(MLA_HEADS):
        nope = qh[:, h * 256:h * 256 + LANES].astype(BF16)
        rope = _rotate(qh[:, h * 256 + LANES:(h + 1) * 256], cosf, sins, MLA_ROPE // 2)
        qs_ref[h * rows:(h + 1) * rows, :LANES] = (_dot(nope, wuk_ref[h]) * MLA_SCALE).astype(BF16)
        qs_ref[h * rows:(h + 1) * rows, LANES:] = (rope * MLA_SCALE).astype(BF16)


def _mla_kv_prep(mkv, kvn, cosf, sins):
    c_kv = _rms(mkv[:, :LANES]) * kvn
    k_rope = _rotate(mkv[:, LANES:], cosf, sins, MLA_ROPE // 2)
    return c_kv, k_rope


def _mla_finish(o, wuv_ref, rows):
    o = o.astype(BF16)
    return jnp.concatenate([_dot(o[h * rows:(h + 1) * rows], wuv_ref[h]) for h in range(MLA_HEADS)], axis=-1)


def _mlap_body(mq_ref, mkv_ref, cq_ref, sq_ref, ck_ref, sk_ref, qn_ref, kvn_ref, wq_ref, wuk_ref, wuv_ref,
               y_ref, lat_ref, kr_ref, kcat_ref, qs_ref, s_ref, m_ref, l_ref, acc_ref):
    tq = Q_TILES * CHUNK
    qi = pl.program_id(1)

    @pl.when(qi == 0)
    def _():
        c_kv, k_rope = _mla_kv_prep(mkv_ref[...], kvn_ref[...], ck_ref[...], sk_ref[...])
        lat_ref[...] = c_kv
        kr_ref[...] = k_rope[:, :MLA_ROPE]
        kcat_ref[:, :LANES] = c_kv.astype(BF16)
        kcat_ref[:, LANES:] = k_rope.astype(BF16)

    _mla_q_prep(mq_ref[...].astype(F32), qn_ref[...], wq_ref, wuk_ref, cq_ref[...], sq_ref[...], qs_ref, tq)

    def score_fn(rows, tile0, ntiles, near):
        s = _dot_nt(qs_ref[rows, :], kcat_ref[_key_rows(tile0, ntiles), :])
        if near:
            ri = lax.broadcasted_iota(jnp.int32, (tq, ntiles * CHUNK), 0)
            ci = lax.broadcasted_iota(jnp.int32, (tq, ntiles * CHUNK), 1) - (ntiles - Q_TILES) * CHUNK
            s = jnp.where(ri >= ci, s, NEG_INF)
        return s

    def value_fn(tile0, ntiles):
        return kcat_ref[_key_rows(tile0, ntiles), :LANES]

    groups = tuple(slice(h * tq, (h + 1) * tq) for h in range(MLA_HEADS))
    o = _two_pass_attention(qi, score_fn, value_fn, s_ref, m_ref, l_ref, acc_ref, groups)
    y_ref[...] = _mla_finish(o, wuv_ref, tq)


def _mla_prompt(mq, mkv, cos, sin, w):
    B, L, _ = mq.shape
    tq = Q_TILES * CHUNK
    assert L % tq == 0
    return pl.pallas_call(
        _mlap_body,
        grid=(B, L // tq),
        in_specs=[pl.BlockSpec((None, tq, 256), lambda b, i: (b, i, 0)),
                  pl.BlockSpec((None, L, 256), lambda b, i: (b, 0, 0)),
                  pl.BlockSpec((tq, LANES), lambda b, i: (i, 0)), pl.BlockSpec((tq, LANES), lambda b, i: (i, 0)),
                  _full((L, LANES)), _full((L, LANES)),
                  _full((1, MLA_Q_RANK)), _full((1, MLA_KV_RANK)),
                  _full((MLA_Q_RANK, MLA_HEADS * 256)), _full((MLA_HEADS, LANES, LANES)),
                  _full((MLA_HEADS, LANES, LANES))],
        out_specs=[pl.BlockSpec((None, tq, BRANCH_W), lambda b, i: (b, i, 0)),
                   pl.BlockSpec((None, L, MLA_KV_RANK), lambda b, i: (b, 0, 0)),
                   pl.BlockSpec((None, L, MLA_ROPE), lambda b, i: (b, 0, 0))],
        out_shape=[jax.ShapeDtypeStruct((B, L, BRANCH_W), F32), jax.ShapeDtypeStruct((B, L, MLA_KV_RANK), F32),
                   jax.ShapeDtypeStruct((B, L, MLA_ROPE), F32)],
        scratch_shapes=[pltpu.VMEM((L, 2 * LANES), BF16), pltpu.VMEM((MLA_HEADS * tq, 2 * LANES), BF16),
                        pltpu.VMEM((L // CHUNK, MLA_HEADS * tq, CHUNK), F32),
                        pltpu.VMEM((MLA_HEADS * tq, CHUNK), F32), pltpu.VMEM((MLA_HEADS * tq, CHUNK), F32),
                        pltpu.VMEM((MLA_HEADS * tq, LANES), F32)],
        compiler_params=pltpu.CompilerParams(dimension_semantics=("parallel", "arbitrary"),
                                             vmem_limit_bytes=VMEM_LIMIT),
        name="mla_prompt",
    )(mq, mkv, cos, sin, cos, sin, w["mla_q_norm"], w["mla_kv_norm"], w["mla_wq"], w["mla_wuk"], w["mla_wuv"])


SROWS = 8


def _paged_fetch(pt_ref, caches, bufs, sems, layer, npg):
    b, j = pl.program_id(0), pl.program_id(1)
    nb, nj = pl.num_programs(0), pl.num_programs(1)
    step = b * nj + j
    slot = step % 2

    def copies(bb, jj, sl):
        return [pltpu.make_async_copy(cache.at[pt_ref[bb, jj * npg + i], layer], buf.at[sl, i], sem.at[sl])
                for cache, buf, sem in zip(caches, bufs, sems) for i in range(npg)]

    @pl.when(step == 0)
    def _():
        for cp in copies(b, j, slot):
            cp.start()

    @pl.when(step + 1 < nb * nj)
    def _():
        wrap = j + 1 == nj
        for cp in copies(jnp.where(wrap, b + 1, b), jnp.where(wrap, 0, j + 1), 1 - slot):
            cp.start()

    for cp in copies(b, j, slot):
        cp.wait()
    return slot


def _diffs_body(pt_ref, tb_ref, sc_ref, q_ref, kn_ref, vn_ref, sub_ref, ckt_hbm, cv_hbm, y_ref,
                qs_ref, m_ref, l_ref, acc_ref, kt_buf, v_buf, kt_sem, v_sem, *, npg, past_len, layer):
    j = pl.program_id(1)
    nj = pl.num_programs(1)
    width = npg * PAGE
    slot = _paged_fetch(pt_ref, (ckt_hbm, cv_hbm), (kt_buf, v_buf), (kt_sem, v_sem), layer, npg)

    @pl.when(j == 0)
    def _():
        for h in range(DIFF_KVH):
            _diff_stack_q(q_ref[:, h * 256:(h + 1) * 256].astype(F32), qs_ref.at[h], SROWS)
        _softmax_init(m_ref, l_ref, acc_ref)

    def visit(h, bias):
        kt = jnp.concatenate([kt_buf[slot, i, h * LANES:(h + 1) * LANES, :].astype(BF16) for i in range(npg)], axis=1)
        s = _dot(qs_ref[h], kt)
        if bias is not None:
            s = s + bias
        v = jnp.concatenate([v_buf[slot, i, pl.ds(h, PAGE, stride=DIFF_KVH), :].astype(BF16) for i in range(npg)],
                            axis=0)
        _softmax_step(s, v, m_ref.at[h], l_ref.at[h], acc_ref.at[h])

    @pl.when(j < nj - 1)
    def _():
        for h in range(DIFF_KVH):
            visit(h, None)

    @pl.when(j == nj - 1)
    def _():
        t_row = lax.broadcasted_iota(jnp.int32, (2 * SROWS, width), 0) % SROWS
        ci = lax.broadcasted_iota(jnp.int32, (2 * SROWS, width), 1)
        dist = past_len + t_row - ((nj - 1) * width + ci)
        t_new = lax.broadcasted_iota(jnp.int32, (2 * SROWS, PAGE), 0) % SROWS
        c_new = lax.broadcasted_iota(jnp.int32, (2 * SROWS, PAGE), 1)
        for h in range(DIFF_KVH):
            hs = slice(h * LANES, (h + 1) * LANES)
            visit(h, jnp.concatenate([_rel_bias_minus_far(dist, tb_ref, h * 2 + g) for g in range(2)], axis=0))
            kn = _pad_rows(kn_ref[:, hs], PAGE).astype(BF16)
            vn = _pad_rows(vn_ref[:, hs], PAGE).astype(BF16)
            bias_n = jnp.concatenate(
                [jnp.where(t_new >= c_new, _rel_bias_minus_far(jnp.maximum(t_new - c_new, 0), tb_ref, h * 2 + g), NEG_INF)
                 for g in range(2)], axis=0)
            _softmax_step(_dot_nt(qs_ref[h], kn) + bias_n, vn, m_ref.at[h], l_ref.at[h], acc_ref.at[h])
            y_ref[:, h * 256:(h + 1) * 256] = _diff_finish(acc_ref[h] / l_ref[h], sc_ref, sub_ref[...], SROWS)


def _pages_per_step(n_pages, want):
    return math.gcd(n_pages, want)


def _diff_sample(pt, tb, sc, dq, dk, dv, subln, cache_kt, cache_v, layer):
    B = dq.shape[0]
    n_pages = pt.shape[1]
    npg = _pages_per_step(n_pages, 16)
    R = 4 * SROWS
    tok = lambda width: pl.BlockSpec((None, SROWS, width), lambda b, j, pt: (b, 0, 0))
    hbm = pl.BlockSpec(memory_space=pl.ANY)
    page_buf = pltpu.VMEM((2, npg, 2 * PAGE, LANES), F32)
    grid_spec = pltpu.PrefetchScalarGridSpec(
        num_scalar_prefetch=1,
        grid=(B, n_pages // npg),
        in_specs=[_SMEM, _SMEM, tok(512), tok(256), tok(256), pl.BlockSpec((1, DIFF_DV), lambda b, j, pt: (0, 0)),
                  hbm, hbm],
        out_specs=tok(BRANCH_W),
        scratch_shapes=[pltpu.VMEM((DIFF_KVH, R, LANES), BF16), pltpu.VMEM((DIFF_KVH, R, 1), F32),
                        pltpu.VMEM((DIFF_KVH, R, 1), F32), pltpu.VMEM((DIFF_KVH, R, LANES), F32),
                        page_buf, page_buf, pltpu.SemaphoreType.DMA((2,)), pltpu.SemaphoreType.DMA((2,))],
    )
    return pl.pallas_call(
        functools.partial(_diffs_body, npg=npg, past_len=n_pages * PAGE, layer=layer),
        grid_spec=grid_spec,
        out_shape=jax.ShapeDtypeStruct((B, SROWS, BRANCH_W), F32),
        compiler_params=pltpu.CompilerParams(dimension_semantics=("arbitrary", "arbitrary"),
                                             vmem_limit_bytes=VMEM_LIMIT),
        name="diff_sample",
    )(pt, tb, sc, dq, dk, dv, subln, cache_kt, cache_v)


def _mlas_body(pt_ref, mq_ref, mkv_ref, cos_ref, sin_ref, qn_ref, kvn_ref, wq_ref, wuk_ref, wuv_ref, cc_hbm, crt_hbm,
               y_ref, lat_ref, kr_ref, qs_ref, kn_ref, m_ref, l_ref, acc_ref, c_buf, rt_buf, c_sem, rt_sem,
               *, npg, layer):
    j = pl.program_id(1)
    nj = pl.num_programs(1)
    R = MLA_HEADS * SROWS
    slot = _paged_fetch(pt_ref, (cc_hbm, crt_hbm), (c_buf, rt_buf), (c_sem, rt_sem), layer, npg)

    @pl.when(j == 0)
    def _():
        c_kv, k_rope = _mla_kv_prep(mkv_ref[...], kvn_ref[...], cos_ref[...], sin_ref[...])
        lat_ref[...] = c_kv
        kr_ref[...] = k_rope[:, :MLA_ROPE]
        kn_ref[...] = jnp.zeros(kn_ref.shape, BF16)
        kn_ref[0:SROWS, :LANES] = c_kv.astype(BF16)
        kn_ref[0:SROWS, LANES:] = k_rope.astype(BF16)
        _mla_q_prep(mq_ref[...].astype(F32), qn_ref[...], wq_ref, wuk_ref, cos_ref[...], sin_ref[...], qs_ref, SROWS)
        _softmax_init(m_ref, l_ref, acc_ref)

    lat = jnp.concatenate([c_buf[slot, i].astype(BF16) for i in range(npg)], axis=0)
    krt = jnp.concatenate([rt_buf[slot, i].astype(BF16) for i in range(npg)], axis=1)
    s = _dot_nt(qs_ref[:, :LANES], lat) + _dot(qs_ref[:, LANES:LANES + MLA_ROPE], krt)
    _softmax_step(s, lat, m_ref, l_ref, acc_ref)

    @pl.when(j == nj - 1)
    def _():
        kc = kn_ref[...]
        t_new = lax.broadcasted_iota(jnp.int32, (R, PAGE), 0) % SROWS
        c_new = lax.broadcasted_iota(jnp.int32, (R, PAGE), 1)
        sn = jnp.where(t_new >= c_new, _dot_nt(qs_ref[...], kc), NEG_INF)
        _softmax_step(sn, kc[:, :LANES], m_ref, l_ref, acc_ref)
        y_ref[...] = _mla_finish(acc_ref[...] / l_ref[...], wuv_ref, SROWS)


def _mla_sample(pt, mq, mkv, cos, sin, w, cache_c, cache_rt, layer):
    B = mq.shape[0]
    n_pages = pt.shape[1]
    npg = _pages_per_step(n_pages, 32)
    R = MLA_HEADS * SROWS
    tok = lambda width: pl.BlockSpec((None, SROWS, width), lambda b, j, pt: (b, 0, 0))
    full = lambda *shape: pl.BlockSpec(shape, lambda b, j, pt: (0,) * len(shape))
    hbm = pl.BlockSpec(memory_space=pl.ANY)
    grid_spec = pltpu.PrefetchScalarGridSpec(
        num_scalar_prefetch=1,
        grid=(B, n_pages // npg),
        in_specs=[tok(256), tok(256), full(SROWS, LANES), full(SROWS, LANES), full(1, MLA_Q_RANK), full(1, MLA_KV_RANK),
                  full(MLA_Q_RANK, MLA_HEADS * 256), full(MLA_HEADS, LANES, LANES), full(MLA_HEADS, LANES, LANES),
                  hbm, hbm],
        out_specs=[tok(BRANCH_W), tok(MLA_KV_RANK), tok(MLA_ROPE)],
        scratch_shapes=[pltpu.VMEM((R, 2 * LANES), BF16), pltpu.VMEM((PAGE, 2 * LANES), BF16),
                        pltpu.VMEM((R, 1), F32), pltpu.VMEM((R, 1), F32), pltpu.VMEM((R, LANES), F32),
                        pltpu.VMEM((2, npg, PAGE, LANES), F32), pltpu.VMEM((2, npg, MLA_ROPE, LANES), F32),
                        pltpu.SemaphoreType.DMA((2,)), pltpu.SemaphoreType.DMA((2,))],
    )
    return pl.pallas_call(
        functools.partial(_mlas_body, npg=npg, layer=layer),
        grid_spec=grid_spec,
        out_shape=[jax.ShapeDtypeStruct((B, SROWS, BRANCH_W), F32), jax.ShapeDtypeStruct((B, SROWS, MLA_KV_RANK), F32),
                   jax.ShapeDtypeStruct((B, SROWS, MLA_ROPE), F32)],
        compiler_params=pltpu.CompilerParams(dimension_semantics=("arbitrary", "arbitrary"),
                                             vmem_limit_bytes=VMEM_LIMIT),
        name="mla_sample",
    )(pt, mq, mkv, cos, sin, w["mla_q_norm"], w["mla_kv_norm"], w["mla_wq"], w["mla_wuk"], w["mla_wuv"],
      cache_c, cache_rt)


def _merge_body(x_ref, y0_ref, y1_ref, y2_ref, y3_ref, gt_ref, wb_ref, wo_ref, o_ref):
    mix = None
    for i, y_ref in enumerate((y0_ref, y1_ref, y2_ref, y3_ref)):
        proj = _dot(y_ref[...].astype(BF16), wb_ref[i])
        gate = gt_ref[:, i * D_MODEL:(i + 1) * D_MODEL].astype(F32)
        term = proj / (1.0 + jnp.exp(-gate))
        mix = term if mix is None else mix + term
    o_ref[...] = x_ref[...] + _dot(mix.astype(BF16), wo_ref[...])


def _merge(x, ys, gates, wb, wo):
    T = x.shape[0]
    tm = min(512, T)
    row = lambda width: pl.BlockSpec((tm, width), lambda i: (i, 0))
    return pl.pallas_call(
        _merge_body,
        grid=(T // tm,),
        in_specs=[row(D_MODEL)] + [row(BRANCH_W)] * 4 + [row(N_BRANCH * D_MODEL),
                                                       _resident((N_BRANCH, BRANCH_W, D_MODEL)),
                                                       _resident((D_MODEL, D_MODEL))],
        out_specs=row(D_MODEL),
        out_shape=jax.ShapeDtypeStruct((T, D_MODEL), F32),
        compiler_params=pltpu.CompilerParams(dimension_semantics=("parallel",), vmem_limit_bytes=VMEM_LIMIT),
        name="merge",
    )(x, *ys, gates, wb, wo)


def _rope_tables(pos, half):
    inv_freq = ROPE_BASE ** (-jnp.arange(half, dtype=F32) / half)
    ang = pos.astype(F32)[:, None] * inv_freq[None, :]
    cos, sin = jnp.cos(ang), jnp.sin(ang)
    return jnp.concatenate([cos, cos], axis=-1), jnp.concatenate([-sin, sin], axis=-1)


def _layer_weights(l, p):
    w = {}
    f = lambda a: a.astype(BF16)
    for n in ("ffn1_w1", "ffn1_w3", "ffn1_w2", "ffn2_w1", "ffn2_w3", "ffn2_w2", "w_out"):
        w[n] = f(p[n][l])
    w["w_branch"] = f(p["w_branch"][l])
    for n in ("norm_ffn1", "norm_mix", "norm_ffn2", "ssm_conv_b", "ssm_norm", "ret_norm", "diff_subln",
              "mla_q_norm", "mla_kv_norm"):
        w[n] = p[n][l][None, :]
    offs = np.cumsum((0,) + IN_SIZES)
    piece = lambda i: p["w_in"][l][:, offs[i]:offs[i + 1]]
    padc = lambda a, width: jnp.pad(a, ((0, 0), (0, width - a.shape[1])))
    cols = [piece(0), piece(1), padc(piece(2), LANES), piece(3), piece(4), piece(5), piece(6), piece(7), piece(8),
            piece(9), piece(10), padc(piece(11), 2 * LANES), piece(12)]
    w["w_in"] = f(jnp.concatenate(cols, axis=1))
    w["conv_w"] = p["ssm_conv_w"][l]
    w["conv_b"] = w["ssm_conv_b"]
    w["dt_bias"] = padc(p["ssm_dt_bias"][l][None, :], LANES)
    w["dt_biasT"] = p["ssm_dt_bias"][l][:, None]
    w["a_log"] = padc(p["ssm_a_log"][l][None, :], LANES)
    w["a_logT"] = p["ssm_a_log"][l][:, None]
    w["d_exp"] = jnp.repeat(p["ssm_d"][l], SSM_HEAD_DIM)[None, :]
    expand = np.zeros((LANES, BRANCH_W), np.float32)
    for h in range(SSM_HEADS):
        expand[h, h * SSM_HEAD_DIM:(h + 1) * SSM_HEAD_DIM] = 1.0
    w["expand"] = jnp.asarray(expand)
    wq = p["mla_w_q_up"][l]
    zq = jnp.zeros((MLA_Q_RANK, MLA_HEADS, LANES - MLA_NOPE), F32)
    zr = jnp.zeros((MLA_Q_RANK, MLA_HEADS, LANES - MLA_ROPE), F32)
    w["mla_wq"] = f(jnp.concatenate([wq[..., :MLA_NOPE], zq, wq[..., MLA_NOPE:], zr], axis=-1)
                    .reshape(MLA_Q_RANK, MLA_HEADS * 256))
    wuk = jnp.transpose(p["mla_w_uk"][l], (1, 2, 0))
    w["mla_wuk"] = f(jnp.pad(wuk, ((0, 0), (0, LANES - MLA_NOPE), (0, 0))))
    w["mla_wuv"] = f(jnp.transpose(p["mla_w_uv"][l], (1, 0, 2)))
    lam_init = 0.8 - 0.6 * math.exp(-0.3 * l)
    lw = p["diff_lambda"][l].astype(F32)
    lam = jnp.exp(jnp.sum(lw[0] * lw[1])) - jnp.exp(jnp.sum(lw[2] * lw[3])) + lam_init
    w["diff_sc"] = jnp.stack([lam, jnp.asarray(1.0 - lam_init, F32)]).astype(F32)
    return w


def _ssm_state_to_pairs(s):
    B = s.shape[0]
    return s.reshape(B, 4, 2, SSM_STATE, SSM_HEAD_DIM).transpose(0, 1, 3, 2, 4).reshape(B, 4, SSM_STATE, 2 * SSM_HEAD_DIM)


def _ssm_state_from_pairs(s):
    B = s.shape[0]
    return s.reshape(B, 4, SSM_STATE, 2, SSM_HEAD_DIM).transpose(0, 1, 3, 2, 4).reshape(B, SSM_HEADS, SSM_STATE, SSM_HEAD_DIM)


def _trunk(x, w_layers, rel_bias, final_norm, rows, valid, pos, conv0, ssm0, ret0, paged):
    B, L, _ = x.shape
    T = B * L
    depth = len(w_layers)
    xt = x.reshape(T, D_MODEL)
    ret_cos, ret_sin = _rope_tables(pos, RET_QK // 2)
    ret_cos, ret_sin = jnp.tile(ret_cos, (1, 4)), jnp.tile(ret_sin, (1, 4))
    mc, ms = _rope_tables(pos, MLA_ROPE // 2)
    mla_cos = jnp.pad(mc, ((0, 0), (0, LANES - MLA_ROPE)))
    mla_sin = jnp.pad(ms, ((0, 0), (0, LANES - MLA_ROPE)))
    nc = L // rows
    outs = []
    for l in range(depth):
        w = w_layers[l]
        xt = _ffn(xt, w["norm_ffn1"], w["ffn1_w1"], w["ffn1_w3"], w["ffn1_w2"])
        pr = _inproj(xt, w["norm_mix"], w["w_in"])
        r3 = lambda a: a.reshape(B, L, a.shape[-1])
        dt = r3(pr["dt"])
        dtT = jnp.transpose(dt[:, :, :SSM_HEADS].reshape(B, nc, rows, SSM_HEADS), (0, 3, 1, 2))
        dtT = jnp.pad(dtT, ((0, 0), (0, 0), (0, 0), (0, CHUNK - rows))).reshape(B, SSM_HEADS, nc * CHUNK)
        dtp = dt if rows == CHUNK else jnp.pad(dt, ((0, 0), (0, CHUNK - rows), (0, 0)))
        y_ssm, conv_new, ssm_new = _ssd(r3(pr["xbc"]), r3(pr["z"]), dtp, dtT, conv0[l], ssm0[l], w, rows, valid)
        y_ret, ret_new = _ret(r3(pr["rq"]), r3(pr["rk"]), r3(pr["rv"]), r3(pr["rg"]), ret_cos, ret_sin, ret0[l],
                              w["ret_norm"], rows, valid)
        dk, dv = r3(pr["dk"]), r3(pr["dv"])
        if paged is None:
            y_diff = _diff_prompt(rel_bias, w["diff_sc"], r3(pr["dq"]), dk, dv, w["diff_subln"])
            y_mla, lat, kr = _mla_prompt(r3(pr["mq"]), r3(pr["mkv"]), mla_cos, mla_sin, w)
        else:
            pt, ck, cv, cc, cr = paged
            y_diff = _diff_sample(pt, rel_bias, w["diff_sc"], r3(pr["dq"]), dk, dv, w["diff_subln"], ck, cv, l)
            y_mla, lat, kr = _mla_sample(pt, r3(pr["mq"]), r3(pr["mkv"]), mla_cos, mla_sin, w, cc, cr, l)
        ys = [a.reshape(T, BRANCH_W) for a in (y_ssm, y_ret, y_diff, y_mla)]
        xt = _merge(xt, ys, pr["gates"], w["w_branch"], w["w_out"])
        fg = final_norm if l == depth - 1 else None
        xt = _ffn(xt, w["norm_ffn2"], w["ffn2_w1"], w["ffn2_w3"], w["ffn2_w2"], fg)
        outs.append((dk, dv, lat, kr, conv_new, ssm_new, ret_new))
    return xt.reshape(B, L, D_MODEL), outs


def kernel(x_prompt, x_sample, cache_diff_k, cache_diff_v, cache_mla_latent, cache_mla_krope, state_ssm_conv, state_ssm, state_retention, page_table, norm_ffn1, ffn1_w1, ffn1_w3, ffn1_w2, norm_mix, w_in, ssm_conv_w, ssm_conv_b, ssm_dt_bias, ssm_a_log, ssm_d, ssm_norm, ret_norm, diff_lambda, diff_subln, mla_q_norm, mla_w_q_up, mla_kv_norm, mla_w_uk, mla_w_uv, rel_bias, w_branch, w_out, norm_ffn2, ffn2_w1, ffn2_w3, ffn2_w2, final_norm):
    p = dict(norm_ffn1=norm_ffn1, ffn1_w1=ffn1_w1, ffn1_w3=ffn1_w3, ffn1_w2=ffn1_w2, norm_mix=norm_mix, w_in=w_in,
             ssm_conv_w=ssm_conv_w, ssm_conv_b=ssm_conv_b, ssm_dt_bias=ssm_dt_bias, ssm_a_log=ssm_a_log, ssm_d=ssm_d,
             ssm_norm=ssm_norm, ret_norm=ret_norm, diff_lambda=diff_lambda, diff_subln=diff_subln,
             mla_q_norm=mla_q_norm, mla_w_q_up=mla_w_q_up, mla_kv_norm=mla_kv_norm, mla_w_uk=mla_w_uk,
             mla_w_uv=mla_w_uv, w_branch=w_branch, w_out=w_out, norm_ffn2=norm_ffn2, ffn2_w1=ffn2_w1,
             ffn2_w3=ffn2_w3, ffn2_w2=ffn2_w2)
    depth = w_in.shape[0]
    w_layers = [_layer_weights(l, p) for l in range(depth)]
    fnorm = final_norm[None, :]
    rel = rel_bias.astype(F32)

    Bp, Lp, _ = x_prompt.shape
    zc = jnp.zeros((Bp, CONV_W - 1, SSM_CONV_DIM), F32)
    zs = jnp.zeros((Bp, 4, LANES, LANES), F32)
    zr = jnp.zeros((Bp, 2 * LANES, LANES), F32)
    y_prompt, rows_p = _trunk(x_prompt, w_layers, rel, fnorm, CHUNK, CHUNK, jnp.arange(Lp, dtype=jnp.int32),
                              [zc] * depth, [zs] * depth, [zr] * depth, None)

    Bs, Ls, _ = x_sample.shape
    n_pages = page_table.shape[1]
    past_len = n_pages * PAGE
    xs = jnp.pad(x_sample, ((0, 0), (0, SROWS - Ls), (0, 0)))
    pos_s = past_len + jnp.arange(SROWS, dtype=jnp.int32)
    n_pool = cache_diff_k.shape[0]
    paged = (page_table.astype(jnp.int32),
             jnp.transpose(cache_diff_k, (0, 1, 3, 4, 5, 2)).reshape(n_pool, depth, 2 * PAGE, LANES),
             cache_diff_v.reshape(n_pool, depth, 2 * PAGE, LANES),
             cache_mla_latent, jnp.transpose(cache_mla_krope, (0, 1, 3, 2)))
    conv_s = [state_ssm_conv[:, l] for l in range(depth)]
    ssm_s = [_ssm_state_to_pairs(state_ssm[:, l]) for l in range(depth)]
    ret_s = [state_retention[:, l].reshape(Bs, 2 * LANES, LANES) for l in range(depth)]
    y_sample, rows_s = _trunk(xs, w_layers, rel, fnorm, SROWS, Ls, pos_s, conv_s, ssm_s, ret_s, paged)

    def collect(rows, B, L, keep):
        st = lambda i: jnp.stack([r[i] for r in rows], axis=1)
        dk = st(0)[:, :, :keep].reshape(B, depth, keep, DIFF_KVH, 2, DIFF_DH)
        dv = st(1)[:, :, :keep].reshape(B, depth, keep, DIFF_KVH, DIFF_DV)
        lat = st(2)[:, :, :keep]
        kr = st(3)[:, :, :keep]
        conv = st(4)
        ssm = jnp.stack([_ssm_state_from_pairs(r[5]) for r in rows], axis=1)
        ret = st(6).reshape(B, depth, RET_HEADS, RET_QK, RET_V)
        return dk, dv, lat, kr, conv, ssm, ret

    return (y_prompt, y_sample[:, :Ls]) + collect(rows_p, Bp, Lp, Lp) + collect(rows_s, Bs, SROWS, Ls)
```

```python
import functools
import math

import numpy as np
import jax
import jax.numpy as jnp
from jax import lax
from jax.experimental import pallas as pl
from jax.experimental.pallas import tpu as pltpu

F32 = jnp.float32
BF16 = jnp.bfloat16
NEG_INF = float("-inf")

D_MODEL = 1024
D_FF = 2816
BRANCH_W = 512
N_BRANCH = 4
SSM_HEADS = 8
SSM_HEAD_DIM = 64
SSM_STATE = 128
CONV_W = 4
SSM_CONV_DIM = 1024
RET_HEADS = 4
RET_QK = 64
RET_V = 128
DIFF_KVH = 2
DIFF_DH = 64
DIFF_DV = 128
MLA_HEADS = 4
MLA_Q_RANK = 256
MLA_KV_RANK = 128
MLA_NOPE = 64
MLA_ROPE = 32
PAGE = 128
REL_BUCKETS = 32
REL_MAX_EXACT = 16
REL_MAX_DIST = 128
ROPE_BASE = 10000.0
NORM_EPS = 1e-6
IN_SIZES = (512, 1024, 8, 256, 256, 512, 512, 512, 256, 256, 256, 160, 4096)

LANES = 128
MXU_WIDTH = 256
CHUNK = 128
VMEM_LIMIT = 56 * 1024 * 1024


def _rel_lower_bounds():
    lb = list(range(REL_MAX_EXACT))
    d = np.arange(REL_MAX_EXACT, 4 * REL_MAX_DIST).astype(np.float32)
    large = REL_MAX_EXACT + (np.log(d / np.float32(REL_MAX_EXACT)) / np.float32(math.log(REL_MAX_DIST / REL_MAX_EXACT))
                             * np.float32(REL_BUCKETS - REL_MAX_EXACT)).astype(np.int32)
    large = np.minimum(large, REL_BUCKETS - 1)
    for bk in range(REL_MAX_EXACT, REL_BUCKETS):
        lb.append(int(REL_MAX_EXACT + np.argmax(large >= bk)))
    return tuple(lb)


REL_LB = _rel_lower_bounds()
REL_FAR = REL_LB[-1]
RET_LOG_GAMMA = tuple(math.log1p(-(2.0 ** (-5.0 - h))) for h in range(RET_HEADS))


def _dot(a, b, precision=None):
    return jnp.dot(a, b, preferred_element_type=F32, precision=precision)


def _dot_nt(a, b):
    return lax.dot_general(a, b, (((1,), (1,)), ((), ())), preferred_element_type=F32)


def _dot_tn(a, b):
    return lax.dot_general(a, b, (((0,), (0,)), ((), ())), preferred_element_type=F32)


def _rms(x):
    return x * lax.rsqrt(jnp.mean(x * x, axis=-1, keepdims=True) + NORM_EPS)


def _silu(x):
    return x / (1.0 + jnp.exp(-x))


def _softplus(x):
    return jnp.maximum(x, 0.0) + jnp.log1p(jnp.exp(-jnp.abs(x)))


def _dot_select(x, sel, sel_first=False):
    hi = x.astype(BF16)
    r1 = x - hi.astype(F32)
    mid = r1.astype(BF16)
    lo = (r1 - mid.astype(F32)).astype(BF16)
    if sel_first:
        return _dot(sel, hi) + _dot(sel, mid) + _dot(sel, lo)
    return _dot(hi, sel) + _dot(mid, sel) + _dot(lo, sel)


def _pad_rows(x, rows):
    if x.shape[0] == rows:
        return x
    return jnp.concatenate([x, jnp.zeros((rows - x.shape[0],) + x.shape[1:], x.dtype)], axis=0)


def _rotate(x, cosf, sins, half):
    outs = []
    for s in range(x.shape[-1] // LANES):
        xs = x[:, s * LANES:(s + 1) * LANES]
        lane = lax.broadcasted_iota(jnp.int32, xs.shape, 1)
        first = (lane % (2 * half)) < half
        partner = jnp.where(first, pltpu.roll(xs, LANES - half, axis=1), pltpu.roll(xs, half, axis=1))
        outs.append(partner)
    partner = outs[0] if len(outs) == 1 else jnp.concatenate(outs, axis=-1)
    return x * cosf + partner * sins


def _full(shape):
    return pl.BlockSpec(shape, lambda *_: (0,) * len(shape))


def _resident(shape):
    return pl.BlockSpec(shape, lambda *_: (0,) * len(shape), pipeline_mode=pl.Buffered(1))


_SMEM = pl.BlockSpec(memory_space=pltpu.SMEM)


def _ffn_body(*refs, fc, final):
    if final:
        x_ref, g_ref, w1_ref, w3_ref, w2_ref, fg_ref, o_ref = refs
    else:
        x_ref, g_ref, w1_ref, w3_ref, w2_ref, o_ref = refs
    x = x_ref[...]
    xn = (_rms(x) * g_ref[...]).astype(BF16)
    acc = x
    for c in range(D_FF // fc):
        sl = slice(c * fc, (c + 1) * fc)
        h1 = _dot(xn, w1_ref[:, sl])
        h3 = _dot(xn, w3_ref[:, sl])
        a = (_silu(h1) * h3).astype(BF16)
        acc = acc + 0.5 * _dot(a, w2_ref[sl, :])
    if final:
        acc = _rms(acc) * fg_ref[...]
    o_ref[...] = acc


def _ffn(x, g, w1, w3, w2, final_g=None):
    T = x.shape[0]
    tm = min(1024, T)
    final = final_g is not None
    in_specs = [pl.BlockSpec((tm, D_MODEL), lambda i: (i, 0)), _full((1, D_MODEL)),
                _resident((D_MODEL, D_FF)), _resident((D_MODEL, D_FF)), _resident((D_FF, D_MODEL))]
    args = [x, g, w1, w3, w2]
    if final:
        in_specs.append(_full((1, D_MODEL)))
        args.append(final_g)
    return pl.pallas_call(
        functools.partial(_ffn_body, fc=MXU_WIDTH, final=final),
        grid=(T // tm,),
        in_specs=in_specs,
        out_specs=pl.BlockSpec((tm, D_MODEL), lambda i: (i, 0)),
        out_shape=jax.ShapeDtypeStruct((T, D_MODEL), F32),
        compiler_params=pltpu.CompilerParams(dimension_semantics=("parallel",), vmem_limit_bytes=VMEM_LIMIT),
        name="ffn",
    )(*args)


PROJ_OUT = (("z", 512, BF16), ("xbc", 1024, F32), ("dt", 128, F32), ("rq", 256, BF16), ("rk", 256, BF16),
            ("rv", 512, BF16), ("rg", 512, BF16), ("dq", 512, BF16), ("dk", 256, F32), ("dv", 256, F32),
            ("mq", 256, BF16), ("mkv", 256, F32), ("gates", 4096, BF16))
PROJ_W = sum(w for _, w, _ in PROJ_OUT)


def _inproj_body(x_ref, g_ref, w_ref, *o_refs):
    xn = (_rms(x_ref[...]) * g_ref[...]).astype(BF16)
    off = 0
    for (_, w, dt), o_ref in zip(PROJ_OUT, o_refs):
        o_ref[...] = _dot(xn, w_ref[:, off:off + w]).astype(dt)
        off += w


def _inproj(x, g, w):
    T = x.shape[0]
    tm = min(512, T)
    outs = pl.pallas_call(
        _inproj_body,
        grid=(T // tm,),
        in_specs=[pl.BlockSpec((tm, D_MODEL), lambda i: (i, 0)), _full((1, D_MODEL)), _resident((D_MODEL, PROJ_W))],
        out_specs=[pl.BlockSpec((tm, w_), lambda i: (i, 0)) for _, w_, _ in PROJ_OUT],
        out_shape=[jax.ShapeDtypeStruct((T, w_), dt) for _, w_, dt in PROJ_OUT],
        compiler_params=pltpu.CompilerParams(dimension_semantics=("parallel",), vmem_limit_bytes=VMEM_LIMIT),
        name="inproj",
    )(x, g, w)
    return {n: o for (n, _, _), o in zip(PROJ_OUT, outs)}


SEQ_PER_STEP = 2


def _per_sequence(body, init, n_seq_in, n_shared, *refs, **static):
    seq_in, shared, seq_rest = refs[:n_seq_in], refs[n_seq_in:n_seq_in + n_shared], refs[n_seq_in + n_shared:]
    views = [[r.at[i] for r in seq_in] + list(shared) + [r.at[i] for r in seq_rest] for i in range(SEQ_PER_STEP)]

    @pl.when(pl.program_id(1) == 0)
    def _():
        for v in views:
            init(*v)

    for v in views:
        body(*v, **static)


def _ssd_init(xbc_ref, z_ref, dt_ref, dtT_ref, cb0_ref, s0_ref, cw_ref, cbias_ref, dtb_ref, dtbT_ref,
              alog_ref, alogT_ref, dexp_ref, nw_ref, e_ref, y_ref, cnew_ref, snew_ref, xp_ref, s_ref):
    xp_ref[0:8, :] = jnp.zeros((8, SSM_CONV_DIM), F32)
    xp_ref[5:8, :] = cb0_ref[...]
    s_ref[...] = s0_ref[...]


def _ssd_chunk(xbc_ref, z_ref, dt_ref, dtT_ref, cb0_ref, s0_ref, cw_ref, cbias_ref, dtb_ref, dtbT_ref,
               alog_ref, alogT_ref, dexp_ref, nw_ref, e_ref,
               y_ref, cnew_ref, snew_ref, xp_ref, s_ref, *, rows, valid):
    Q = CHUNK
    xp_ref[8:8 + Q, :] = _pad_rows(xbc_ref[...], Q)
    conv = cbias_ref[...] + xp_ref[5:5 + Q, :] * cw_ref[0:1, :]
    for w in range(1, CONV_W):
        conv = conv + xp_ref[5 + w:5 + w + Q, :] * cw_ref[w:w + 1, :]
    u = _silu(conv)
    last3 = xp_ref[8 + valid - 3:8 + valid, :]
    xp_ref[5:8, :] = last3
    cnew_ref[...] = last3

    row = lax.broadcasted_iota(jnp.int32, (Q, Q), 0)
    col = lax.broadcasted_iota(jnp.int32, (Q, Q), 1)
    causal = row >= col

    dtv = _softplus(dt_ref[...] + dtb_ref[...])
    dtv = jnp.where(lax.broadcasted_iota(jnp.int32, dtv.shape, 0) < valid, dtv, 0.0)
    la = -jnp.exp(alog_ref[...]) * dtv
    cum = _dot_select(la, causal.astype(BF16), sel_first=True)
    dtT = _softplus(dtT_ref[...] + dtbT_ref[...])
    dtT = jnp.where(lax.broadcasted_iota(jnp.int32, dtT.shape, 1) < valid, dtT, 0.0)
    laT = _pad_rows(-jnp.exp(alogT_ref[...]) * dtT, 2 * SSM_HEADS)
    cumT = _dot_select(laT, (row <= col).astype(BF16))
    both = _dot_select(jnp.concatenate([cum, dtv], axis=0), e_ref[...])
    cum_exp, dt_exp = both[:Q], both[Q:]
    cl_exp = cum_exp[Q - 1:Q, :]
    ecum = jnp.exp(cum_exp)
    tail = jnp.exp(cl_exp - cum_exp)
    sdecay = jnp.exp(cl_exp)

    xs = u[:, :BRANCH_W]
    v = xs * dt_exp
    vb = v.astype(BF16)
    vt = (v * tail).astype(BF16)
    lane = lax.broadcasted_iota(jnp.int32, (Q, LANES), 1)
    ys = []
    for g in range(2):
        bg = u[:, 512 + g * 128:512 + (g + 1) * 128].astype(BF16)
        cg = u[:, 768 + g * 128:768 + (g + 1) * 128].astype(BF16)
        gmat = _dot_nt(cg, bg)
        for pp in range(2):
            p = g * 2 + pp
            sl = slice(p * LANES, (p + 1) * LANES)
            s_pair = s_ref[p]
            inter = _dot(cg, s_pair.astype(BF16))
            yh = []
            for hh in range(2):
                h = 2 * p + hh
                seg = cum[:, h:h + 1] - cumT[h:h + 1, :]
                dec = jnp.exp(jnp.where(causal, seg, NEG_INF))
                yh.append(_dot((gmat * dec).astype(BF16), vb[:, sl]))
            ys.append(jnp.where(lane < SSM_HEAD_DIM, yh[0], yh[1]) + inter * ecum[:, sl])
            s_ref[p] = s_pair * sdecay[:, sl] + _dot_tn(bg, vt[:, sl])
    y = jnp.concatenate(ys, axis=-1) + xs * dexp_ref[...]
    y = y * _silu(_pad_rows(z_ref[...].astype(F32), Q))
    half = BRANCH_W // 2
    y = jnp.concatenate([_rms(y[:, :half]), _rms(y[:, half:])], axis=-1) * nw_ref[...]
    y_ref[...] = y[:rows].astype(y_ref.dtype)
    snew_ref[...] = s_ref[...]


def _ssd(xbc, z, dt, dtT, cb0, s0, w, rows, valid):
    B, L, _ = xbc.shape
    nc = L // rows
    ns = SEQ_PER_STEP
    assert B % ns == 0
    seq = lambda width: pl.BlockSpec((ns, rows, width), lambda b, c: (b, c, 0))
    per_b = lambda *shape: pl.BlockSpec((ns,) + shape, lambda b, c: (b,) + (0,) * len(shape))
    return pl.pallas_call(
        functools.partial(_per_sequence, _ssd_chunk, _ssd_init, 6, 9, rows=rows, valid=valid),
        grid=(B // ns, nc),
        in_specs=[seq(SSM_CONV_DIM), seq(BRANCH_W), pl.BlockSpec((ns, CHUNK, LANES), lambda b, c: (b, c, 0)),
                  pl.BlockSpec((ns, SSM_HEADS, CHUNK), lambda b, c: (b, 0, c)),
                  per_b(CONV_W - 1, SSM_CONV_DIM), per_b(4, LANES, LANES),
                  _full((CONV_W, SSM_CONV_DIM)), _full((1, SSM_CONV_DIM)), _full((1, LANES)), _full((SSM_HEADS, 1)),
                  _full((1, LANES)), _full((SSM_HEADS, 1)), _full((1, BRANCH_W)), _full((1, BRANCH_W)),
                  _full((LANES, BRANCH_W))],
        out_specs=[seq(BRANCH_W), per_b(CONV_W - 1, SSM_CONV_DIM), per_b(4, LANES, LANES)],
        out_shape=[jax.ShapeDtypeStruct((B, L, BRANCH_W), BF16),
                   jax.ShapeDtypeStruct((B, CONV_W - 1, SSM_CONV_DIM), F32),
                   jax.ShapeDtypeStruct((B, 4, LANES, LANES), F32)],
        scratch_shapes=[pltpu.VMEM((ns, 8 + CHUNK, SSM_CONV_DIM), F32), pltpu.VMEM((ns, 4, LANES, LANES), F32)],
        compiler_params=pltpu.CompilerParams(dimension_semantics=("parallel", "arbitrary")),
        name="ssd",
    )(xbc, z, dt, dtT, cb0, s0, w["conv_w"], w["conv_b"], w["dt_bias"], w["dt_biasT"], w["a_log"], w["a_logT"],
      w["d_exp"], w["ssm_norm"], w["expand"])


def _ret_init(q_ref, k_ref, v_ref, g_ref, s0_ref, cos_ref, sin_ref, nw_ref, y_ref, snew_ref, s_ref):
    s_ref[...] = s0_ref[...]


def _ret_chunk(q_ref, k_ref, v_ref, g_ref, s0_ref, cos_ref, sin_ref, nw_ref, y_ref, snew_ref, s_ref, *, rows, valid):
    Q = CHUNK
    cosf = _pad_rows(cos_ref[...], Q)
    sins = _pad_rows(sin_ref[...], Q)
    q = _rotate(_pad_rows(q_ref[...].astype(F32), Q), cosf, sins, RET_QK // 2)
    k = _rotate(_pad_rows(k_ref[...].astype(F32), Q), cosf, sins, RET_QK // 2) * (RET_QK ** -0.5)
    v = _pad_rows(v_ref[...].astype(F32), Q)
    gate = _pad_rows(g_ref[...].astype(F32), Q)
    ri = lax.broadcasted_iota(jnp.int32, (Q, 1), 0)
    ci = lax.broadcasted_iota(jnp.int32, (1, Q), 1)
    cnt_i = jnp.minimum(ri + 1, valid).astype(F32)
    cnt_j = jnp.minimum(ci + 1, valid).astype(F32)
    causal = ri >= ci
    lane = lax.broadcasted_iota(jnp.int32, (1, LANES), 1)
    ys = []
    for h in range(RET_HEADS):
        lg = RET_LOG_GAMMA[h]
        p, hh = divmod(h, 2)
        mine = (lane // RET_QK) == hh
        qm = jnp.where(mine, q[:, p * LANES:(p + 1) * LANES], 0.0).astype(BF16)
        km = jnp.where(mine, k[:, p * LANES:(p + 1) * LANES], 0.0).astype(BF16)
        dec = jnp.exp(jnp.where(causal, (cnt_i - cnt_j) * lg, NEG_INF))
        sc = (_dot_nt(qm, km) * dec).astype(BF16)
        vh = jnp.where(ri < valid, v[:, h * RET_V:(h + 1) * RET_V], 0.0)
        s_stack = s_ref[p * LANES:(p + 1) * LANES, :]
        y = _dot(sc, vh.astype(BF16)) + _dot(qm, s_stack.astype(BF16)) * jnp.exp(cnt_i * lg)
        tail = jnp.exp((valid - cnt_i) * lg)
        upd = _dot_tn(km, (vh * tail).astype(BF16))
        rs = slice(h * RET_QK, (h + 1) * RET_QK)
        s_ref[rs, :] = s_ref[rs, :] * math.exp(valid * lg) + upd[hh * RET_QK:(hh + 1) * RET_QK, :]
        mu = jnp.mean(y, axis=-1, keepdims=True)
        yc = y - mu
        var = jnp.mean(yc * yc, axis=-1, keepdims=True)
        ys.append(yc * lax.rsqrt(var + NORM_EPS))
    y = jnp.concatenate(ys, axis=-1) * nw_ref[...] * _silu(gate)
    y_ref[...] = y[:rows].astype(y_ref.dtype)
    snew_ref[...] = s_ref[...]


def _ret(q, k, v, g, cos, sin, s0, nw, rows, valid):
    B, L, _ = q.shape
    nc = L // rows
    ns = SEQ_PER_STEP
    assert B % ns == 0
    seq = lambda width: pl.BlockSpec((ns, rows, width), lambda b, c: (b, c, 0))
    tab = pl.BlockSpec((rows, 2 * LANES), lambda b, c: (c, 0))
    st = pl.BlockSpec((ns, 2 * LANES, LANES), lambda b, c: (b, 0, 0))
    return pl.pallas_call(
        functools.partial(_per_sequence, _ret_chunk, _ret_init, 5, 3, rows=rows, valid=valid),
        grid=(B // ns, nc),
        in_specs=[seq(256), seq(256), seq(512), seq(512), st, tab, tab, _full((1, BRANCH_W))],
        out_specs=[seq(BRANCH_W), st],
        out_shape=[jax.ShapeDtypeStruct((B, L, BRANCH_W), BF16), jax.ShapeDtypeStruct((B, 2 * LANES, LANES), F32)],
        scratch_shapes=[pltpu.VMEM((ns, 2 * LANES, LANES), F32)],
        compiler_params=pltpu.CompilerParams(dimension_semantics=("parallel", "arbitrary")),
        name="retention",
    )(q, k, v, g, s0, cos, sin, nw)


def _rel_bias_minus_far(dist, tb_ref, head):
    far = tb_ref[REL_BUCKETS - 1, head]
    val = jnp.full(dist.shape, tb_ref[0, head] - far, F32)
    for bk in range(1, REL_BUCKETS):
        val = jnp.where(dist >= REL_LB[bk], tb_ref[bk, head] - far, val)
    return val


def _softmax_step(s, v_bf16, m_ref, l_ref, acc_ref):
    m_prev = m_ref[...]
    m_new = jnp.maximum(m_prev, jnp.max(s, axis=-1, keepdims=True))
    alpha = jnp.exp(m_prev - m_new)
    p = jnp.exp(s - m_new)
    l_ref[...] = alpha * l_ref[...] + jnp.sum(p, axis=-1, keepdims=True)
    acc_ref[...] = alpha * acc_ref[...] + _dot(p.astype(BF16), v_bf16)
    m_ref[...] = m_new


def _softmax_init(m_ref, l_ref, acc_ref):
    m_ref[...] = jnp.full(m_ref.shape, NEG_INF, F32)
    l_ref[...] = jnp.zeros(l_ref.shape, F32)
    acc_ref[...] = jnp.zeros(acc_ref.shape, F32)


LOG2E = math.log2(math.e)


def _diff_stack_q(q, qs_ref, rows, scale=DIFF_DH ** -0.5):
    lane = lax.broadcasted_iota(jnp.int32, (1, LANES), 1)
    for g in range(2):
        qg = q[:, g * LANES:(g + 1) * LANES] * scale
        for m in range(2):
            r = (g * 2 + m) * rows
            qs_ref[r:r + rows, :] = jnp.where((lane // DIFF_DH) == m, qg, 0.0).astype(BF16)


def _diff_finish(o, sc_ref, subln, rows):
    outs = []
    for g in range(2):
        og = o[(2 * g) * rows:(2 * g + 1) * rows] - sc_ref[0] * o[(2 * g + 1) * rows:(2 * g + 2) * rows]
        outs.append(_rms(og) * subln * sc_ref[1])
    return jnp.concatenate(outs, axis=-1)


FAR_TILES = 4


Q_TILES = 2


def _causal_sweep(qi, visit):
    n_far = jnp.maximum(Q_TILES * qi - 1, 0)
    n_full = n_far // FAR_TILES
    rem = n_far - n_full * FAR_TILES

    def far(i, carry):
        visit(i * FAR_TILES, FAR_TILES, 0)
        return carry

    lax.fori_loop(0, n_full, far, 0)

    @pl.when(qi == 0)
    def _():
        visit(0, Q_TILES, Q_TILES)

    for r in sorted({(Q_TILES * i - 1) % FAR_TILES for i in range(1, FAR_TILES + 1)}):
        @pl.when(jnp.logical_and(qi >= 1, rem == r))
        def _(r=r):
            visit(n_full * FAR_TILES, r + Q_TILES + 1, Q_TILES + 1)


def _key_rows(tile0, ntiles):
    start = tile0 * CHUNK
    return pl.ds(start if isinstance(start, int) else pl.multiple_of(start, CHUNK), ntiles * CHUNK)


def _two_pass_attention(qi, score_fn, value_fn, s_ref, m_ref, l_ref, acc_ref, halves):
    m_ref[...] = jnp.full(m_ref.shape, NEG_INF, F32)

    def track_max(tile0, ntiles, near):
        for rows in halves:
            s = score_fn(rows, tile0, ntiles, near)
            mx = None
            for c in range(ntiles):
                sc = s[:, c * CHUNK:(c + 1) * CHUNK]
                s_ref[tile0 + c, rows, :] = sc
                mx = sc if mx is None else jnp.maximum(mx, sc)
            m_ref[rows, :] = jnp.maximum(m_ref[rows, :], mx)

    _causal_sweep(qi, track_max)
    m_ref[...] = jnp.broadcast_to(jnp.max(m_ref[...], axis=-1, keepdims=True), m_ref.shape)
    l_ref[...] = jnp.zeros(l_ref.shape, F32)
    acc_ref[...] = jnp.zeros(acc_ref.shape, F32)

    def accumulate(tile0, ntiles, near):
        v = value_fn(tile0, ntiles)
        for rows in halves:
            m = m_ref[rows, :]
            ps = [jnp.exp2(s_ref[tile0 + c, rows, :] - m) for c in range(ntiles)]
            lsum = ps[0]
            for pc in ps[1:]:
                lsum = lsum + pc
            l_ref[rows, :] += lsum
            p = ps[0] if len(ps) == 1 else jnp.concatenate(ps, axis=-1)
            acc_ref[rows, :] += _dot(p.astype(BF16), v)

    _causal_sweep(qi, accumulate)
    return acc_ref[...] / jnp.sum(l_ref[...], axis=-1, keepdims=True)


def _diffp_body(tb_ref, sc_ref, q_ref, k_ref, v_ref, sub_ref, y_ref,
                qs_ref, s_ref, m_ref, l_ref, acc_ref, tn_ref):
    tq = Q_TILES * CHUNK
    near_w = (Q_TILES + 1) * CHUNK
    h = pl.program_id(1)
    qi = pl.program_id(2)

    @pl.when(qi == 0)
    def _():
        ri = lax.broadcasted_iota(jnp.int32, (tq, near_w), 0)
        ci = lax.broadcasted_iota(jnp.int32, (tq, near_w), 1)
        dist = ri + CHUNK - ci
        for g in range(2):
            bias = _rel_bias_minus_far(jnp.maximum(dist, 0), tb_ref, h * 2 + g) * LOG2E
            bias = jnp.where(dist >= 0, bias, NEG_INF)
            for m in range(2):
                r = (g * 2 + m) * tq
                tn_ref[r:r + tq, :] = bias

    _diff_stack_q(q_ref[...].astype(F32), qs_ref, tq, scale=DIFF_DH ** -0.5 * LOG2E)

    def score_fn(rows, tile0, ntiles, near):
        s = _dot_nt(qs_ref[rows, :], k_ref[_key_rows(tile0, ntiles), :].astype(BF16))
        if near:
            far_w = (ntiles - near) * CHUNK
            biased = s[:, far_w:] + tn_ref[rows, near_w - near * CHUNK:]
            s = biased if far_w == 0 else jnp.concatenate([s[:, :far_w], biased], axis=-1)
        return s

    def value_fn(tile0, ntiles):
        return v_ref[_key_rows(tile0, ntiles), :].astype(BF16)

    groups = tuple(slice(i * tq, (i + 1) * tq) for i in range(4))
    o = _two_pass_attention(qi, score_fn, value_fn, s_ref, m_ref, l_ref, acc_ref, groups)
    y_ref[...] = _diff_finish(o, sc_ref, sub_ref[...], tq).astype(y_ref.dtype)


def _diff_prompt(tb, sc, dq, dk, dv, subln):
    B, L, _ = dq.shape
    tq = Q_TILES * CHUNK
    assert CHUNK + 1 >= REL_FAR and L % tq == 0
    return pl.pallas_call(
        _diffp_body,
        grid=(B, DIFF_KVH, L // tq),
        in_specs=[_SMEM, _SMEM,
                  pl.BlockSpec((None, tq, 2 * LANES), lambda b, h, i: (b, i, h)),
                  pl.BlockSpec((None, L, LANES), lambda b, h, i: (b, 0, h)),
                  pl.BlockSpec((None, L, LANES), lambda b, h, i: (b, 0, h)),
                  _full((1, DIFF_DV))],
        out_specs=pl.BlockSpec((None, tq, 2 * LANES), lambda b, h, i: (b, i, h)),
        out_shape=jax.ShapeDtypeStruct((B, L, BRANCH_W), BF16),
        scratch_shapes=[pltpu.VMEM((4 * tq, LANES), BF16), pltpu.VMEM((L // CHUNK, 4 * tq, CHUNK), F32),
                        pltpu.VMEM((4 * tq, CHUNK), F32), pltpu.VMEM((4 * tq, CHUNK), F32),
                        pltpu.VMEM((4 * tq, LANES), F32), pltpu.VMEM((4 * tq, (Q_TILES + 1) * CHUNK), F32)],
        compiler_params=pltpu.CompilerParams(dimension_semantics=("parallel", "parallel", "arbitrary"),
                                             vmem_limit_bytes=VMEM_LIMIT),
        name="diff_prompt",
    )(tb, sc, dq, dk, dv, subln)


MLA_SCALE = (MLA_NOPE + MLA_ROPE) ** -0.5


def _mla_q_prep(mq, qn, wq_ref, wuk_ref, cosf, sins, qs_ref, rows, scale=MLA_SCALE):
    cq = (_rms(mq) * qn).astype(BF16)
    qh = _dot(cq, wq_ref[...])
    for h in range(MLA_HEADS):
        nope = qh[:, h * 256:h * 256 + LANES].astype(BF16)
        rope = _rotate(qh[:, h * 256 + LANES:(h + 1) * 256], cosf, sins, MLA_ROPE // 2)
        qs_ref[h * rows:(h + 1) * rows, :LANES] = (_dot(nope, wuk_ref[h]) * scale).astype(BF16)
        qs_ref[h * rows:(h + 1) * rows, LANES:] = (rope * scale).astype(BF16)


def _mla_kv_prep(mkv, kvn, cosf, sins):
    c_kv = _rms(mkv[:, :LANES]) * kvn
    k_rope = _rotate(mkv[:, LANES:], cosf, sins, MLA_ROPE // 2)
    return c_kv, k_rope


def _mla_finish(o, wuv_ref, rows):
    o = o.astype(BF16)
    return jnp.concatenate([_dot(o[h * rows:(h + 1) * rows], wuv_ref[h]) for h in range(MLA_HEADS)], axis=-1)


def _mlap_body(mq_ref, mkv_ref, cq_ref, sq_ref, ck_ref, sk_ref, qn_ref, kvn_ref, wq_ref, wuk_ref, wuv_ref,
               y_ref, lat_ref, kr_ref, kcat_ref, qs_ref, s_ref, m_ref, l_ref, acc_ref):
    tq = Q_TILES * CHUNK
    qi = pl.program_id(1)

    @pl.when(qi == 0)
    def _():
        c_kv, k_rope = _mla_kv_prep(mkv_ref[...], kvn_ref[...], ck_ref[...], sk_ref[...])
        lat_ref[...] = c_kv
        kr_ref[...] = k_rope[:, :MLA_ROPE]
        kcat_ref[:, :LANES] = c_kv.astype(BF16)
        kcat_ref[:, LANES:] = k_rope.astype(BF16)

    _mla_q_prep(mq_ref[...].astype(F32), qn_ref[...], wq_ref, wuk_ref, cq_ref[...], sq_ref[...], qs_ref, tq,
                scale=MLA_SCALE * LOG2E)

    def score_fn(rows, tile0, ntiles, near):
        s = _dot_nt(qs_ref[rows, :], kcat_ref[_key_rows(tile0, ntiles), :])
        if near:
            ri = lax.broadcasted_iota(jnp.int32, (tq, ntiles * CHUNK), 0)
            ci = lax.broadcasted_iota(jnp.int32, (tq, ntiles * CHUNK), 1) - (ntiles - Q_TILES) * CHUNK
            s = jnp.where(ri >= ci, s, NEG_INF)
        return s

    def value_fn(tile0, ntiles):
        return kcat_ref[_key_rows(tile0, ntiles), :LANES]

    groups = tuple(slice(h * tq, (h + 1) * tq) for h in range(MLA_HEADS))
    o = _two_pass_attention(qi, score_fn, value_fn, s_ref, m_ref, l_ref, acc_ref, groups)
    y_ref[...] = _mla_finish(o, wuv_ref, tq).astype(y_ref.dtype)


def _mla_prompt(mq, mkv, cos, sin, w):
    B, L, _ = mq.shape
    tq = Q_TILES * CHUNK
    assert L % tq == 0
    return pl.pallas_call(
        _mlap_body,
        grid=(B, L // tq),
        in_specs=[pl.BlockSpec((None, tq, 256), lambda b, i: (b, i, 0)),
                  pl.BlockSpec((None, L, 256), lambda b, i: (b, 0, 0)),
                  pl.BlockSpec((tq, LANES), lambda b, i: (i, 0)), pl.BlockSpec((tq, LANES), lambda b, i: (i, 0)),
                  _full((L, LANES)), _full((L, LANES)),
                  _full((1, MLA_Q_RANK)), _full((1, MLA_KV_RANK)),
                  _full((MLA_Q_RANK, MLA_HEADS * 256)), _full((MLA_HEADS, LANES, LANES)),
                  _full((MLA_HEADS, LANES, LANES))],
        out_specs=[pl.BlockSpec((None, tq, BRANCH_W), lambda b, i: (b, i, 0)),
                   pl.BlockSpec((None, L, MLA_KV_RANK), lambda b, i: (b, 0, 0)),
                   pl.BlockSpec((None, L, MLA_ROPE), lambda b, i: (b, 0, 0))],
        out_shape=[jax.ShapeDtypeStruct((B, L, BRANCH_W), BF16), jax.ShapeDtypeStruct((B, L, MLA_KV_RANK), F32),
                   jax.ShapeDtypeStruct((B, L, MLA_ROPE), F32)],
        scratch_shapes=[pltpu.VMEM((L, 2 * LANES), BF16), pltpu.VMEM((MLA_HEADS * tq, 2 * LANES), BF16),
                        pltpu.VMEM((L // CHUNK, MLA_HEADS * tq, CHUNK), F32),
                        pltpu.VMEM((MLA_HEADS * tq, CHUNK), F32), pltpu.VMEM((MLA_HEADS * tq, CHUNK), F32),
                        pltpu.VMEM((MLA_HEADS * tq, LANES), F32)],
        compiler_params=pltpu.CompilerParams(dimension_semantics=("parallel", "arbitrary"),
                                             vmem_limit_bytes=VMEM_LIMIT),
        name="mla_prompt",
    )(mq, mkv, cos, sin, cos, sin, w["mla_q_norm"], w["mla_kv_norm"], w["mla_wq"], w["mla_wuk"], w["mla_wuv"])


SROWS = 8


def _paged_fetch(pt_ref, caches, bufs, sems, layer, npg):
    b, j = pl.program_id(0), pl.program_id(1)
    nb, nj = pl.num_programs(0), pl.num_programs(1)
    step = b * nj + j
    slot = step % 2

    def copies(bb, jj, sl):
        return [pltpu.make_async_copy(cache.at[pt_ref[bb, jj * npg + i], layer], buf.at[sl, i], sem.at[sl])
                for cache, buf, sem in zip(caches, bufs, sems) for i in range(npg)]

    @pl.when(step == 0)
    def _():
        for cp in copies(b, j, slot):
            cp.start()

    @pl.when(step + 1 < nb * nj)
    def _():
        wrap = j + 1 == nj
        for cp in copies(jnp.where(wrap, b + 1, b), jnp.where(wrap, 0, j + 1), 1 - slot):
            cp.start()

    for cp in copies(b, j, slot):
        cp.wait()
    return slot


def _diffs_body(pt_ref, tb_ref, sc_ref, q_ref, kn_ref, vn_ref, sub_ref, ckt_hbm, cv_hbm, y_ref,
                qs_ref, m_ref, l_ref, acc_ref, kt_buf, v_buf, kt_sem, v_sem, *, npg, past_len, layer):
    j = pl.program_id(1)
    nj = pl.num_programs(1)
    width = npg * PAGE
    slot = _paged_fetch(pt_ref, (ckt_hbm, cv_hbm), (kt_buf, v_buf), (kt_sem, v_sem), layer, npg)

    @pl.when(j == 0)
    def _():
        for h in range(DIFF_KVH):
            _diff_stack_q(q_ref[:, h * 256:(h + 1) * 256].astype(F32), qs_ref.at[h], SROWS)
        _softmax_init(m_ref, l_ref, acc_ref)

    def visit(h, bias):
        kt = jnp.concatenate([kt_buf[slot, i, h * LANES:(h + 1) * LANES, :].astype(BF16) for i in range(npg)], axis=1)
        s = _dot(qs_ref[h], kt)
        if bias is not None:
            s = s + bias
        v = jnp.concatenate([v_buf[slot, i, pl.ds(h, PAGE, stride=DIFF_KVH), :].astype(BF16) for i in range(npg)],
                            axis=0)
        _softmax_step(s, v, m_ref.at[h], l_ref.at[h], acc_ref.at[h])

    @pl.when(j < nj - 1)
    def _():
        for h in range(DIFF_KVH):
            visit(h, None)

    @pl.when(j == nj - 1)
    def _():
        t_row = lax.broadcasted_iota(jnp.int32, (2 * SROWS, width), 0) % SROWS
        ci = lax.broadcasted_iota(jnp.int32, (2 * SROWS, width), 1)
        dist = past_len + t_row - ((nj - 1) * width + ci)
        t_new = lax.broadcasted_iota(jnp.int32, (2 * SROWS, PAGE), 0) % SROWS
        c_new = lax.broadcasted_iota(jnp.int32, (2 * SROWS, PAGE), 1)
        for h in range(DIFF_KVH):
            hs = slice(h * LANES, (h + 1) * LANES)
            visit(h, jnp.concatenate([_rel_bias_minus_far(dist, tb_ref, h * 2 + g) for g in range(2)], axis=0))
            kn = _pad_rows(kn_ref[:, hs], PAGE).astype(BF16)
            vn = _pad_rows(vn_ref[:, hs], PAGE).astype(BF16)
            bias_n = jnp.concatenate(
                [jnp.where(t_new >= c_new, _rel_bias_minus_far(jnp.maximum(t_new - c_new, 0), tb_ref, h * 2 + g), NEG_INF)
                 for g in range(2)], axis=0)
            _softmax_step(_dot_nt(qs_ref[h], kn) + bias_n, vn, m_ref.at[h], l_ref.at[h], acc_ref.at[h])
            y_ref[:, h * 256:(h + 1) * 256] = _diff_finish(acc_ref[h] / l_ref[h], sc_ref, sub_ref[...],
                                                           SROWS).astype(y_ref.dtype)


def _pages_per_step(n_pages, want):
    return math.gcd(n_pages, want)


def _diff_sample(pt, tb, sc, dq, dk, dv, subln, cache_kt, cache_v, layer):
    B = dq.shape[0]
    n_pages = pt.shape[1]
    npg = _pages_per_step(n_pages, 16)
    R = 4 * SROWS
    tok = lambda width: pl.BlockSpec((None, SROWS, width), lambda b, j, pt: (b, 0, 0))
    hbm = pl.BlockSpec(memory_space=pl.ANY)
    page_buf = pltpu.VMEM((2, npg, 2 * PAGE, LANES), F32)
    grid_spec = pltpu.PrefetchScalarGridSpec(
        num_scalar_prefetch=1,
        grid=(B, n_pages // npg),
        in_specs=[_SMEM, _SMEM, tok(512), tok(256), tok(256), pl.BlockSpec((1, DIFF_DV), lambda b, j, pt: (0, 0)),
                  hbm, hbm],
        out_specs=tok(BRANCH_W),
        scratch_shapes=[pltpu.VMEM((DIFF_KVH, R, LANES), BF16), pltpu.VMEM((DIFF_KVH, R, 1), F32),
                        pltpu.VMEM((DIFF_KVH, R, 1), F32), pltpu.VMEM((DIFF_KVH, R, LANES), F32),
                        page_buf, page_buf, pltpu.SemaphoreType.DMA((2,)), pltpu.SemaphoreType.DMA((2,))],
    )
    return pl.pallas_call(
        functools.partial(_diffs_body, npg=npg, past_len=n_pages * PAGE, layer=layer),
        grid_spec=grid_spec,
        out_shape=jax.ShapeDtypeStruct((B, SROWS, BRANCH_W), BF16),
        compiler_params=pltpu.CompilerParams(dimension_semantics=("arbitrary", "arbitrary"),
                                             vmem_limit_bytes=VMEM_LIMIT),
        name="diff_sample",
    )(pt, tb, sc, dq, dk, dv, subln, cache_kt, cache_v)


def _mlas_body(pt_ref, mq_ref, mkv_ref, cos_ref, sin_ref, qn_ref, kvn_ref, wq_ref, wuk_ref, wuv_ref, cc_hbm, crt_hbm,
               y_ref, lat_ref, kr_ref, qs_ref, kn_ref, m_ref, l_ref, acc_ref, c_buf, rt_buf, c_sem, rt_sem,
               *, npg, layer):
    j = pl.program_id(1)
    nj = pl.num_programs(1)
    R = MLA_HEADS * SROWS
    slot = _paged_fetch(pt_ref, (cc_hbm, crt_hbm), (c_buf, rt_buf), (c_sem, rt_sem), layer, npg)

    @pl.when(j == 0)
    def _():
        c_kv, k_rope = _mla_kv_prep(mkv_ref[...], kvn_ref[...], cos_ref[...], sin_ref[...])
        lat_ref[...] = c_kv
        kr_ref[...] = k_rope[:, :MLA_ROPE]
        kn_ref[...] = jnp.zeros(kn_ref.shape, BF16)
        kn_ref[0:SROWS, :LANES] = c_kv.astype(BF16)
        kn_ref[0:SROWS, LANES:] = k_rope.astype(BF16)
        _mla_q_prep(mq_ref[...].astype(F32), qn_ref[...], wq_ref, wuk_ref, cos_ref[...], sin_ref[...], qs_ref, SROWS)
        _softmax_init(m_ref, l_ref, acc_ref)

    lat = jnp.concatenate([c_buf[slot, i].astype(BF16) for i in range(npg)], axis=0)
    krt = jnp.concatenate([rt_buf[slot, i].astype(BF16) for i in range(npg)], axis=1)
    s = _dot_nt(qs_ref[:, :LANES], lat) + _dot(qs_ref[:, LANES:LANES + MLA_ROPE], krt)
    _softmax_step(s, lat, m_ref, l_ref, acc_ref)

    @pl.when(j == nj - 1)
    def _():
        kc = kn_ref[...]
        t_new = lax.broadcasted_iota(jnp.int32, (R, PAGE), 0) % SROWS
        c_new = lax.broadcasted_iota(jnp.int32, (R, PAGE), 1)
        sn = jnp.where(t_new >= c_new, _dot_nt(qs_ref[...], kc), NEG_INF)
        _softmax_step(sn, kc[:, :LANES], m_ref, l_ref, acc_ref)
        y_ref[...] = _mla_finish(acc_ref[...] / l_ref[...], wuv_ref, SROWS).astype(y_ref.dtype)


def _mla_sample(pt, mq, mkv, cos, sin, w, cache_c, cache_rt, layer):
    B = mq.shape[0]
    n_pages = pt.shape[1]
    npg = _pages_per_step(n_pages, 32)
    R = MLA_HEADS * SROWS
    tok = lambda width: pl.BlockSpec((None, SROWS, width), lambda b, j, pt: (b, 0, 0))
    full = lambda *shape: pl.BlockSpec(shape, lambda b, j, pt: (0,) * len(shape))
    hbm = pl.BlockSpec(memory_space=pl.ANY)
    grid_spec = pltpu.PrefetchScalarGridSpec(
        num_scalar_prefetch=1,
        grid=(B, n_pages // npg),
        in_specs=[tok(256), tok(256), full(SROWS, LANES), full(SROWS, LANES), full(1, MLA_Q_RANK), full(1, MLA_KV_RANK),
                  full(MLA_Q_RANK, MLA_HEADS * 256), full(MLA_HEADS, LANES, LANES), full(MLA_HEADS, LANES, LANES),
                  hbm, hbm],
        out_specs=[tok(BRANCH_W), tok(MLA_KV_RANK), tok(MLA_ROPE)],
        scratch_shapes=[pltpu.VMEM((R, 2 * LANES), BF16), pltpu.VMEM((PAGE, 2 * LANES), BF16),
                        pltpu.VMEM((R, 1), F32), pltpu.VMEM((R, 1), F32), pltpu.VMEM((R, LANES), F32),
                        pltpu.VMEM((2, npg, PAGE, LANES), F32), pltpu.VMEM((2, npg, MLA_ROPE, LANES), F32),
                        pltpu.SemaphoreType.DMA((2,)), pltpu.SemaphoreType.DMA((2,))],
    )
    return pl.pallas_call(
        functools.partial(_mlas_body, npg=npg, layer=layer),
        grid_spec=grid_spec,
        out_shape=[jax.ShapeDtypeStruct((B, SROWS, BRANCH_W), BF16), jax.ShapeDtypeStruct((B, SROWS, MLA_KV_RANK), F32),
                   jax.ShapeDtypeStruct((B, SROWS, MLA_ROPE), F32)],
        compiler_params=pltpu.CompilerParams(dimension_semantics=("arbitrary", "arbitrary"),
                                             vmem_limit_bytes=VMEM_LIMIT),
        name="mla_sample",
    )(pt, mq, mkv, cos, sin, w["mla_q_norm"], w["mla_kv_norm"], w["mla_wq"], w["mla_wuk"], w["mla_wuv"],
      cache_c, cache_rt)


def _merge_body(x_ref, y0_ref, y1_ref, y2_ref, y3_ref, gt_ref, wb_ref, wo_ref, o_ref):
    mix = None
    for i, y_ref in enumerate((y0_ref, y1_ref, y2_ref, y3_ref)):
        proj = _dot(y_ref[...].astype(BF16), wb_ref[i])
        gate = gt_ref[:, i * D_MODEL:(i + 1) * D_MODEL].astype(F32)
        term = proj / (1.0 + jnp.exp(-gate))
        mix = term if mix is None else mix + term
    o_ref[...] = x_ref[...] + _dot(mix.astype(BF16), wo_ref[...])


def _merge(x, ys, gates, wb, wo):
    T = x.shape[0]
    tm = min(512, T)
    row = lambda width: pl.BlockSpec((tm, width), lambda i: (i, 0))
    return pl.pallas_call(
        _merge_body,
        grid=(T // tm,),
        in_specs=[row(D_MODEL)] + [row(BRANCH_W)] * 4 + [row(N_BRANCH * D_MODEL),
                                                       _resident((N_BRANCH, BRANCH_W, D_MODEL)),
                                                       _resident((D_MODEL, D_MODEL))],
        out_specs=row(D_MODEL),
        out_shape=jax.ShapeDtypeStruct((T, D_MODEL), F32),
        compiler_params=pltpu.CompilerParams(dimension_semantics=("parallel",), vmem_limit_bytes=VMEM_LIMIT),
        name="merge",
    )(x, *ys, gates, wb, wo)


def _rope_tables(pos, half):
    inv_freq = ROPE_BASE ** (-jnp.arange(half, dtype=F32) / half)
    ang = pos.astype(F32)[:, None] * inv_freq[None, :]
    cos, sin = jnp.cos(ang), jnp.sin(ang)
    return jnp.concatenate([cos, cos], axis=-1), jnp.concatenate([-sin, sin], axis=-1)


def _layer_weights(l, p):
    w = {}
    f = lambda a: a.astype(BF16)
    for n in ("ffn1_w1", "ffn1_w3", "ffn1_w2", "ffn2_w1", "ffn2_w3", "ffn2_w2", "w_out"):
        w[n] = f(p[n][l])
    w["w_branch"] = f(p["w_branch"][l])
    for n in ("norm_ffn1", "norm_mix", "norm_ffn2", "ssm_conv_b", "ssm_norm", "ret_norm", "diff_subln",
              "mla_q_norm", "mla_kv_norm"):
        w[n] = p[n][l][None, :]
    offs = np.cumsum((0,) + IN_SIZES)
    piece = lambda i: p["w_in"][l][:, offs[i]:offs[i + 1]]
    padc = lambda a, width: jnp.pad(a, ((0, 0), (0, width - a.shape[1])))
    cols = [piece(0), piece(1), padc(piece(2), LANES), piece(3), piece(4), piece(5), piece(6), piece(7), piece(8),
            piece(9), piece(10), padc(piece(11), 2 * LANES), piece(12)]
    w["w_in"] = f(jnp.concatenate(cols, axis=1))
    w["conv_w"] = p["ssm_conv_w"][l]
    w["conv_b"] = w["ssm_conv_b"]
    w["dt_bias"] = padc(p["ssm_dt_bias"][l][None, :], LANES)
    w["dt_biasT"] = p["ssm_dt_bias"][l][:, None]
    w["a_log"] = padc(p["ssm_a_log"][l][None, :], LANES)
    w["a_logT"] = p["ssm_a_log"][l][:, None]
    w["d_exp"] = jnp.repeat(p["ssm_d"][l], SSM_HEAD_DIM)[None, :]
    expand = np.zeros((LANES, BRANCH_W), np.float32)
    for h in range(SSM_HEADS):
        expand[h, h * SSM_HEAD_DIM:(h + 1) * SSM_HEAD_DIM] = 1.0
    w["expand"] = jnp.asarray(expand, dtype=BF16)
    wq = p["mla_w_q_up"][l]
    zq = jnp.zeros((MLA_Q_RANK, MLA_HEADS, LANES - MLA_NOPE), F32)
    zr = jnp.zeros((MLA_Q_RANK, MLA_HEADS, LANES - MLA_ROPE), F32)
    w["mla_wq"] = f(jnp.concatenate([wq[..., :MLA_NOPE], zq, wq[..., MLA_NOPE:], zr], axis=-1)
                    .reshape(MLA_Q_RANK, MLA_HEADS * 256))
    wuk = jnp.transpose(p["mla_w_uk"][l], (1, 2, 0))
    w["mla_wuk"] = f(jnp.pad(wuk, ((0, 0), (0, LANES - MLA_NOPE), (0, 0))))
    w["mla_wuv"] = f(jnp.transpose(p["mla_w_uv"][l], (1, 0, 2)))
    lam_init = 0.8 - 0.6 * math.exp(-0.3 * l)
    lw = p["diff_lambda"][l].astype(F32)
    lam = jnp.exp(jnp.sum(lw[0] * lw[1])) - jnp.exp(jnp.sum(lw[2] * lw[3])) + lam_init
    w["diff_sc"] = jnp.stack([lam, jnp.asarray(1.0 - lam_init, F32)]).astype(F32)
    return w


def _ssm_state_to_pairs(s):
    B = s.shape[0]
    return s.reshape(B, 4, 2, SSM_STATE, SSM_HEAD_DIM).transpose(0, 1, 3, 2, 4).reshape(B, 4, SSM_STATE, 2 * SSM_HEAD_DIM)


def _ssm_state_from_pairs(s):
    B = s.shape[0]
    return s.reshape(B, 4, SSM_STATE, 2, SSM_HEAD_DIM).transpose(0, 1, 3, 2, 4).reshape(B, SSM_HEADS, SSM_STATE, SSM_HEAD_DIM)


def _trunk(x, w_layers, rel_bias, final_norm, rows, valid, pos, conv0, ssm0, ret0, paged):
    B, L, _ = x.shape
    T = B * L
    depth = len(w_layers)
    xt = x.reshape(T, D_MODEL)
    ret_cos, ret_sin = _rope_tables(pos, RET_QK // 2)
    ret_cos, ret_sin = jnp.tile(ret_cos, (1, 4)), jnp.tile(ret_sin, (1, 4))
    mc, ms = _rope_tables(pos, MLA_ROPE // 2)
    mla_cos = jnp.pad(mc, ((0, 0), (0, LANES - MLA_ROPE)))
    mla_sin = jnp.pad(ms, ((0, 0), (0, LANES - MLA_ROPE)))
    nc = L // rows
    outs = []
    for l in range(depth):
        w = w_layers[l]
        xt = _ffn(xt, w["norm_ffn1"], w["ffn1_w1"], w["ffn1_w3"], w["ffn1_w2"])
        pr = _inproj(xt, w["norm_mix"], w["w_in"])
        r3 = lambda a: a.reshape(B, L, a.shape[-1])
        dt = r3(pr["dt"])
        dtT = jnp.transpose(dt[:, :, :SSM_HEADS].reshape(B, nc, rows, SSM_HEADS), (0, 3, 1, 2))
        dtT = jnp.pad(dtT, ((0, 0), (0, 0), (0, 0), (0, CHUNK - rows))).reshape(B, SSM_HEADS, nc * CHUNK)
        dtp = dt if rows == CHUNK else jnp.pad(dt, ((0, 0), (0, CHUNK - rows), (0, 0)))
        y_ssm, conv_new, ssm_new = _ssd(r3(pr["xbc"]), r3(pr["z"]), dtp, dtT, conv0[l], ssm0[l], w, rows, valid)
        y_ret, ret_new = _ret(r3(pr["rq"]), r3(pr["rk"]), r3(pr["rv"]), r3(pr["rg"]), ret_cos, ret_sin, ret0[l],
                              w["ret_norm"], rows, valid)
        dk, dv = r3(pr["dk"]), r3(pr["dv"])
        if paged is None:
            y_diff = _diff_prompt(rel_bias, w["diff_sc"], r3(pr["dq"]), dk, dv, w["diff_subln"])
            y_mla, lat, kr = _mla_prompt(r3(pr["mq"]), r3(pr["mkv"]), mla_cos, mla_sin, w)
        else:
            pt, ck, cv, cc, cr = paged
            y_diff = _diff_sample(pt, rel_bias, w["diff_sc"], r3(pr["dq"]), dk, dv, w["diff_subln"], ck, cv, l)
            y_mla, lat, kr = _mla_sample(pt, r3(pr["mq"]), r3(pr["mkv"]), mla_cos, mla_sin, w, cc, cr, l)
        ys = [a.reshape(T, BRANCH_W) for a in (y_ssm, y_ret, y_diff, y_mla)]
        xt = _merge(xt, ys, pr["gates"], w["w_branch"], w["w_out"])
        fg = final_norm if l == depth - 1 else None
        xt = _ffn(xt, w["norm_ffn2"], w["ffn2_w1"], w["ffn2_w3"], w["ffn2_w2"], fg)
        outs.append((dk, dv, lat, kr, conv_new, ssm_new, ret_new))
    return xt.reshape(B, L, D_MODEL), outs


def kernel(x_prompt, x_sample, cache_diff_k, cache_diff_v, cache_mla_latent, cache_mla_krope, state_ssm_conv, state_ssm, state_retention, page_table, norm_ffn1, ffn1_w1, ffn1_w3, ffn1_w2, norm_mix, w_in, ssm_conv_w, ssm_conv_b, ssm_dt_bias, ssm_a_log, ssm_d, ssm_norm, ret_norm, diff_lambda, diff_subln, mla_q_norm, mla_w_q_up, mla_kv_norm, mla_w_uk, mla_w_uv, rel_bias, w_branch, w_out, norm_ffn2, ffn2_w1, ffn2_w3, ffn2_w2, final_norm):
    p = dict(norm_ffn1=norm_ffn1, ffn1_w1=ffn1_w1, ffn1_w3=ffn1_w3, ffn1_w2=ffn1_w2, norm_mix=norm_mix, w_in=w_in,
             ssm_conv_w=ssm_conv_w, ssm_conv_b=ssm_conv_b, ssm_dt_bias=ssm_dt_bias, ssm_a_log=ssm_a_log, ssm_d=ssm_d,
             ssm_norm=ssm_norm, ret_norm=ret_norm, diff_lambda=diff_lambda, diff_subln=diff_subln,
             mla_q_norm=mla_q_norm, mla_w_q_up=mla_w_q_up, mla_kv_norm=mla_kv_norm, mla_w_uk=mla_w_uk,
             mla_w_uv=mla_w_uv, w_branch=w_branch, w_out=w_out, norm_ffn2=norm_ffn2, ffn2_w1=ffn2_w1,
             ffn2_w3=ffn2_w3, ffn2_w2=ffn2_w2)
    depth = w_in.shape[0]
    w_layers = [_layer_weights(l, p) for l in range(depth)]
    fnorm = final_norm[None, :]
    rel = rel_bias.astype(F32)

    Bp, Lp, _ = x_prompt.shape
    zc = jnp.zeros((Bp, CONV_W - 1, SSM_CONV_DIM), F32)
    zs = jnp.zeros((Bp, 4, LANES, LANES), F32)
    zr = jnp.zeros((Bp, 2 * LANES, LANES), F32)
    y_prompt, rows_p = _trunk(x_prompt, w_layers, rel, fnorm, CHUNK, CHUNK, jnp.arange(Lp, dtype=jnp.int32),
                              [zc] * depth, [zs] * depth, [zr] * depth, None)

    Bs, Ls, _ = x_sample.shape
    n_pages = page_table.shape[1]
    past_len = n_pages * PAGE
    xs = jnp.pad(x_sample, ((0, 0), (0, SROWS - Ls), (0, 0)))
    pos_s = past_len + jnp.arange(SROWS, dtype=jnp.int32)
    n_pool = cache_diff_k.shape[0]
    paged = (page_table.astype(jnp.int32),
             jnp.transpose(cache_diff_k, (0, 1, 3, 4, 5, 2)).reshape(n_pool, depth, 2 * PAGE, LANES),
             cache_diff_v.reshape(n_pool, depth, 2 * PAGE, LANES),
             cache_mla_latent, jnp.transpose(cache_mla_krope, (0, 1, 3, 2)))
    conv_s = [state_ssm_conv[:, l] for l in range(depth)]
    ssm_s = [_ssm_state_to_pairs(state_ssm[:, l]) for l in range(depth)]
    ret_s = [state_retention[:, l].reshape(Bs, 2 * LANES, LANES) for l in range(depth)]
    y_sample, rows_s = _trunk(xs, w_layers, rel, fnorm, SROWS, Ls, pos_s, conv_s, ssm_s, ret_s, paged)

    def collect(rows, B, L, keep):
        st = lambda i: jnp.stack([r[i] for r in rows], axis=1)
        dk = st(0)[:, :, :keep].reshape(B, depth, keep, DIFF_KVH, 2, DIFF_DH)
        dv = st(1)[:, :, :keep].reshape(B, depth, keep, DIFF_KVH, DIFF_DV)
        lat = st(2)[:, :, :keep]
        kr = st(3)[:, :, :keep]
        conv = st(4)
        ssm = jnp.stack([_ssm_state_from_pairs(r[5]) for r in rows], axis=1)
        ret = st(6).reshape(B, depth, RET_HEADS, RET_QK, RET_V)
        return dk, dv, lat, kr, conv, ssm, ret

    return (y_prompt, y_sample[:, :Ls]) + collect(rows_p, Bp, Lp, Lp) + collect(rows_s, Bs, SROWS, Ls)
```

```python
import functools
import math

import numpy as np
import jax
import jax.numpy as jnp
from jax import lax
from jax.experimental import pallas as pl
from jax.experimental.pallas import tpu as pltpu

F32 = jnp.float32
BF16 = jnp.bfloat16
NEG_INF = float("-inf")

D_MODEL = 1024
D_FF = 2816
BRANCH_W = 512
N_BRANCH = 4
SSM_HEADS = 8
SSM_HEAD_DIM = 64
SSM_STATE = 128
CONV_W = 4
SSM_CONV_DIM = 1024
RET_HEADS = 4
RET_QK = 64
RET_V = 128
DIFF_KVH = 2
DIFF_DH = 64
DIFF_DV = 128
MLA_HEADS = 4
MLA_Q_RANK = 256
MLA_KV_RANK = 128
MLA_NOPE = 64
MLA_ROPE = 32
PAGE = 128
REL_BUCKETS = 32
REL_MAX_EXACT = 16
REL_MAX_DIST = 128
ROPE_BASE = 10000.0
NORM_EPS = 1e-6
IN_SIZES = (512, 1024, 8, 256, 256, 512, 512, 512, 256, 256, 256, 160, 4096)

LANES = 128
MXU_WIDTH = 256
CHUNK = 128
VMEM_LIMIT = 56 * 1024 * 1024


def _rel_lower_bounds():
    lb = list(range(REL_MAX_EXACT))
    d = np.arange(REL_MAX_EXACT, 4 * REL_MAX_DIST).astype(np.float32)
    large = REL_MAX_EXACT + (np.log(d / np.float32(REL_MAX_EXACT)) / np.float32(math.log(REL_MAX_DIST / REL_MAX_EXACT))
                             * np.float32(REL_BUCKETS - REL_MAX_EXACT)).astype(np.int32)
    large = np.minimum(large, REL_BUCKETS - 1)
    for bk in range(REL_MAX_EXACT, REL_BUCKETS):
        lb.append(int(REL_MAX_EXACT + np.argmax(large >= bk)))
    return tuple(lb)


REL_LB = _rel_lower_bounds()
REL_FAR = REL_LB[-1]
RET_LOG_GAMMA = tuple(math.log1p(-(2.0 ** (-5.0 - h))) for h in range(RET_HEADS))


def _dot(a, b, precision=None):
    return jnp.dot(a, b, preferred_element_type=F32, precision=precision)


def _dot_nt(a, b):
    return lax.dot_general(a, b, (((1,), (1,)), ((), ())), preferred_element_type=F32)


def _dot_tn(a, b):
    return lax.dot_general(a, b, (((0,), (0,)), ((), ())), preferred_element_type=F32)


def _rms(x):
    return x * lax.rsqrt(jnp.mean(x * x, axis=-1, keepdims=True) + NORM_EPS)


def _silu(x):
    return x / (1.0 + jnp.exp(-x))


def _softplus(x):
    return jnp.maximum(x, 0.0) + jnp.log1p(jnp.exp(-jnp.abs(x)))


def _dot_select(x, sel, sel_first=False):
    hi = x.astype(BF16)
    r1 = x - hi.astype(F32)
    mid = r1.astype(BF16)
    lo = (r1 - mid.astype(F32)).astype(BF16)
    if sel_first:
        return _dot(sel, hi) + _dot(sel, mid) + _dot(sel, lo)
    return _dot(hi, sel) + _dot(mid, sel) + _dot(lo, sel)


def _pad_rows(x, rows):
    if x.shape[0] == rows:
        return x
    return jnp.concatenate([x, jnp.zeros((rows - x.shape[0],) + x.shape[1:], x.dtype)], axis=0)


def _rotate(x, cosf, sins, half):
    outs = []
    for s in range(x.shape[-1] // LANES):
        xs = x[:, s * LANES:(s + 1) * LANES]
        lane = lax.broadcasted_iota(jnp.int32, xs.shape, 1)
        first = (lane % (2 * half)) < half
        partner = jnp.where(first, pltpu.roll(xs, LANES - half, axis=1), pltpu.roll(xs, half, axis=1))
        outs.append(partner)
    partner = outs[0] if len(outs) == 1 else jnp.concatenate(outs, axis=-1)
    return x * cosf + partner * sins


def _full(shape):
    return pl.BlockSpec(shape, lambda *_: (0,) * len(shape))


def _resident(shape):
    return pl.BlockSpec(shape, lambda *_: (0,) * len(shape), pipeline_mode=pl.Buffered(1))


_SMEM = pl.BlockSpec(memory_space=pltpu.SMEM)


def _ffn_body(*refs, fc, final):
    if final:
        x_ref, g_ref, w1_ref, w3_ref, w2_ref, fg_ref, o_ref = refs
    else:
        x_ref, g_ref, w1_ref, w3_ref, w2_ref, o_ref = refs
    x = x_ref[...]
    xn = (_rms(x) * g_ref[...]).astype(BF16)
    acc = x
    for c in range(D_FF // fc):
        sl = slice(c * fc, (c + 1) * fc)
        h1 = _dot(xn, w1_ref[:, sl])
        h3 = _dot(xn, w3_ref[:, sl])
        a = (_silu(h1) * h3).astype(BF16)
        acc = acc + 0.5 * _dot(a, w2_ref[sl, :])
    if final:
        acc = _rms(acc) * fg_ref[...]
    o_ref[...] = acc


def _ffn(x, g, w1, w3, w2, final_g=None):
    T = x.shape[0]
    tm = min(1024, T)
    final = final_g is not None
    in_specs = [pl.BlockSpec((tm, D_MODEL), lambda i: (i, 0)), _full((1, D_MODEL)),
                _resident((D_MODEL, D_FF)), _resident((D_MODEL, D_FF)), _resident((D_FF, D_MODEL))]
    args = [x, g, w1, w3, w2]
    if final:
        in_specs.append(_full((1, D_MODEL)))
        args.append(final_g)
    return pl.pallas_call(
        functools.partial(_ffn_body, fc=MXU_WIDTH, final=final),
        grid=(T // tm,),
        in_specs=in_specs,
        out_specs=pl.BlockSpec((tm, D_MODEL), lambda i: (i, 0)),
        out_shape=jax.ShapeDtypeStruct((T, D_MODEL), F32),
        compiler_params=pltpu.CompilerParams(dimension_semantics=("parallel",), vmem_limit_bytes=VMEM_LIMIT),
        name="ffn",
    )(*args)


PROJ_OUT = (("z", 512, BF16), ("xbc", 1024, F32), ("dt", 128, F32), ("rq", 256, BF16), ("rk", 256, BF16),
            ("rv", 512, BF16), ("rg", 512, BF16), ("dq", 512, BF16), ("dk", 256, F32), ("dv", 256, F32),
            ("mq", 256, BF16), ("mkv", 256, F32), ("gates", 4096, BF16))
PROJ_W = sum(w for _, w, _ in PROJ_OUT)


def _inproj_body(x_ref, g_ref, w_ref, *o_refs):
    xn = (_rms(x_ref[...]) * g_ref[...]).astype(BF16)
    off = 0
    for (_, w, dt), o_ref in zip(PROJ_OUT, o_refs):
        o_ref[...] = _dot(xn, w_ref[:, off:off + w]).astype(dt)
        off += w


def _inproj(x, g, w):
    T = x.shape[0]
    tm = min(512, T)
    outs = pl.pallas_call(
        _inproj_body,
        grid=(T // tm,),
        in_specs=[pl.BlockSpec((tm, D_MODEL), lambda i: (i, 0)), _full((1, D_MODEL)), _resident((D_MODEL, PROJ_W))],
        out_specs=[pl.BlockSpec((tm, w_), lambda i: (i, 0)) for _, w_, _ in PROJ_OUT],
        out_shape=[jax.ShapeDtypeStruct((T, w_), dt) for _, w_, dt in PROJ_OUT],
        compiler_params=pltpu.CompilerParams(dimension_semantics=("parallel",), vmem_limit_bytes=VMEM_LIMIT),
        name="inproj",
    )(x, g, w)
    return {n: o for (n, _, _), o in zip(PROJ_OUT, outs)}


SEQ_PER_STEP = 4


def _per_sequence(body, init, n_seq_in, n_shared, *refs, **static):
    seq_in, shared, seq_rest = refs[:n_seq_in], refs[n_seq_in:n_seq_in + n_shared], refs[n_seq_in + n_shared:]
    views = [[r.at[i] for r in seq_in] + list(shared) + [r.at[i] for r in seq_rest] for i in range(SEQ_PER_STEP)]

    @pl.when(pl.program_id(1) == 0)
    def _():
        for v in views:
            init(*v)

    for v in views:
        body(*v, **static)


def _ssd_init(xbc_ref, z_ref, dt_ref, dtT_ref, cb0_ref, s0_ref, cw_ref, cbias_ref, dtb_ref, dtbT_ref,
              alog_ref, alogT_ref, dexp_ref, nw_ref, e_ref, y_ref, cnew_ref, snew_ref, xp_ref, s_ref):
    xp_ref[0:8, :] = jnp.zeros((8, SSM_CONV_DIM), F32)
    xp_ref[5:8, :] = cb0_ref[...]
    s_ref[...] = s0_ref[...]


def _ssd_chunk(xbc_ref, z_ref, dt_ref, dtT_ref, cb0_ref, s0_ref, cw_ref, cbias_ref, dtb_ref, dtbT_ref,
               alog_ref, alogT_ref, dexp_ref, nw_ref, e_ref,
               y_ref, cnew_ref, snew_ref, xp_ref, s_ref, *, rows, valid):
    Q = CHUNK
    xp_ref[8:8 + Q, :] = _pad_rows(xbc_ref[...], Q)
    conv = cbias_ref[...] + xp_ref[5:5 + Q, :] * cw_ref[0:1, :]
    for w in range(1, CONV_W):
        conv = conv + xp_ref[5 + w:5 + w + Q, :] * cw_ref[w:w + 1, :]
    u = _silu(conv)
    last3 = xp_ref[8 + valid - 3:8 + valid, :]
    xp_ref[5:8, :] = last3
    cnew_ref[...] = last3

    row = lax.broadcasted_iota(jnp.int32, (Q, Q), 0)
    col = lax.broadcasted_iota(jnp.int32, (Q, Q), 1)
    causal = row >= col

    dtv = _softplus(dt_ref[...] + dtb_ref[...])
    dtv = jnp.where(lax.broadcasted_iota(jnp.int32, dtv.shape, 0) < valid, dtv, 0.0)
    la = -jnp.exp(alog_ref[...]) * dtv
    cum = _dot_select(la, causal.astype(BF16), sel_first=True)
    dtT = _softplus(dtT_ref[...] + dtbT_ref[...])
    dtT = jnp.where(lax.broadcasted_iota(jnp.int32, dtT.shape, 1) < valid, dtT, 0.0)
    laT = _pad_rows(-jnp.exp(alogT_ref[...]) * dtT, 2 * SSM_HEADS)
    cumT = _dot_select(laT, (row <= col).astype(BF16))
    both = _dot_select(jnp.concatenate([cum, dtv], axis=0), e_ref[...])
    cum_exp, dt_exp = both[:Q], both[Q:]
    cl_exp = cum_exp[Q - 1:Q, :]
    ecum = jnp.exp(cum_exp)
    tail = jnp.exp(cl_exp - cum_exp)
    sdecay = jnp.exp(cl_exp)

    xs = u[:, :BRANCH_W]
    v = xs * dt_exp
    vb = v.astype(BF16)
    vt = (v * tail).astype(BF16)
    lane = lax.broadcasted_iota(jnp.int32, (Q, LANES), 1)
    ys = []
    for g in range(2):
        bg = u[:, 512 + g * 128:512 + (g + 1) * 128].astype(BF16)
        cg = u[:, 768 + g * 128:768 + (g + 1) * 128].astype(BF16)
        gmat = _dot_nt(cg, bg)
        for pp in range(2):
            p = g * 2 + pp
            sl = slice(p * LANES, (p + 1) * LANES)
            s_pair = s_ref[p]
            inter = _dot(cg, s_pair.astype(BF16))
            yh = []
            for hh in range(2):
                h = 2 * p + hh
                seg = cum[:, h:h + 1] - cumT[h:h + 1, :]
                dec = jnp.exp(jnp.where(causal, seg, NEG_INF))
                yh.append(_dot((gmat * dec).astype(BF16), vb[:, sl]))
            ys.append(jnp.where(lane < SSM_HEAD_DIM, yh[0], yh[1]) + inter * ecum[:, sl])
            s_ref[p] = s_pair * sdecay[:, sl] + _dot_tn(bg, vt[:, sl])
    y = jnp.concatenate(ys, axis=-1) + xs * dexp_ref[...]
    y = y * _silu(_pad_rows(z_ref[...].astype(F32), Q))
    half = BRANCH_W // 2
    y = jnp.concatenate([_rms(y[:, :half]), _rms(y[:, half:])], axis=-1) * nw_ref[...]
    y_ref[...] = y[:rows].astype(y_ref.dtype)
    snew_ref[...] = s_ref[...]


def _ssd(xbc, z, dt, dtT, cb0, s0, w, rows, valid):
    B, L, _ = xbc.shape
    nc = L // rows
    ns = SEQ_PER_STEP
    assert B % ns == 0
    seq = lambda width: pl.BlockSpec((ns, rows, width), lambda b, c: (b, c, 0))
    per_b = lambda *shape: pl.BlockSpec((ns,) + shape, lambda b, c: (b,) + (0,) * len(shape))
    return pl.pallas_call(
        functools.partial(_per_sequence, _ssd_chunk, _ssd_init, 6, 9, rows=rows, valid=valid),
        grid=(B // ns, nc),
        in_specs=[seq(SSM_CONV_DIM), seq(BRANCH_W), pl.BlockSpec((ns, CHUNK, LANES), lambda b, c: (b, c, 0)),
                  pl.BlockSpec((ns, SSM_HEADS, CHUNK), lambda b, c: (b, 0, c)),
                  per_b(CONV_W - 1, SSM_CONV_DIM), per_b(4, LANES, LANES),
                  _full((CONV_W, SSM_CONV_DIM)), _full((1, SSM_CONV_DIM)), _full((1, LANES)), _full((SSM_HEADS, 1)),
                  _full((1, LANES)), _full((SSM_HEADS, 1)), _full((1, BRANCH_W)), _full((1, BRANCH_W)),
                  _full((LANES, BRANCH_W))],
        out_specs=[seq(BRANCH_W), per_b(CONV_W - 1, SSM_CONV_DIM), per_b(4, LANES, LANES)],
        out_shape=[jax.ShapeDtypeStruct((B, L, BRANCH_W), BF16),
                   jax.ShapeDtypeStruct((B, CONV_W - 1, SSM_CONV_DIM), F32),
                   jax.ShapeDtypeStruct((B, 4, LANES, LANES), F32)],
        scratch_shapes=[pltpu.VMEM((ns, 8 + CHUNK, SSM_CONV_DIM), F32), pltpu.VMEM((ns, 4, LANES, LANES), F32)],
        compiler_params=pltpu.CompilerParams(dimension_semantics=("parallel", "arbitrary")),
        name="ssd",
    )(xbc, z, dt, dtT, cb0, s0, w["conv_w"], w["conv_b"], w["dt_bias"], w["dt_biasT"], w["a_log"], w["a_logT"],
      w["d_exp"], w["ssm_norm"], w["expand"])


def _ret_init(q_ref, k_ref, v_ref, g_ref, s0_ref, cos_ref, sin_ref, nw_ref, y_ref, snew_ref, s_ref):
    s_ref[...] = s0_ref[...]


def _ret_chunk(q_ref, k_ref, v_ref, g_ref, s0_ref, cos_ref, sin_ref, nw_ref, y_ref, snew_ref, s_ref, *, rows, valid):
    Q = CHUNK
    cosf = _pad_rows(cos_ref[...], Q)
    sins = _pad_rows(sin_ref[...], Q)
    q = _rotate(_pad_rows(q_ref[...].astype(F32), Q), cosf, sins, RET_QK // 2)
    k = _rotate(_pad_rows(k_ref[...].astype(F32), Q), cosf, sins, RET_QK // 2) * (RET_QK ** -0.5)
    v = _pad_rows(v_ref[...].astype(F32), Q)
    gate = _pad_rows(g_ref[...].astype(F32), Q)
    ri = lax.broadcasted_iota(jnp.int32, (Q, 1), 0)
    ci = lax.broadcasted_iota(jnp.int32, (1, Q), 1)
    cnt_i = jnp.minimum(ri + 1, valid).astype(F32)
    cnt_j = jnp.minimum(ci + 1, valid).astype(F32)
    causal = ri >= ci
    lane = lax.broadcasted_iota(jnp.int32, (1, LANES), 1)
    ys = []
    for h in range(RET_HEADS):
        lg = RET_LOG_GAMMA[h]
        p, hh = divmod(h, 2)
        mine = (lane // RET_QK) == hh
        qm = jnp.where(mine, q[:, p * LANES:(p + 1) * LANES], 0.0).astype(BF16)
        km = jnp.where(mine, k[:, p * LANES:(p + 1) * LANES], 0.0).astype(BF16)
        dec = jnp.exp(jnp.where(causal, (cnt_i - cnt_j) * lg, NEG_INF))
        sc = (_dot_nt(qm, km) * dec).astype(BF16)
        vh = jnp.where(ri < valid, v[:, h * RET_V:(h + 1) * RET_V], 0.0)
        s_stack = s_ref[p * LANES:(p + 1) * LANES, :]
        y = _dot(sc, vh.astype(BF16)) + _dot(qm, s_stack.astype(BF16)) * jnp.exp(cnt_i * lg)
        tail = jnp.exp((valid - cnt_i) * lg)
        upd = _dot_tn(km, (vh * tail).astype(BF16))
        rs = slice(h * RET_QK, (h + 1) * RET_QK)
        s_ref[rs, :] = s_ref[rs, :] * math.exp(valid * lg) + upd[hh * RET_QK:(hh + 1) * RET_QK, :]
        mu = jnp.mean(y, axis=-1, keepdims=True)
        yc = y - mu
        var = jnp.mean(yc * yc, axis=-1, keepdims=True)
        ys.append(yc * lax.rsqrt(var + NORM_EPS))
    y = jnp.concatenate(ys, axis=-1) * nw_ref[...] * _silu(gate)
    y_ref[...] = y[:rows].astype(y_ref.dtype)
    snew_ref[...] = s_ref[...]


def _ret(q, k, v, g, cos, sin, s0, nw, rows, valid):
    B, L, _ = q.shape
    nc = L // rows
    ns = SEQ_PER_STEP
    assert B % ns == 0
    seq = lambda width: pl.BlockSpec((ns, rows, width), lambda b, c: (b, c, 0))
    tab = pl.BlockSpec((rows, 2 * LANES), lambda b, c: (c, 0))
    st = pl.BlockSpec((ns, 2 * LANES, LANES), lambda b, c: (b, 0, 0))
    return pl.pallas_call(
        functools.partial(_per_sequence, _ret_chunk, _ret_init, 5, 3, rows=rows, valid=valid),
        grid=(B // ns, nc),
        in_specs=[seq(256), seq(256), seq(512), seq(512), st, tab, tab, _full((1, BRANCH_W))],
        out_specs=[seq(BRANCH_W), st],
        out_shape=[jax.ShapeDtypeStruct((B, L, BRANCH_W), BF16), jax.ShapeDtypeStruct((B, 2 * LANES, LANES), F32)],
        scratch_shapes=[pltpu.VMEM((ns, 2 * LANES, LANES), F32)],
        compiler_params=pltpu.CompilerParams(dimension_semantics=("parallel", "arbitrary")),
        name="retention",
    )(q, k, v, g, s0, cos, sin, nw)


def _rel_bias_minus_far(dist, tb_ref, head):
    far = tb_ref[REL_BUCKETS - 1, head]
    val = jnp.full(dist.shape, tb_ref[0, head] - far, F32)
    for bk in range(1, REL_BUCKETS):
        val = jnp.where(dist >= REL_LB[bk], tb_ref[bk, head] - far, val)
    return val


def _softmax_step(s, v_bf16, m_ref, l_ref, acc_ref):
    m_prev = m_ref[...]
    m_new = jnp.maximum(m_prev, jnp.max(s, axis=-1, keepdims=True))
    alpha = jnp.exp(m_prev - m_new)
    p = jnp.exp(s - m_new)
    l_ref[...] = alpha * l_ref[...] + jnp.sum(p, axis=-1, keepdims=True)
    acc_ref[...] = alpha * acc_ref[...] + _dot(p.astype(BF16), v_bf16)
    m_ref[...] = m_new


def _softmax_init(m_ref, l_ref, acc_ref):
    m_ref[...] = jnp.full(m_ref.shape, NEG_INF, F32)
    l_ref[...] = jnp.zeros(l_ref.shape, F32)
    acc_ref[...] = jnp.zeros(acc_ref.shape, F32)


LOG2E = math.log2(math.e)


def _diff_stack_q(q, qs_ref, rows, scale=DIFF_DH ** -0.5):
    lane = lax.broadcasted_iota(jnp.int32, (1, LANES), 1)
    for g in range(2):
        qg = q[:, g * LANES:(g + 1) * LANES] * scale
        for m in range(2):
            r = (g * 2 + m) * rows
            qs_ref[r:r + rows, :] = jnp.where((lane // DIFF_DH) == m, qg, 0.0).astype(BF16)


def _diff_finish(o, sc_ref, subln, rows):
    outs = []
    for g in range(2):
        og = o[(2 * g) * rows:(2 * g + 1) * rows] - sc_ref[0] * o[(2 * g + 1) * rows:(2 * g + 2) * rows]
        outs.append(_rms(og) * subln * sc_ref[1])
    return jnp.concatenate(outs, axis=-1)


FAR_TILES = 4


Q_TILES = 2


def _causal_sweep(qi, visit):
    n_far = jnp.maximum(Q_TILES * qi - 1, 0)
    n_full = n_far // FAR_TILES
    rem = n_far - n_full * FAR_TILES

    def far(i, carry):
        visit(i * FAR_TILES, FAR_TILES, 0)
        return carry

    lax.fori_loop(0, n_full, far, 0)

    @pl.when(qi == 0)
    def _():
        visit(0, Q_TILES, Q_TILES)

    for r in sorted({(Q_TILES * i - 1) % FAR_TILES for i in range(1, FAR_TILES + 1)}):
        @pl.when(jnp.logical_and(qi >= 1, rem == r))
        def _(r=r):
            visit(n_full * FAR_TILES, r + Q_TILES + 1, Q_TILES + 1)


def _key_rows(tile0, ntiles):
    start = tile0 * CHUNK
    return pl.ds(start if isinstance(start, int) else pl.multiple_of(start, CHUNK), ntiles * CHUNK)


def _two_pass_attention(qi, score_fn, value_fn, s_ref, m_ref, acc_ref, halves):
    m_ref[...] = jnp.full(m_ref.shape, NEG_INF, F32)

    def track_max(tile0, ntiles, near):
        for rows in halves:
            s = score_fn(rows, tile0, ntiles, near)
            mx = None
            for c in range(ntiles):
                sc = s[:, c * CHUNK:(c + 1) * CHUNK]
                s_ref[tile0 + c, rows, :] = sc
                mx = sc if mx is None else jnp.maximum(mx, sc)
            m_ref[rows, :] = jnp.maximum(m_ref[rows, :], mx)

    _causal_sweep(qi, track_max)
    m_ref[...] = jnp.broadcast_to(jnp.max(m_ref[...], axis=-1, keepdims=True), m_ref.shape)
    acc_ref[...] = jnp.zeros(acc_ref.shape, F32)

    def accumulate(tile0, ntiles, near):
        v = value_fn(tile0, ntiles)
        for rows in halves:
            m = m_ref[rows, :]
            ps = [jnp.exp2(s_ref[tile0 + c, rows, :] - m) for c in range(ntiles)]
            p = ps[0] if len(ps) == 1 else jnp.concatenate(ps, axis=-1)
            acc_ref[rows, :] += _dot(p.astype(BF16), v)

    _causal_sweep(qi, accumulate)
    return acc_ref[:, :LANES] / acc_ref[:, LANES:]


def _diffp_body(tb_ref, sc_ref, q_ref, k_ref, v_ref, sub_ref, y_ref,
                qs_ref, s_ref, m_ref, acc_ref, tn_ref, vext_ref):
    tq = Q_TILES * CHUNK
    near_w = (Q_TILES + 1) * CHUNK
    h = pl.program_id(1)
    qi = pl.program_id(2)

    @pl.when(qi == 0)
    def _():
        vext_ref[:, :LANES] = v_ref[...].astype(BF16)
        vext_ref[:, LANES:] = jnp.ones((v_ref.shape[0], LANES), BF16)
        ri = lax.broadcasted_iota(jnp.int32, (tq, near_w), 0)
        ci = lax.broadcasted_iota(jnp.int32, (tq, near_w), 1)
        dist = ri + CHUNK - ci
        for g in range(2):
            bias = _rel_bias_minus_far(jnp.maximum(dist, 0), tb_ref, h * 2 + g) * LOG2E
            bias = jnp.where(dist >= 0, bias, NEG_INF)
            for m in range(2):
                r = (g * 2 + m) * tq
                tn_ref[r:r + tq, :] = bias

    _diff_stack_q(q_ref[...].astype(F32), qs_ref, tq, scale=DIFF_DH ** -0.5 * LOG2E)

    def score_fn(rows, tile0, ntiles, near):
        s = _dot_nt(qs_ref[rows, :], k_ref[_key_rows(tile0, ntiles), :].astype(BF16))
        if near:
            far_w = (ntiles - near) * CHUNK
            biased = s[:, far_w:] + tn_ref[rows, near_w - near * CHUNK:]
            s = biased if far_w == 0 else jnp.concatenate([s[:, :far_w], biased], axis=-1)
        return s

    def value_fn(tile0, ntiles):
        return vext_ref[_key_rows(tile0, ntiles), :]

    groups = tuple(slice(i * tq, (i + 1) * tq) for i in range(4))
    o = _two_pass_attention(qi, score_fn, value_fn, s_ref, m_ref, acc_ref, groups)
    y_ref[...] = _diff_finish(o, sc_ref, sub_ref[...], tq).astype(y_ref.dtype)


def _diff_prompt(tb, sc, dq, dk, dv, subln):
    B, L, _ = dq.shape
    tq = Q_TILES * CHUNK
    assert CHUNK + 1 >= REL_FAR and L % tq == 0
    return pl.pallas_call(
        _diffp_body,
        grid=(B, DIFF_KVH, L // tq),
        in_specs=[_SMEM, _SMEM,
                  pl.BlockSpec((None, tq, 2 * LANES), lambda b, h, i: (b, i, h)),
                  pl.BlockSpec((None, L, LANES), lambda b, h, i: (b, 0, h)),
                  pl.BlockSpec((None, L, LANES), lambda b, h, i: (b, 0, h)),
                  _full((1, DIFF_DV))],
        out_specs=pl.BlockSpec((None, tq, 2 * LANES), lambda b, h, i: (b, i, h)),
        out_shape=jax.ShapeDtypeStruct((B, L, BRANCH_W), BF16),
        scratch_shapes=[pltpu.VMEM((4 * tq, LANES), BF16), pltpu.VMEM((L // CHUNK, 4 * tq, CHUNK), F32),
                        pltpu.VMEM((4 * tq, CHUNK), F32), pltpu.VMEM((4 * tq, 2 * LANES), F32),
                        pltpu.VMEM((4 * tq, (Q_TILES + 1) * CHUNK), F32), pltpu.VMEM((L, 2 * LANES), BF16)],
        compiler_params=pltpu.CompilerParams(dimension_semantics=("parallel", "parallel", "arbitrary"),
                                             vmem_limit_bytes=VMEM_LIMIT),
        name="diff_prompt",
    )(tb, sc, dq, dk, dv, subln)


MLA_SCALE = (MLA_NOPE + MLA_ROPE) ** -0.5


def _mla_q_prep(mq, qn, wq_ref, wuk_ref, cosf, sins, qs_ref, rows, scale=MLA_SCALE):
    cq = (_rms(mq) * qn).astype(BF16)
    qh = _dot(cq, wq_ref[...])
    for h in range(MLA_HEADS):
        nope = qh[:, h * 256:h * 256 + LANES].astype(BF16)
        rope = _rotate(qh[:, h * 256 + LANES:(h + 1) * 256], cosf, sins, MLA_ROPE // 2)
        qs_ref[h * rows:(h + 1) * rows, :LANES] = (_dot(nope, wuk_ref[h]) * scale).astype(BF16)
        qs_ref[h * rows:(h + 1) * rows, LANES:] = (rope * scale).astype(BF16)


def _mla_kv_prep(mkv, kvn, cosf, sins):
    c_kv = _rms(mkv[:, :LANES]) * kvn
    k_rope = _rotate(mkv[:, LANES:], cosf, sins, MLA_ROPE // 2)
    return c_kv, k_rope


def _mla_finish(o, wuv_ref, rows):
    o = o.astype(BF16)
    return jnp.concatenate([_dot(o[h * rows:(h + 1) * rows], wuv_ref[h]) for h in range(MLA_HEADS)], axis=-1)


def _mlap_body(mq_ref, mkv_ref, cq_ref, sq_ref, ck_ref, sk_ref, qn_ref, kvn_ref, wq_ref, wuk_ref, wuv_ref,
               y_ref, lat_ref, kr_ref, kcat_ref, qs_ref, s_ref, m_ref, acc_ref, vext_ref):
    tq = Q_TILES * CHUNK
    qi = pl.program_id(1)

    @pl.when(qi == 0)
    def _():
        c_kv, k_rope = _mla_kv_prep(mkv_ref[...], kvn_ref[...], ck_ref[...], sk_ref[...])
        lat_ref[...] = c_kv
        kr_ref[...] = k_rope[:, :MLA_ROPE]
        kcat_ref[:, :LANES] = c_kv.astype(BF16)
        kcat_ref[:, LANES:] = k_rope.astype(BF16)
        vext_ref[:, :LANES] = c_kv.astype(BF16)
        vext_ref[:, LANES:] = jnp.ones(c_kv.shape, BF16)

    _mla_q_prep(mq_ref[...].astype(F32), qn_ref[...], wq_ref, wuk_ref, cq_ref[...], sq_ref[...], qs_ref, tq,
                scale=MLA_SCALE * LOG2E)

    def score_fn(rows, tile0, ntiles, near):
        s = _dot_nt(qs_ref[rows, :], kcat_ref[_key_rows(tile0, ntiles), :])
        if near:
            ri = lax.broadcasted_iota(jnp.int32, (tq, ntiles * CHUNK), 0)
            ci = lax.broadcasted_iota(jnp.int32, (tq, ntiles * CHUNK), 1) - (ntiles - Q_TILES) * CHUNK
            s = jnp.where(ri >= ci, s, NEG_INF)
        return s

    def value_fn(tile0, ntiles):
        return vext_ref[_key_rows(tile0, ntiles), :]

    groups = tuple(slice(h * tq, (h + 1) * tq) for h in range(MLA_HEADS))
    o = _two_pass_attention(qi, score_fn, value_fn, s_ref, m_ref, acc_ref, groups)
    y_ref[...] = _mla_finish(o, wuv_ref, tq).astype(y_ref.dtype)


def _mla_prompt(mq, mkv, cos, sin, w):
    B, L, _ = mq.shape
    tq = Q_TILES * CHUNK
    assert L % tq == 0
    return pl.pallas_call(
        _mlap_body,
        grid=(B, L // tq),
        in_specs=[pl.BlockSpec((None, tq, 256), lambda b, i: (b, i, 0)),
                  pl.BlockSpec((None, L, 256), lambda b, i: (b, 0, 0)),
                  pl.BlockSpec((tq, LANES), lambda b, i: (i, 0)), pl.BlockSpec((tq, LANES), lambda b, i: (i, 0)),
                  _full((L, LANES)), _full((L, LANES)),
                  _full((1, MLA_Q_RANK)), _full((1, MLA_KV_RANK)),
                  _full((MLA_Q_RANK, MLA_HEADS * 256)), _full((MLA_HEADS, LANES, LANES)),
                  _full((MLA_HEADS, LANES, LANES))],
        out_specs=[pl.BlockSpec((None, tq, BRANCH_W), lambda b, i: (b, i, 0)),
                   pl.BlockSpec((None, L, MLA_KV_RANK), lambda b, i: (b, 0, 0)),
                   pl.BlockSpec((None, L, MLA_ROPE), lambda b, i: (b, 0, 0))],
        out_shape=[jax.ShapeDtypeStruct((B, L, BRANCH_W), BF16), jax.ShapeDtypeStruct((B, L, MLA_KV_RANK), F32),
                   jax.ShapeDtypeStruct((B, L, MLA_ROPE), F32)],
        scratch_shapes=[pltpu.VMEM((L, 2 * LANES), BF16), pltpu.VMEM((MLA_HEADS * tq, 2 * LANES), BF16),
                        pltpu.VMEM((L // CHUNK, MLA_HEADS * tq, CHUNK), F32),
                        pltpu.VMEM((MLA_HEADS * tq, CHUNK), F32), pltpu.VMEM((MLA_HEADS * tq, 2 * LANES), F32),
                        pltpu.VMEM((L, 2 * LANES), BF16)],
        compiler_params=pltpu.CompilerParams(dimension_semantics=("parallel", "arbitrary"),
                                             vmem_limit_bytes=VMEM_LIMIT),
        name="mla_prompt",
    )(mq, mkv, cos, sin, cos, sin, w["mla_q_norm"], w["mla_kv_norm"], w["mla_wq"], w["mla_wuk"], w["mla_wuv"])


SROWS = 8


def _paged_fetch(pt_ref, caches, bufs, sems, layer, npg):
    b, j = pl.program_id(0), pl.program_id(1)
    nb, nj = pl.num_programs(0), pl.num_programs(1)
    step = b * nj + j
    slot = step % 2

    def copies(bb, jj, sl):
        return [pltpu.make_async_copy(cache.at[pt_ref[bb, jj * npg + i], layer], buf.at[sl, i], sem.at[sl])
                for cache, buf, sem in zip(caches, bufs, sems) for i in range(npg)]

    @pl.when(step == 0)
    def _():
        for cp in copies(b, j, slot):
            cp.start()

    @pl.when(step + 1 < nb * nj)
    def _():
        wrap = j + 1 == nj
        for cp in copies(jnp.where(wrap, b + 1, b), jnp.where(wrap, 0, j + 1), 1 - slot):
            cp.start()

    for cp in copies(b, j, slot):
        cp.wait()
    return slot


def _diffs_body(pt_ref, tb_ref, sc_ref, q_ref, kn_ref, vn_ref, sub_ref, ckt_hbm, cv_hbm, y_ref,
                qs_ref, m_ref, l_ref, acc_ref, kt_buf, v_buf, kt_sem, v_sem, *, npg, past_len, layer):
    j = pl.program_id(1)
    nj = pl.num_programs(1)
    width = npg * PAGE
    slot = _paged_fetch(pt_ref, (ckt_hbm, cv_hbm), (kt_buf, v_buf), (kt_sem, v_sem), layer, npg)

    @pl.when(j == 0)
    def _():
        for h in range(DIFF_KVH):
            _diff_stack_q(q_ref[:, h * 256:(h + 1) * 256].astype(F32), qs_ref.at[h], SROWS)
        _softmax_init(m_ref, l_ref, acc_ref)

    def visit(h, bias):
        kt = jnp.concatenate([kt_buf[slot, i, h * LANES:(h + 1) * LANES, :].astype(BF16) for i in range(npg)], axis=1)
        s = _dot(qs_ref[h], kt)
        if bias is not None:
            s = s + bias
        v = jnp.concatenate([v_buf[slot, i, pl.ds(h, PAGE, stride=DIFF_KVH), :].astype(BF16) for i in range(npg)],
                            axis=0)
        _softmax_step(s, v, m_ref.at[h], l_ref.at[h], acc_ref.at[h])

    @pl.when(j < nj - 1)
    def _():
        for h in range(DIFF_KVH):
            visit(h, None)

    @pl.when(j == nj - 1)
    def _():
        t_row = lax.broadcasted_iota(jnp.int32, (2 * SROWS, width), 0) % SROWS
        ci = lax.broadcasted_iota(jnp.int32, (2 * SROWS, width), 1)
        dist = past_len + t_row - ((nj - 1) * width + ci)
        t_new = lax.broadcasted_iota(jnp.int32, (2 * SROWS, PAGE), 0) % SROWS
        c_new = lax.broadcasted_iota(jnp.int32, (2 * SROWS, PAGE), 1)
        for h in range(DIFF_KVH):
            hs = slice(h * LANES, (h + 1) * LANES)
            visit(h, jnp.concatenate([_rel_bias_minus_far(dist, tb_ref, h * 2 + g) for g in range(2)], axis=0))
            kn = _pad_rows(kn_ref[:, hs], PAGE).astype(BF16)
            vn = _pad_rows(vn_ref[:, hs], PAGE).astype(BF16)
            bias_n = jnp.concatenate(
                [jnp.where(t_new >= c_new, _rel_bias_minus_far(jnp.maximum(t_new - c_new, 0), tb_ref, h * 2 + g), NEG_INF)
                 for g in range(2)], axis=0)
            _softmax_step(_dot_nt(qs_ref[h], kn) + bias_n, vn, m_ref.at[h], l_ref.at[h], acc_ref.at[h])
            y_ref[:, h * 256:(h + 1) * 256] = _diff_finish(acc_ref[h] / l_ref[h], sc_ref, sub_ref[...],
                                                           SROWS).astype(y_ref.dtype)


def _pages_per_step(n_pages, want):
    return math.gcd(n_pages, want)


def _diff_sample(pt, tb, sc, dq, dk, dv, subln, cache_kt, cache_v, layer):
    B = dq.shape[0]
    n_pages = pt.shape[1]
    npg = _pages_per_step(n_pages, 16)
    R = 4 * SROWS
    tok = lambda width: pl.BlockSpec((None, SROWS, width), lambda b, j, pt: (b, 0, 0))
    hbm = pl.BlockSpec(memory_space=pl.ANY)
    page_buf = pltpu.VMEM((2, npg, 2 * PAGE, LANES), F32)
    grid_spec = pltpu.PrefetchScalarGridSpec(
        num_scalar_prefetch=1,
        grid=(B, n_pages // npg),
        in_specs=[_SMEM, _SMEM, tok(512), tok(256), tok(256), pl.BlockSpec((1, DIFF_DV), lambda b, j, pt: (0, 0)),
                  hbm, hbm],
        out_specs=tok(BRANCH_W),
        scratch_shapes=[pltpu.VMEM((DIFF_KVH, R, LANES), BF16), pltpu.VMEM((DIFF_KVH, R, 1), F32),
                        pltpu.VMEM((DIFF_KVH, R, 1), F32), pltpu.VMEM((DIFF_KVH, R, LANES), F32),
                        page_buf, page_buf, pltpu.SemaphoreType.DMA((2,)), pltpu.SemaphoreType.DMA((2,))],
    )
    return pl.pallas_call(
        functools.partial(_diffs_body, npg=npg, past_len=n_pages * PAGE, layer=layer),
        grid_spec=grid_spec,
        out_shape=jax.ShapeDtypeStruct((B, SROWS, BRANCH_W), BF16),
        compiler_params=pltpu.CompilerParams(dimension_semantics=("arbitrary", "arbitrary"),
                                             vmem_limit_bytes=VMEM_LIMIT),
        name="diff_sample",
    )(pt, tb, sc, dq, dk, dv, subln, cache_kt, cache_v)


def _mlas_body(pt_ref, mq_ref, mkv_ref, cos_ref, sin_ref, qn_ref, kvn_ref, wq_ref, wuk_ref, wuv_ref, cc_hbm, crt_hbm,
               y_ref, lat_ref, kr_ref, qs_ref, kn_ref, m_ref, l_ref, acc_ref, c_buf, rt_buf, c_sem, rt_sem,
               *, npg, layer):
    j = pl.program_id(1)
    nj = pl.num_programs(1)
    R = MLA_HEADS * SROWS
    slot = _paged_fetch(pt_ref, (cc_hbm, crt_hbm), (c_buf, rt_buf), (c_sem, rt_sem), layer, npg)

    @pl.when(j == 0)
    def _():
        c_kv, k_rope = _mla_kv_prep(mkv_ref[...], kvn_ref[...], cos_ref[...], sin_ref[...])
        lat_ref[...] = c_kv
        kr_ref[...] = k_rope[:, :MLA_ROPE]
        kn_ref[...] = jnp.zeros(kn_ref.shape, BF16)
        kn_ref[0:SROWS, :LANES] = c_kv.astype(BF16)
        kn_ref[0:SROWS, LANES:] = k_rope.astype(BF16)
        _mla_q_prep(mq_ref[...].astype(F32), qn_ref[...], wq_ref, wuk_ref, cos_ref[...], sin_ref[...], qs_ref, SROWS)
        _softmax_init(m_ref, l_ref, acc_ref)

    lat = jnp.concatenate([c_buf[slot, i].astype(BF16) for i in range(npg)], axis=0)
    krt = jnp.concatenate([rt_buf[slot, i].astype(BF16) for i in range(npg)], axis=1)
    s = _dot_nt(qs_ref[:, :LANES], lat) + _dot(qs_ref[:, LANES:LANES + MLA_ROPE], krt)
    _softmax_step(s, lat, m_ref, l_ref, acc_ref)

    @pl.when(j == nj - 1)
    def _():
        kc = kn_ref[...]
        t_new = lax.broadcasted_iota(jnp.int32, (R, PAGE), 0) % SROWS
        c_new = lax.broadcasted_iota(jnp.int32, (R, PAGE), 1)
        sn = jnp.where(t_new >= c_new, _dot_nt(qs_ref[...], kc), NEG_INF)
        _softmax_step(sn, kc[:, :LANES], m_ref, l_ref, acc_ref)
        y_ref[...] = _mla_finish(acc_ref[...] / l_ref[...], wuv_ref, SROWS).astype(y_ref.dtype)


def _mla_sample(pt, mq, mkv, cos, sin, w, cache_c, cache_rt, layer):
    B = mq.shape[0]
    n_pages = pt.shape[1]
    npg = _pages_per_step(n_pages, 32)
    R = MLA_HEADS * SROWS
    tok = lambda width: pl.BlockSpec((None, SROWS, width), lambda b, j, pt: (b, 0, 0))
    full = lambda *shape: pl.BlockSpec(shape, lambda b, j, pt: (0,) * len(shape))
    hbm = pl.BlockSpec(memory_space=pl.ANY)
    grid_spec = pltpu.PrefetchScalarGridSpec(
        num_scalar_prefetch=1,
        grid=(B, n_pages // npg),
        in_specs=[tok(256), tok(256), full(SROWS, LANES), full(SROWS, LANES), full(1, MLA_Q_RANK), full(1, MLA_KV_RANK),
                  full(MLA_Q_RANK, MLA_HEADS * 256), full(MLA_HEADS, LANES, LANES), full(MLA_HEADS, LANES, LANES),
                  hbm, hbm],
        out_specs=[tok(BRANCH_W), tok(MLA_KV_RANK), tok(MLA_ROPE)],
        scratch_shapes=[pltpu.VMEM((R, 2 * LANES), BF16), pltpu.VMEM((PAGE, 2 * LANES), BF16),
                        pltpu.VMEM((R, 1), F32), pltpu.VMEM((R, 1), F32), pltpu.VMEM((R, LANES), F32),
                        pltpu.VMEM((2, npg, PAGE, LANES), F32), pltpu.VMEM((2, npg, MLA_ROPE, LANES), F32),
                        pltpu.SemaphoreType.DMA((2,)), pltpu.SemaphoreType.DMA((2,))],
    )
    return pl.pallas_call(
        functools.partial(_mlas_body, npg=npg, layer=layer),
        grid_spec=grid_spec,
        out_shape=[jax.ShapeDtypeStruct((B, SROWS, BRANCH_W), BF16), jax.ShapeDtypeStruct((B, SROWS, MLA_KV_RANK), F32),
                   jax.ShapeDtypeStruct((B, SROWS, MLA_ROPE), F32)],
        compiler_params=pltpu.CompilerParams(dimension_semantics=("arbitrary", "arbitrary"),
                                             vmem_limit_bytes=VMEM_LIMIT),
        name="mla_sample",
    )(pt, mq, mkv, cos, sin, w["mla_q_norm"], w["mla_kv_norm"], w["mla_wq"], w["mla_wuk"], w["mla_wuv"],
      cache_c, cache_rt)


def _merge_body(x_ref, y0_ref, y1_ref, y2_ref, y3_ref, gt_ref, wb_ref, wo_ref, o_ref):
    mix = None
    for i, y_ref in enumerate((y0_ref, y1_ref, y2_ref, y3_ref)):
        proj = _dot(y_ref[...].astype(BF16), wb_ref[i])
        gate = gt_ref[:, i * D_MODEL:(i + 1) * D_MODEL].astype(F32)
        term = proj / (1.0 + jnp.exp(-gate))
        mix = term if mix is None else mix + term
    o_ref[...] = x_ref[...] + _dot(mix.astype(BF16), wo_ref[...])


def _merge(x, ys, gates, wb, wo):
    T = x.shape[0]
    tm = min(512, T)
    row = lambda width: pl.BlockSpec((tm, width), lambda i: (i, 0))
    return pl.pallas_call(
        _merge_body,
        grid=(T // tm,),
        in_specs=[row(D_MODEL)] + [row(BRANCH_W)] * 4 + [row(N_BRANCH * D_MODEL),
                                                       _resident((N_BRANCH, BRANCH_W, D_MODEL)),
                                                       _resident((D_MODEL, D_MODEL))],
        out_specs=row(D_MODEL),
        out_shape=jax.ShapeDtypeStruct((T, D_MODEL), F32),
        compiler_params=pltpu.CompilerParams(dimension_semantics=("parallel",), vmem_limit_bytes=VMEM_LIMIT),
        name="merge",
    )(x, *ys, gates, wb, wo)


def _rope_tables(pos, half):
    inv_freq = ROPE_BASE ** (-jnp.arange(half, dtype=F32) / half)
    ang = pos.astype(F32)[:, None] * inv_freq[None, :]
    cos, sin = jnp.cos(ang), jnp.sin(ang)
    return jnp.concatenate([cos, cos], axis=-1), jnp.concatenate([-sin, sin], axis=-1)


def _layer_weights(l, p):
    w = {}
    f = lambda a: a.astype(BF16)
    for n in ("ffn1_w1", "ffn1_w3", "ffn1_w2", "ffn2_w1", "ffn2_w3", "ffn2_w2", "w_out"):
        w[n] = f(p[n][l])
    w["w_branch"] = f(p["w_branch"][l])
    for n in ("norm_ffn1", "norm_mix", "norm_ffn2", "ssm_conv_b", "ssm_norm", "ret_norm", "diff_subln",
              "mla_q_norm", "mla_kv_norm"):
        w[n] = p[n][l][None, :]
    offs = np.cumsum((0,) + IN_SIZES)
    piece = lambda i: p["w_in"][l][:, offs[i]:offs[i + 1]]
    padc = lambda a, width: jnp.pad(a, ((0, 0), (0, width - a.shape[1])))
    cols = [piece(0), piece(1), padc(piece(2), LANES), piece(3), piece(4), piece(5), piece(6), piece(7), piece(8),
            piece(9), piece(10), padc(piece(11), 2 * LANES), piece(12)]
    w["w_in"] = f(jnp.concatenate(cols, axis=1))
    w["conv_w"] = p["ssm_conv_w"][l]
    w["conv_b"] = w["ssm_conv_b"]
    w["dt_bias"] = padc(p["ssm_dt_bias"][l][None, :], LANES)
    w["dt_biasT"] = p["ssm_dt_bias"][l][:, None]
    w["a_log"] = padc(p["ssm_a_log"][l][None, :], LANES)
    w["a_logT"] = p["ssm_a_log"][l][:, None]
    w["d_exp"] = jnp.repeat(p["ssm_d"][l], SSM_HEAD_DIM)[None, :]
    expand = np.zeros((LANES, BRANCH_W), np.float32)
    for h in range(SSM_HEADS):
        expand[h, h * SSM_HEAD_DIM:(h + 1) * SSM_HEAD_DIM] = 1.0
    w["expand"] = jnp.asarray(expand, dtype=BF16)
    wq = p["mla_w_q_up"][l]
    zq = jnp.zeros((MLA_Q_RANK, MLA_HEADS, LANES - MLA_NOPE), F32)
    zr = jnp.zeros((MLA_Q_RANK, MLA_HEADS, LANES - MLA_ROPE), F32)
    w["mla_wq"] = f(jnp.concatenate([wq[..., :MLA_NOPE], zq, wq[..., MLA_NOPE:], zr], axis=-1)
                    .reshape(MLA_Q_RANK, MLA_HEADS * 256))
    wuk = jnp.transpose(p["mla_w_uk"][l], (1, 2, 0))
    w["mla_wuk"] = f(jnp.pad(wuk, ((0, 0), (0, LANES - MLA_NOPE), (0, 0))))
    w["mla_wuv"] = f(jnp.transpose(p["mla_w_uv"][l], (1, 0, 2)))
    lam_init = 0.8 - 0.6 * math.exp(-0.3 * l)
    lw = p["diff_lambda"][l].astype(F32)
    lam = jnp.exp(jnp.sum(lw[0] * lw[1])) - jnp.exp(jnp.sum(lw[2] * lw[3])) + lam_init
    w["diff_sc"] = jnp.stack([lam, jnp.asarray(1.0 - lam_init, F32)]).astype(F32)
    return w


def _ssm_state_to_pairs(s):
    B = s.shape[0]
    return s.reshape(B, 4, 2, SSM_STATE, SSM_HEAD_DIM).transpose(0, 1, 3, 2, 4).reshape(B, 4, SSM_STATE, 2 * SSM_HEAD_DIM)


def _ssm_state_from_pairs(s):
    B = s.shape[0]
    return s.reshape(B, 4, SSM_STATE, 2, SSM_HEAD_DIM).transpose(0, 1, 3, 2, 4).reshape(B, SSM_HEADS, SSM_STATE, SSM_HEAD_DIM)


def _trunk(x, w_layers, rel_bias, final_norm, rows, valid, pos, conv0, ssm0, ret0, paged):
    B, L, _ = x.shape
    T = B * L
    depth = len(w_layers)
    xt = x.reshape(T, D_MODEL)
    ret_cos, ret_sin = _rope_tables(pos, RET_QK // 2)
    ret_cos, ret_sin = jnp.tile(ret_cos, (1, 4)), jnp.tile(ret_sin, (1, 4))
    mc, ms = _rope_tables(pos, MLA_ROPE // 2)
    mla_cos = jnp.pad(mc, ((0, 0), (0, LANES - MLA_ROPE)))
    mla_sin = jnp.pad(ms, ((0, 0), (0, LANES - MLA_ROPE)))
    nc = L // rows
    outs = []
    for l in range(depth):
        w = w_layers[l]
        xt = _ffn(xt, w["norm_ffn1"], w["ffn1_w1"], w["ffn1_w3"], w["ffn1_w2"])
        pr = _inproj(xt, w["norm_mix"], w["w_in"])
        r3 = lambda a: a.reshape(B, L, a.shape[-1])
        dt = r3(pr["dt"])
        dtT = jnp.transpose(dt[:, :, :SSM_HEADS].reshape(B, nc, rows, SSM_HEADS), (0, 3, 1, 2))
        dtT = jnp.pad(dtT, ((0, 0), (0, 0), (0, 0), (0, CHUNK - rows))).reshape(B, SSM_HEADS, nc * CHUNK)
        dtp = dt if rows == CHUNK else jnp.pad(dt, ((0, 0), (0, CHUNK - rows), (0, 0)))
        y_ssm, conv_new, ssm_new = _ssd(r3(pr["xbc"]), r3(pr["z"]), dtp, dtT, conv0[l], ssm0[l], w, rows, valid)
        y_ret, ret_new = _ret(r3(pr["rq"]), r3(pr["rk"]), r3(pr["rv"]), r3(pr["rg"]), ret_cos, ret_sin, ret0[l],
                              w["ret_norm"], rows, valid)
        dk, dv = r3(pr["dk"]), r3(pr["dv"])
        if paged is None:
            y_diff = _diff_prompt(rel_bias, w["diff_sc"], r3(pr["dq"]), dk, dv, w["diff_subln"])
            y_mla, lat, kr = _mla_prompt(r3(pr["mq"]), r3(pr["mkv"]), mla_cos, mla_sin, w)
        else:
            pt, ck, cv, cc, cr = paged
            y_diff = _diff_sample(pt, rel_bias, w["diff_sc"], r3(pr["dq"]), dk, dv, w["diff_subln"], ck, cv, l)
            y_mla, lat, kr = _mla_sample(pt, r3(pr["mq"]), r3(pr["mkv"]), mla_cos, mla_sin, w, cc, cr, l)
        ys = [a.reshape(T, BRANCH_W) for a in (y_ssm, y_ret, y_diff, y_mla)]
        xt = _merge(xt, ys, pr["gates"], w["w_branch"], w["w_out"])
        fg = final_norm if l == depth - 1 else None
        xt = _ffn(xt, w["norm_ffn2"], w["ffn2_w1"], w["ffn2_w3"], w["ffn2_w2"], fg)
        outs.append((dk, dv, lat, kr, conv_new, ssm_new, ret_new))
    return xt.reshape(B, L, D_MODEL), outs


def kernel(x_prompt, x_sample, cache_diff_k, cache_diff_v, cache_mla_latent, cache_mla_krope, state_ssm_conv, state_ssm, state_retention, page_table, norm_ffn1, ffn1_w1, ffn1_w3, ffn1_w2, norm_mix, w_in, ssm_conv_w, ssm_conv_b, ssm_dt_bias, ssm_a_log, ssm_d, ssm_norm, ret_norm, diff_lambda, diff_subln, mla_q_norm, mla_w_q_up, mla_kv_norm, mla_w_uk, mla_w_uv, rel_bias, w_branch, w_out, norm_ffn2, ffn2_w1, ffn2_w3, ffn2_w2, final_norm):
    p = dict(norm_ffn1=norm_ffn1, ffn1_w1=ffn1_w1, ffn1_w3=ffn1_w3, ffn1_w2=ffn1_w2, norm_mix=norm_mix, w_in=w_in,
             ssm_conv_w=ssm_conv_w, ssm_conv_b=ssm_conv_b, ssm_dt_bias=ssm_dt_bias, ssm_a_log=ssm_a_log, ssm_d=ssm_d,
             ssm_norm=ssm_norm, ret_norm=ret_norm, diff_lambda=diff_lambda, diff_subln=diff_subln,
             mla_q_norm=mla_q_norm, mla_w_q_up=mla_w_q_up, mla_kv_norm=mla_kv_norm, mla_w_uk=mla_w_uk,
             mla_w_uv=mla_w_uv, w_branch=w_branch, w_out=w_out, norm_ffn2=norm_ffn2, ffn2_w1=ffn2_w1,
             ffn2_w3=ffn2_w3, ffn2_w2=ffn2_w2)
    depth = w_in.shape[0]
    w_layers = [_layer_weights(l, p) for l in range(depth)]
    fnorm = final_norm[None, :]
    rel = rel_bias.astype(F32)

    Bp, Lp, _ = x_prompt.shape
    zc = jnp.zeros((Bp, CONV_W - 1, SSM_CONV_DIM), F32)
    zs = jnp.zeros((Bp, 4, LANES, LANES), F32)
    zr = jnp.zeros((Bp, 2 * LANES, LANES), F32)
    y_prompt, rows_p = _trunk(x_prompt, w_layers, rel, fnorm, CHUNK, CHUNK, jnp.arange(Lp, dtype=jnp.int32),
                              [zc] * depth, [zs] * depth, [zr] * depth, None)

    Bs, Ls, _ = x_sample.shape
    n_pages = page_table.shape[1]
    past_len = n_pages * PAGE
    xs = jnp.pad(x_sample, ((0, 0), (0, SROWS - Ls), (0, 0)))
    pos_s = past_len + jnp.arange(SROWS, dtype=jnp.int32)
    n_pool = cache_diff_k.shape[0]
    paged = (page_table.astype(jnp.int32),
             jnp.transpose(cache_diff_k, (0, 1, 3, 4, 5, 2)).reshape(n_pool, depth, 2 * PAGE, LANES),
             cache_diff_v.reshape(n_pool, depth, 2 * PAGE, LANES),
             cache_mla_latent, jnp.transpose(cache_mla_krope, (0, 1, 3, 2)))
    conv_s = [state_ssm_conv[:, l] for l in range(depth)]
    ssm_s = [_ssm_state_to_pairs(state_ssm[:, l]) for l in range(depth)]
    ret_s = [state_retention[:, l].reshape(Bs, 2 * LANES, LANES) for l in range(depth)]
    y_sample, rows_s = _trunk(xs, w_layers, rel, fnorm, SROWS, Ls, pos_s, conv_s, ssm_s, ret_s, paged)

    def collect(rows, B, L, keep):
        st = lambda i: jnp.stack([r[i] for r in rows], axis=1)
        dk = st(0)[:, :, :keep].reshape(B, depth, keep, DIFF_KVH, 2, DIFF_DH)
        dv = st(1)[:, :, :keep].reshape(B, depth, keep, DIFF_KVH, DIFF_DV)
        lat = st(2)[:, :, :keep]
        kr = st(3)[:, :, :keep]
        conv = st(4)
        ssm = jnp.stack([_ssm_state_from_pairs(r[5]) for r in rows], axis=1)
        ret = st(6).reshape(B, depth, RET_HEADS, RET_QK, RET_V)
        return dk, dv, lat, kr, conv, ssm, ret

    return (y_prompt, y_sample[:, :Ls]) + collect(rows_p, Bp, Lp, Lp) + collect(rows_s, Bs, SROWS, Ls)
```
